```python
import math
import jax, jax.numpy as jnp
from jax import lax
import numpy as np

D_MODEL = 1024
BATCH = 8
SEQ = 2048
DEPTH = 1
DEC_BATCH = 128
DEC_SEQ = 1
PAST_LEN = 16384
PAGE_SIZE = 128

GLA_HEADS = 4
GLA_DK = 128
GLA_DV = 256
GLA_GATE_RANK = 16
GLA_GATE_NORMALIZER = 16.0
HGRN_HEADS = 8
HGRN_DK = 128
HGRN_DV = 128
CHUNK = 64
N_EXPERTS = 64
N_GROUPS = 8
TOPK_GROUPS = 4
TOP_K = 8
D_EXPERT = 256
D_SHARED = 256
ROUTED_SCALE = 2.5
DN_ALPHA = (2.0 * DEPTH) ** 0.25
DN_BETA = (8.0 * DEPTH) ** -0.25
EPS = 1e-5

GLA_QK = GLA_HEADS * GLA_DK
GLA_V = GLA_HEADS * GLA_DV
HGRN_K = HGRN_HEADS * HGRN_DK
HGRN_V = HGRN_HEADS * HGRN_DV
IN_SPLITS = (GLA_QK, GLA_QK, GLA_V, GLA_V, GLA_GATE_RANK, HGRN_K, HGRN_K, HGRN_V, HGRN_V, D_MODEL, D_MODEL)
IN_WIDTH = sum(IN_SPLITS)

kernel_name = 'hybrid_gla_hgrn2_moe_deepnorm_adaln_step'


def _split_cols(z):
    out, start = [], 0
    for w in IN_SPLITS:
        out.append(z[..., start:start + w])
        start += w
    return out


def _rmsnorm(x, w):
    xf = x.astype(jnp.float32)
    return xf * lax.rsqrt(jnp.mean(xf * xf, axis=-1, keepdims=True) + EPS) * w.astype(jnp.float32)


def _layernorm(x, g, b):
    xf = x.astype(jnp.float32)
    mu = jnp.mean(xf, axis=-1, keepdims=True)
    var = jnp.mean(jnp.square(xf - mu), axis=-1, keepdims=True)
    return ((xf - mu) * lax.rsqrt(var + EPS) * g.astype(jnp.float32) + b.astype(jnp.float32)).astype(x.dtype)


def _gated_linear_recurrence(q, k, v, log_a, s0):
    B, L, H, K = q.shape
    V = v.shape[-1]
    C = math.gcd(CHUNK, L)
    N = L // C

    def to_chunks(t):
        return t.astype(jnp.float32).reshape(B, N, C, H, t.shape[-1]).transpose(1, 0, 3, 2, 4)

    qc, kc, vc, gc = to_chunks(q), to_chunks(k), to_chunks(v), to_chunks(log_a)
    mask = jnp.tril(jnp.ones((C, C), dtype=bool))[:, :, None]

    def step(s, inp):
        qi, ki, vi, gi = inp
        b = jnp.cumsum(gi, axis=2)
        diff = b[:, :, :, None, :] - b[:, :, None, :, :]
        decay = jnp.exp(jnp.where(mask, diff, -jnp.inf))
        scores = jnp.einsum('bhik,bhjk,bhijk->bhij', qi, ki, decay)
        o = jnp.einsum('bhij,bhjv->bhiv', scores, vi) + jnp.einsum('bhik,bhkv->bhiv', qi * jnp.exp(b), s)
        b_last = b[:, :, -1:, :]
        s_new = s * jnp.exp(b_last[:, :, 0, :, None]) + jnp.einsum('bhjk,bhjv->bhkv', ki * jnp.exp(b_last - b), vi)
        return s_new, o

    s_final, o = lax.scan(step, s0.astype(jnp.float32), (qc, kc, vc, gc))
    o = o.transpose(1, 0, 3, 2, 4).reshape(B, L, H, V)
    return o, s_final.astype(s0.dtype)


def _token_mixer(h, s_gla, s_hgrn, layer, w_in, w_gk2, b_gk, hgrn_lb, gla_norm_w, hgrn_norm_w, w_proj_a, w_proj_b, w_out):
    B, L, _ = h.shape
    z = jnp.einsum('bld,de->ble', h, w_in)
    qa, ka, va, ga, gk_lr, qb, fb, ib, gb, ua, ub = _split_cols(z)
    f32 = jnp.float32
    log_a = jax.nn.log_sigmoid((jnp.einsum('blr,rk->blk', gk_lr, w_gk2) + b_gk).astype(f32)) / GLA_GATE_NORMALIZER
    oa, s_gla_new = _gated_linear_recurrence(
        qa.reshape(B, L, GLA_HEADS, GLA_DK).astype(f32) * GLA_DK ** -0.5,
        ka.reshape(B, L, GLA_HEADS, GLA_DK),
        va.reshape(B, L, GLA_HEADS, GLA_DV),
        log_a.reshape(B, L, GLA_HEADS, GLA_DK), s_gla)
    oa = _rmsnorm(oa, gla_norm_w) * jax.nn.silu(ga.reshape(B, L, GLA_HEADS, GLA_DV).astype(f32))
    lb = jnp.cumsum(jax.nn.softmax(hgrn_lb.astype(f32), axis=0), axis=0)[layer]
    forget = lb + (1.0 - lb) * jax.nn.sigmoid(fb.astype(f32))
    ob, s_hgrn_new = _gated_linear_recurrence(
        jax.nn.silu(qb.astype(f32)).reshape(B, L, HGRN_HEADS, HGRN_DK) * HGRN_DK ** -0.5,
        (1.0 - forget).reshape(B, L, HGRN_HEADS, HGRN_DK),
        ib.reshape(B, L, HGRN_HEADS, HGRN_DV),
        jnp.log(forget).reshape(B, L, HGRN_HEADS, HGRN_DK), s_hgrn)
    ob = _rmsnorm(ob, hgrn_norm_w) * jax.nn.sigmoid(gb.reshape(B, L, HGRN_HEADS, HGRN_DV).astype(f32))
    ya = jnp.einsum('ble,ed->bld', oa.reshape(B, L, GLA_V).astype(h.dtype), w_proj_a)
    yb = jnp.einsum('ble,ed->bld', ob.reshape(B, L, HGRN_V).astype(h.dtype), w_proj_b)
    merged = jax.nn.sigmoid(ua) * ya + jax.nn.sigmoid(ub) * yb
    return jnp.einsum('bld,de->ble', merged, w_out), s_gla_new, s_hgrn_new


def _moe(h, w_router, router_bias, w_exp_gate, w_exp_up, w_exp_down, w_sh_gate, w_sh_up, w_sh_down):
    B, L, D = h.shape
    t = h.reshape(B * L, D)
    scores = jax.nn.sigmoid(jnp.einsum('nd,de->ne', t, w_router).astype(jnp.float32))
    biased = scores + router_bias.astype(jnp.float32)
    grouped = biased.reshape(-1, N_GROUPS, N_EXPERTS // N_GROUPS)
    group_score = jnp.sum(lax.top_k(grouped, 2)[0], axis=-1)
    _, top_groups = lax.top_k(group_score, TOPK_GROUPS)
    group_mask = jnp.sum(jax.nn.one_hot(top_groups, N_GROUPS, dtype=jnp.float32), axis=1) > 0
    expert_mask = jnp.repeat(group_mask, N_EXPERTS // N_GROUPS, axis=1)
    _, idx = lax.top_k(jnp.where(expert_mask, biased, -jnp.inf), TOP_K)
    w = jnp.take_along_axis(scores, idx, axis=1)
    w = w / jnp.sum(w, axis=-1, keepdims=True) * ROUTED_SCALE
    gates = jnp.einsum('nk,nke->ne', w, jax.nn.one_hot(idx, N_EXPERTS, dtype=jnp.float32)).astype(h.dtype)

    def expert(acc, p):
        wg, wu, wd, g = p
        a = jax.nn.silu(t @ wg) * (t @ wu)
        return acc + g[:, None] * (a @ wd), None

    routed, _ = lax.scan(expert, jnp.zeros_like(t), (w_exp_gate, w_exp_up, w_exp_down, gates.T))
    shared = (jax.nn.silu(t @ w_sh_gate) * (t @ w_sh_up)) @ w_sh_down
    return (routed + shared).reshape(B, L, D)


def _trunk(x, c, s_gla, s_hgrn, w_ada, b_ada, w_in, w_gk2, b_gk, hgrn_lb, gla_norm_w, hgrn_norm_w,
           w_proj_a, w_proj_b, w_out, ln1_g, ln1_b, w_router, router_bias, w_exp_gate, w_exp_up,
           w_exp_down, w_sh_gate, w_sh_up, w_sh_down, ln2_g, ln2_b):
    new_gla, new_hgrn = [], []
    for l in range(DEPTH):
        mod = jnp.einsum('bd,de->be', jax.nn.silu(c), w_ada[l]) + b_ada[l]
        sh1, sc1, g1, sh2, sc2, g2 = [m[:, None, :] for m in jnp.split(mod, 6, axis=-1)]
        h = x * (1.0 + sc1) + sh1
        mix, sg, shh = _token_mixer(h, s_gla[l], s_hgrn[l], l, w_in[l], w_gk2[l], b_gk[l], hgrn_lb,
                                    gla_norm_w[l], hgrn_norm_w[l], w_proj_a[l], w_proj_b[l], w_out[l])
        x = _layernorm(DN_ALPHA * x + g1 * mix, ln1_g[l], ln1_b[l])
        h = x * (1.0 + sc2) + sh2
        ffn = _moe(h, w_router[l], router_bias[l], w_exp_gate[l], w_exp_up[l], w_exp_down[l],
                   w_sh_gate[l], w_sh_up[l], w_sh_down[l])
        x = _layernorm(DN_ALPHA * x + g2 * ffn, ln2_g[l], ln2_b[l])
        new_gla.append(sg)
        new_hgrn.append(shh)
    return x, jnp.stack(new_gla), jnp.stack(new_hgrn)


def setup_inputs(seed: int = 0) -> dict:
    key = jax.random.key(seed)
    ks = iter(jax.random.split(key, 40))
    nrm = lambda shape, s: jax.random.normal(next(ks), shape, jnp.float32) * s
    D = D_MODEL
    return {
        'x_prompt': nrm((BATCH, SEQ, D), 1.0),
        'x_sample': nrm((DEC_BATCH, DEC_SEQ, D), 1.0),
        'state_gla': nrm((DEPTH, DEC_BATCH, GLA_HEADS, GLA_DK, GLA_DV), 1.0),
        'state_hgrn': nrm((DEPTH, DEC_BATCH, HGRN_HEADS, HGRN_DK, HGRN_DV), 1.0),
        'c_prompt': nrm((BATCH, D), 1.0),
        'c_sample': nrm((DEC_BATCH, D), 1.0),
        'w_ada': nrm((DEPTH, D, 6 * D), 0.5 * D ** -0.5),
        'b_ada': nrm((DEPTH, 6 * D), 0.02),
        'w_in': nrm((DEPTH, D, IN_WIDTH), D ** -0.5),
        'w_gk2': nrm((DEPTH, GLA_GATE_RANK, GLA_QK), GLA_GATE_RANK ** -0.5),
        'b_gk': nrm((DEPTH, GLA_QK), 0.1),
        'hgrn_lb': nrm((DEPTH + 1, HGRN_K), 1.0),
        'gla_norm_w': 1.0 + nrm((DEPTH, GLA_DV), 0.02),
        'hgrn_norm_w': 1.0 + nrm((DEPTH, HGRN_DV), 0.02),
        'w_proj_a': nrm((DEPTH, GLA_V, D), GLA_V ** -0.5),
        'w_proj_b': nrm((DEPTH, HGRN_V, D), HGRN_V ** -0.5),
        'w_out': nrm((DEPTH, D, D), DN_BETA * D ** -0.5),
        'ln1_g': 1.0 + nrm((DEPTH, D), 0.02),
        'ln1_b': nrm((DEPTH, D), 0.02),
        'w_router': nrm((DEPTH, D, N_EXPERTS), D ** -0.5),
        'router_bias': nrm((DEPTH, N_EXPERTS), 0.01),
        'w_exp_gate': nrm((DEPTH, N_EXPERTS, D, D_EXPERT), D ** -0.5),
        'w_exp_up': nrm((DEPTH, N_EXPERTS, D, D_EXPERT), D ** -0.5),
        'w_exp_down': nrm((DEPTH, N_EXPERTS, D_EXPERT, D), DN_BETA * D_EXPERT ** -0.5),
        'w_sh_gate': nrm((DEPTH, D, D_SHARED), D ** -0.5),
        'w_sh_up': nrm((DEPTH, D, D_SHARED), D ** -0.5),
        'w_sh_down': nrm((DEPTH, D_SHARED, D), DN_BETA * D_SHARED ** -0.5),
        'ln2_g': 1.0 + nrm((DEPTH, D), 0.02),
        'ln2_b': nrm((DEPTH, D), 0.02),
    }


def reference(x_prompt, x_sample, state_gla, state_hgrn, c_prompt, c_sample, w_ada, b_ada, w_in, w_gk2, b_gk,
              hgrn_lb, gla_norm_w, hgrn_norm_w, w_proj_a, w_proj_b, w_out, ln1_g, ln1_b, w_router, router_bias,
              w_exp_gate, w_exp_up, w_exp_down, w_sh_gate, w_sh_up, w_sh_down, ln2_g, ln2_b):
    b_prompt = x_prompt.shape[0]
    zero_gla = jnp.zeros((DEPTH, b_prompt, GLA_HEADS, GLA_DK, GLA_DV), state_gla.dtype)
    zero_hgrn = jnp.zeros((DEPTH, b_prompt, HGRN_HEADS, HGRN_DK, HGRN_DV), state_hgrn.dtype)
    y_prompt, gla_prompt, hgrn_prompt = _trunk(
        x_prompt, c_prompt, zero_gla, zero_hgrn, w_ada, b_ada, w_in, w_gk2, b_gk, hgrn_lb, gla_norm_w,
        hgrn_norm_w, w_proj_a, w_proj_b, w_out, ln1_g, ln1_b, w_router, router_bias, w_exp_gate, w_exp_up,
        w_exp_down, w_sh_gate, w_sh_up, w_sh_down, ln2_g, ln2_b)
    y_sample, gla_sample, hgrn_sample = _trunk(
        x_sample, c_sample, state_gla, state_hgrn, w_ada, b_ada, w_in, w_gk2, b_gk, hgrn_lb, gla_norm_w,
        hgrn_norm_w, w_proj_a, w_proj_b, w_out, ln1_g, ln1_b, w_router, router_bias, w_exp_gate, w_exp_up,
        w_exp_down, w_sh_gate, w_sh_up, w_sh_down, ln2_g, ln2_b)
    return (y_prompt, y_sample, gla_prompt, hgrn_prompt, gla_sample, hgrn_sample)
```

```python
import functools

import jax
import jax.numpy as jnp
from jax import lax
from jax.experimental import pallas as pl
from jax.experimental.pallas import tpu as pltpu

F32 = jnp.float32
BF16 = jnp.bfloat16

GLA_GATE_NORMALIZER = 16.0
N_GROUPS = 8
TOPK_GROUPS = 4
TOP_K = 8
ROUTED_SCALE = 2.5
EPS = 1e-5

SUBLANES = 8
LANES = 128
VMEM_LIMIT_BYTES = 56 * 1024 * 1024

CHUNK = 128
SUB = SUBLANES
NEG_BIG = -1e30


def _cparams(sem):
    return pltpu.CompilerParams(dimension_semantics=sem, vmem_limit_bytes=VMEM_LIMIT_BYTES)


def _dot(a, b):
    return jnp.dot(a, b, preferred_element_type=F32)


def _dot_nt(a, b):
    return lax.dot_general(a, b, (((1,), (1,)), ((), ())), preferred_element_type=F32)


def _silu(x):
    return x * jax.nn.sigmoid(x)


def _log_sigmoid(x):
    return jnp.minimum(x, 0.0) - jnp.log1p(jnp.exp(-jnp.abs(x)))


def _layernorm(r, g, b):
    mu = jnp.mean(r, axis=-1, keepdims=True)
    d = r - mu
    var = jnp.mean(d * d, axis=-1, keepdims=True)
    return d * lax.rsqrt(var + EPS) * g + b


def _ada_kernel(c_ref, w_ref, b_ref, o_ref):
    c = c_ref[...]
    o_ref[...] = _dot(_silu(c).astype(BF16), w_ref[...].astype(BF16)) + b_ref[...]


def _ada_mod(c, w_ada, b_ada):
    R, D = c.shape
    N = w_ada.shape[1]
    tn = D
    return pl.pallas_call(
        _ada_kernel,
        grid=(N // tn,),
        in_specs=[pl.BlockSpec((R, D), lambda j: (0, 0)),
                  pl.BlockSpec((D, tn), lambda j: (0, j)),
                  pl.BlockSpec((1, tn), lambda j: (0, j))],
        out_specs=pl.BlockSpec((R, tn), lambda j: (0, j)),
        out_shape=jax.ShapeDtypeStruct((R, N), F32),
        compiler_params=_cparams(("arbitrary",)),
        name="ada_mod",
    )(c, w_ada, b_ada.reshape(1, N))


def _chunk_masks(C):
    row = lax.broadcasted_iota(jnp.int32, (C, 1), 0)
    ri = lax.broadcasted_iota(jnp.int32, (C, C), 0)
    ci = lax.broadcasted_iota(jnp.int32, (C, C), 1)
    levels = []
    s = SUB
    while s < C:
        right = ((row // s) % 2) == 1
        same_group = (ri // (2 * s)) == (ci // (2 * s))
        levels.append((s, right, same_group))
        s *= 2
    diag = (ri // SUB) == (ci // SUB)
    return row, levels, diag


def _bcast_rows(x, group, idx):
    C, K = x.shape
    G = C // group
    x3 = x.reshape(G, group, K)
    return jnp.broadcast_to(x3[:, idx:idx + 1, :], (G, group, K)).reshape(C, K)


def _chunk_head(q, k, la, v, st, sel, masks):
    C, K = q.shape
    row, levels, diag = masks
    rmod = row % SUB

    x = la
    sh = 1
    while sh < SUB:
        x = x + jnp.where(rmod >= sh, pltpu.roll(x, sh, 0), 0.0)
        sh *= 2
    x_sub = x

    sc = jnp.zeros((C, C), F32)
    for s, right, same_group in levels:
        y = _bcast_rows(x, 2 * s, s - 1)
        f = jnp.exp(jnp.where(right, x, y - x))
        ql = jnp.where(right, q * f, 0.0).astype(BF16)
        kl = jnp.where(right, 0.0, k * f).astype(BF16)
        sc = sc + jnp.where(same_group, _dot_nt(ql, kl), 0.0)
        x = x + jnp.where(right, y, 0.0)
    b = x

    terms = []
    for jj in range(SUB):
        kb = _bcast_rows(k, SUB, jj)
        xb = _bcast_rows(x_sub, SUB, jj)
        e = jnp.where(rmod >= jj, x_sub - xb, NEG_BIG)
        terms.append((q * kb * jnp.exp(e)).astype(BF16))
    d = _dot(jnp.concatenate(terms, axis=1), sel)
    sc = sc + jnp.where(diag, d, 0.0)

    vb = v.astype(BF16)
    o = _dot(sc.astype(BF16), vb) + _dot_nt((q * jnp.exp(b)).astype(BF16), st.astype(BF16))
    b_last = b[C - 1:C, :]
    kd = (k * jnp.exp(b_last - b)).astype(BF16)
    st_new = st * jnp.exp(b_last) + _dot(v.T.astype(BF16), kd)
    return o, st_new


def _recurrence_tile(q_ref, k_ref, la_ref, v_ref, o_ref, st_ref, sel_ref, n_heads, K, V, T):
    C = CHUNK
    masks = _chunk_masks(C)
    sel = sel_ref[...]

    def body(c, carry):
        r0 = pl.multiple_of(c * C, C)
        for h in range(n_heads):
            ks = slice(h * K, (h + 1) * K)
            vs = slice(h * V, (h + 1) * V)
            o, st_new = _chunk_head(q_ref[pl.ds(r0, C), ks], k_ref[pl.ds(r0, C), ks],
                                    la_ref[pl.ds(r0, C), ks], v_ref[pl.ds(r0, C), vs],
                                    st_ref[h], sel, masks)
            o_ref[pl.ds(r0, C), vs] = o
            st_ref[h] = st_new
        return carry

    lax.fori_loop(0, T // C, body, 0)


def _branch_kernel(*refs, kind, n_heads, K, V, T, layer):
    if kind == "gla":
        (x_ref, sh_ref, sc_ref, wq_ref, wk_ref, wv_ref, wg_ref, wgk1_ref, wgk2_ref, bgk_ref,
         nw_ref, wp_ref, sel_ref, y_ref, sout_ref,
         q_s, k_s, la_s, v_s, g_s, o_s, st_s) = refs
    else:
        (x_ref, sh_ref, sc_ref, wq_ref, wk_ref, wv_ref, wg_ref, lb_ref,
         nw_ref, wp_ref, sel_ref, y_ref, sout_ref,
         q_s, k_s, la_s, v_s, g_s, o_s, st_s) = refs
    lt = pl.program_id(1)

    @pl.when(lt == 0)
    def _():
        st_s[...] = jnp.zeros_like(st_s)

    h = (x_ref[0] * (1.0 + sc_ref[0]) + sh_ref[0]).astype(BF16)
    scale = K ** -0.5
    if kind == "gla":
        q_s[...] = _dot(h, wq_ref[...]) * scale
        k_s[...] = _dot(h, wk_ref[...])
        lr = _dot(h, wgk1_ref[...]).astype(BF16)
        la_s[...] = _log_sigmoid(_dot(lr, wgk2_ref[...]) + bgk_ref[...]) * (1.0 / GLA_GATE_NORMALIZER)
    else:
        q_s[...] = _silu(_dot(h, wq_ref[...])) * scale
        lbp = lb_ref[...]
        e = jnp.exp(lbp - jnp.max(lbp, axis=0, keepdims=True))
        lb = jnp.sum(e[:layer + 1], axis=0, keepdims=True) / jnp.sum(e, axis=0, keepdims=True)
        forget = lb + (1.0 - lb) * jax.nn.sigmoid(_dot(h, wk_ref[...]))
        k_s[...] = 1.0 - forget
        la_s[...] = jnp.log(forget)
    v_s[...] = _dot(h, wv_ref[...])
    g_s[...] = _dot(h, wg_ref[...])

    _recurrence_tile(q_s, k_s, la_s, v_s, o_s, st_s, sel_ref, n_heads, K, V, T)

    nw = nw_ref[...]
    outs = []
    for hd in range(n_heads):
        vs = slice(hd * V, (hd + 1) * V)
        o = o_s[:, vs]
        g = g_s[:, vs]
        gate = _silu(g) if kind == "gla" else jax.nn.sigmoid(g)
        o = o * lax.rsqrt(jnp.mean(o * o, axis=-1, keepdims=True) + EPS) * nw * gate
        outs.append(o.astype(BF16))
    y_ref[0] = _dot(jnp.concatenate(outs, axis=1), wp_ref[...])

    @pl.when(lt == pl.num_programs(1) - 1)
    def _():
        for hd in range(n_heads):
            sout_ref[0, hd] = st_s[hd].T


def _sel_matrix(K, C):
    r = jnp.arange(SUB * K, dtype=jnp.int32)[:, None] // K
    c = jnp.arange(C, dtype=jnp.int32)[None, :] % SUB
    return (r == c).astype(BF16)


def _const_spec(shape):
    nd = len(shape)
    return pl.BlockSpec(shape, lambda b, l: (0,) * nd)


def _branch_prompt(kind, x, mod3, weights, norm_w, w_proj, n_heads, K, V, layer):
    B, L, D = x.shape
    T = min(512, L)
    HK, HV = n_heads * K, n_heads * V
    sel = _sel_matrix(K, CHUNK)
    x_spec = pl.BlockSpec((1, T, D), lambda b, l: (b, l, 0))
    sh_spec = pl.BlockSpec((1, 1, D), lambda b, l: (b, 0, 0))
    sc_spec = pl.BlockSpec((1, 1, D), lambda b, l: (b, 0, 1))
    w_specs = [_const_spec(w.shape) for w in weights]
    nw2 = norm_w.reshape(1, V)
    in_specs = [x_spec, sh_spec, sc_spec] + w_specs + [_const_spec(nw2.shape), _const_spec(w_proj.shape),
                                                       _const_spec(sel.shape)]
    kern = functools.partial(_branch_kernel, kind=kind, n_heads=n_heads, K=K, V=V, T=T, layer=layer)
    return pl.pallas_call(
        kern,
        grid=(B, L // T),
        in_specs=in_specs,
        out_specs=[pl.BlockSpec((1, T, D), lambda b, l: (b, l, 0)),
                   pl.BlockSpec((1, n_heads, K, V), lambda b, l: (b, 0, 0, 0))],
        out_shape=[jax.ShapeDtypeStruct((B, L, D), F32),
                   jax.ShapeDtypeStruct((B, n_heads, K, V), F32)],
        scratch_shapes=[pltpu.VMEM((T, HK), F32), pltpu.VMEM((T, HK), F32), pltpu.VMEM((T, HK), F32),
                        pltpu.VMEM((T, HV), F32), pltpu.VMEM((T, HV), F32), pltpu.VMEM((T, HV), F32),
                        pltpu.VMEM((n_heads, V, K), F32)],
        compiler_params=_cparams(("arbitrary", "arbitrary")),
        name=f"{kind}_prompt",
    )(x, mod3, mod3, *weights, nw2, w_proj, sel)


def _sample_kernel(*refs, kind, n_heads, K, V, TB, layer):
    if kind == "gla":
        (x_ref, sh_ref, sc_ref, wq_ref, wk_ref, wv_ref, wg_ref, wgk1_ref, wgk2_ref, bgk_ref,
         nw_ref, wp_ref, s_ref, y_ref, sout_ref, qT_s, kT_s, aT_s, v_s, g_s, o_s) = refs
    else:
        (x_ref, sh_ref, sc_ref, wq_ref, wk_ref, wv_ref, wg_ref, lb_ref,
         nw_ref, wp_ref, s_ref, y_ref, sout_ref, qT_s, kT_s, aT_s, v_s, g_s, o_s) = refs
    step = pl.program_id(0)
    NT = x_ref.shape[0]

    @pl.when(step == 0)
    def _():
        h = (x_ref[...] * (1.0 + sc_ref[...]) + sh_ref[...]).astype(BF16)
        scale = K ** -0.5
        if kind == "gla":
            q = _dot(h, wq_ref[...]) * scale
            k = _dot(h, wk_ref[...])
            lr = _dot(h, wgk1_ref[...]).astype(BF16)
            a = jnp.exp(_log_sigmoid(_dot(lr, wgk2_ref[...]) + bgk_ref[...]) * (1.0 / GLA_GATE_NORMALIZER))
        else:
            q = _silu(_dot(h, wq_ref[...])) * scale
            lbp = lb_ref[...]
            e = jnp.exp(lbp - jnp.max(lbp, axis=0, keepdims=True))
            lb = jnp.sum(e[:layer + 1], axis=0, keepdims=True) / jnp.sum(e, axis=0, keepdims=True)
            a = lb + (1.0 - lb) * jax.nn.sigmoid(_dot(h, wk_ref[...]))
            k = 1.0 - a
        for hd in range(n_heads):
            ks = slice(hd * K, (hd + 1) * K)
            qT_s[ks, :] = q[:, ks].T
            kT_s[ks, :] = k[:, ks].T
            aT_s[ks, :] = a[:, ks].T
        v_s[...] = _dot(h, wv_ref[...])
        g_s[...] = _dot(h, wg_ref[...])

    lane = lax.broadcasted_iota(jnp.int32, (1, NT), 1)
    sub = lax.broadcasted_iota(jnp.int32, (TB, 1), 0)
    t0 = pl.multiple_of(step * TB, TB)
    for hd in range(n_heads):
        ks = slice(hd * K, (hd + 1) * K)
        vs = slice(hd * V, (hd + 1) * V)
        v_rows = v_s[pl.ds(t0, TB), vs]
        o_rows = jnp.zeros((TB, V), F32)
        for j in range(TB):
            pick = lane == t0 + j
            acol = jnp.sum(jnp.where(pick, aT_s[ks, :], 0.0), axis=1, keepdims=True)
            kcol = jnp.sum(jnp.where(pick, kT_s[ks, :], 0.0), axis=1, keepdims=True)
            qcol = jnp.sum(jnp.where(pick, qT_s[ks, :], 0.0), axis=1, keepdims=True)
            s1 = acol * s_ref[j, hd] + kcol * v_rows[j:j + 1, :]
            sout_ref[j, hd] = s1
            o_rows = jnp.where(sub == j, jnp.sum(qcol * s1, axis=0, keepdims=True), o_rows)
        o_s[pl.ds(t0, TB), vs] = o_rows

    @pl.when(step == pl.num_programs(0) - 1)
    def _():
        nw = nw_ref[...]
        outs = []
        for hd in range(n_heads):
            vs = slice(hd * V, (hd + 1) * V)
            o = o_s[:, vs]
            g = g_s[:, vs]
            gate = _silu(g) if kind == "gla" else jax.nn.sigmoid(g)
            o = o * lax.rsqrt(jnp.mean(o * o, axis=-1, keepdims=True) + EPS) * nw * gate
            outs.append(o.astype(BF16))
        y_ref[...] = _dot(jnp.concatenate(outs, axis=1), wp_ref[...])


def _branch_sample(kind, x, mod, weights, norm_w, w_proj, state, layer):
    NT, D = x.shape
    _, n_heads, K, V = state.shape
    HK, HV = n_heads * K, n_heads * V
    TB = SUBLANES
    c1 = lambda s: pl.BlockSpec(s, lambda i: (0,) * len(s))
    nw2 = norm_w.reshape(1, V)
    in_specs = ([c1((NT, D)), pl.BlockSpec((NT, D), lambda i: (0, 0)), pl.BlockSpec((NT, D), lambda i: (0, 1))]
                + [c1(w.shape) for w in weights] + [c1(nw2.shape), c1(w_proj.shape),
                                                    pl.BlockSpec((TB, n_heads, K, V), lambda i: (i, 0, 0, 0))])
    kern = functools.partial(_sample_kernel, kind=kind, n_heads=n_heads, K=K, V=V, TB=TB, layer=layer)
    return pl.pallas_call(
        kern,
        grid=(NT // TB,),
        in_specs=in_specs,
        out_specs=[c1((NT, D)), pl.BlockSpec((TB, n_heads, K, V), lambda i: (i, 0, 0, 0))],
        out_shape=[jax.ShapeDtypeStruct((NT, D), F32), jax.ShapeDtypeStruct(state.shape, F32)],
        scratch_shapes=[pltpu.VMEM((HK, NT), F32), pltpu.VMEM((HK, NT), F32), pltpu.VMEM((HK, NT), F32),
                        pltpu.VMEM((NT, HV), F32), pltpu.VMEM((NT, HV), F32), pltpu.VMEM((NT, HV), F32)],
        compiler_params=_cparams(("arbitrary",)),
        name=f"{kind}_sample",
    )(x, mod, mod, *weights, nw2, w_proj, state)


def _merge_kernel(x_ref, ya_ref, yb_ref, sh_ref, sc_ref, g_ref, wu_ref, wo_ref, lg_ref, lb_ref, o_ref, *, alpha):
    x = x_ref[0]
    D = x.shape[-1]
    h = (x * (1.0 + sc_ref[0]) + sh_ref[0]).astype(BF16)
    u = _dot(h, wu_ref[...])
    merged = jax.nn.sigmoid(u[:, :D]) * ya_ref[0] + jax.nn.sigmoid(u[:, D:]) * yb_ref[0]
    mix = _dot(merged.astype(BF16), wo_ref[...])
    o_ref[0] = _layernorm(alpha * x + g_ref[0] * mix, lg_ref[...], lb_ref[...])


def _mod_specs(mod3, cols, T):
    D = mod3.shape[-1] // 6
    if mod3.shape[1] == 1:
        return [pl.BlockSpec((1, 1, D), functools.partial(lambda b, l, *_, c: (b, 0, c), c=c)) for c in cols]
    return [pl.BlockSpec((1, T, D), functools.partial(lambda b, l, *_, c: (b, l, c), c=c)) for c in cols]


def _merge(x, ya, yb, mod3, wu, w_out, ln_g, ln_b, alpha):
    B, L, D = x.shape
    T = min(512, L)
    tok = pl.BlockSpec((1, T, D), lambda b, l: (b, l, 0))
    return pl.pallas_call(
        functools.partial(_merge_kernel, alpha=alpha),
        grid=(B, L // T),
        in_specs=[tok, tok, tok] + _mod_specs(mod3, (0, 1, 2), T)
        + [_const_spec(wu.shape), _const_spec(w_out.shape), _const_spec((1, D)), _const_spec((1, D))],
        out_specs=tok,
        out_shape=jax.ShapeDtypeStruct((B, L, D), F32),
        compiler_params=_cparams(("arbitrary", "arbitrary")),
        name="merge",
    )(x, ya, yb, mod3, mod3, mod3, wu, w_out, ln_g.reshape(1, D), ln_b.reshape(1, D))


def _first_argmax(vals, iota, n, axis):
    m = jnp.max(vals, axis=axis, keepdims=True)
    idx = jnp.min(jnp.where(vals == m, iota, n), axis=axis, keepdims=True)
    return m, idx


def _router_kernel(x_ref, sh_ref, sc_ref, wrT_ref, bias_ref, o_ref, *, n_experts):
    E = n_experts
    per = E // N_GROUPS
    h = (x_ref[0] * (1.0 + sc_ref[0]) + sh_ref[0]).astype(BF16)
    T = h.shape[0]
    scores = jax.nn.sigmoid(_dot_nt(wrT_ref[...], h))
    biased = scores + bias_ref[...]
    b3 = biased.reshape(N_GROUPS, per, T)
    i3 = lax.broadcasted_iota(jnp.int32, (N_GROUPS, per, T), 1)
    m1, a1 = _first_argmax(b3, i3, per, 1)
    m2 = jnp.max(jnp.where(i3 == a1, -jnp.inf, b3), axis=1, keepdims=True)
    gscore = (m1 + m2).reshape(N_GROUPS, T)
    gi = lax.broadcasted_iota(jnp.int32, (N_GROUPS, T), 0)
    gsel = jnp.zeros((N_GROUPS, T), jnp.bool_)
    for _ in range(TOPK_GROUPS):
        _, a = _first_argmax(gscore, gi, N_GROUPS, 0)
        hit = gi == a
        gsel = jnp.logical_or(gsel, hit)
        gscore = jnp.where(hit, -jnp.inf, gscore)
    emask = jnp.broadcast_to(gsel.reshape(N_GROUPS, 1, T), (N_GROUPS, per, T)).reshape(E, T)
    cand = jnp.where(emask, biased, -jnp.inf)
    ei = lax.broadcasted_iota(jnp.int32, (E, T), 0)
    chosen = jnp.zeros((E, T), jnp.bool_)
    for _ in range(TOP_K):
        _, a = _first_argmax(cand, ei, E, 0)
        hit = ei == a
        chosen = jnp.logical_or(chosen, hit)
        cand = jnp.where(hit, -jnp.inf, cand)
    w = jnp.where(chosen, scores, 0.0)
    w = w / jnp.sum(w, axis=0, keepdims=True) * ROUTED_SCALE
    wpad = jnp.concatenate([w, jnp.zeros((LANES - E, T), F32)], axis=0)
    o_ref[0] = wpad.T


def _router(x1, mod3, wrT, bias):
    B, L, D = x1.shape
    E = wrT.shape[0]
    T = min(512, L)
    tok = pl.BlockSpec((1, T, D), lambda b, l: (b, l, 0))
    return pl.pallas_call(
        functools.partial(_router_kernel, n_experts=E),
        grid=(B, L // T),
        in_specs=[tok] + _mod_specs(mod3, (3, 4), T) + [_const_spec(wrT.shape), _const_spec((E, 1))],
        out_specs=pl.BlockSpec((1, T, LANES), lambda b, l: (b, l, 0)),
        out_shape=jax.ShapeDtypeStruct((B, L, LANES), F32),
        compiler_params=_cparams(("arbitrary", "arbitrary")),
        name="router",
    )(x1, mod3, mod3, wrT, bias.reshape(E, 1))


def _moe_kernel(x_ref, sh_ref, sc_ref, g2_ref, gates_ref, wg_ref, wu_ref, wd_ref,
                sg_ref, su_ref, sd_ref, lg_ref, lb_ref, o_ref, h_s, acc_s, *, alpha):
    e = pl.program_id(2)

    @pl.when(e == 0)
    def _():
        h = (x_ref[0] * (1.0 + sc_ref[0]) + sh_ref[0]).astype(BF16)
        h_s[...] = h
        a = _silu(_dot(h, sg_ref[...])) * _dot(h, su_ref[...])
        acc_s[...] = _dot(a.astype(BF16), sd_ref[...])

    h = h_s[...]
    a = _silu(_dot(h, wg_ref[0])) * _dot(h, wu_ref[0])
    gates = gates_ref[0]
    lane = lax.broadcasted_iota(jnp.int32, gates.shape, 1)
    gcol = jnp.sum(jnp.where(lane == e, gates, 0.0), axis=1, keepdims=True)
    acc_s[...] += _dot((gcol * a).astype(BF16), wd_ref[0])

    @pl.when(e == pl.num_programs(2) - 1)
    def _():
        o_ref[0] = _layernorm(alpha * x_ref[0] + g2_ref[0] * acc_s[...], lg_ref[...], lb_ref[...])


def _moe(x1, mod3, gates, wg, wu, wd, sg, su, sd, ln_g, ln_b, alpha):
    B, L, D = x1.shape
    E, _, DE = wg.shape
    T = min(1024, L)
    tok = pl.BlockSpec((1, T, D), lambda b, l, e: (b, l, 0))
    c3 = lambda s: pl.BlockSpec(s, lambda b, l, e: (0,) * len(s))
    return pl.pallas_call(
        functools.partial(_moe_kernel, alpha=alpha),
        grid=(B, L // T, E),
        in_specs=[tok] + _mod_specs(mod3, (3, 4, 5), T)
        + [pl.BlockSpec((1, T, LANES), lambda b, l, e: (b, l, 0)),
           pl.BlockSpec((1, D, DE), lambda b, l, e: (e, 0, 0)),
           pl.BlockSpec((1, D, DE), lambda b, l, e: (e, 0, 0)),
           pl.BlockSpec((1, DE, D), lambda b, l, e: (e, 0, 0)),
           c3(sg.shape), c3(su.shape), c3(sd.shape), c3((1, D)), c3((1, D))],
        out_specs=tok,
        out_shape=jax.ShapeDtypeStruct((B, L, D), F32),
        scratch_shapes=[pltpu.VMEM((T, D), BF16), pltpu.VMEM((T, D), F32)],
        compiler_params=_cparams(("arbitrary", "arbitrary", "arbitrary")),
        name="moe",
    )(x1, mod3, mod3, mod3, gates, wg, wu, wd, sg, su, sd, ln_g.reshape(1, D), ln_b.reshape(1, D))


def _split_w_in(w_in_l, gla_shape, hgrn_shape, D):
    Hg, Kg, Vg = gla_shape
    Hh, Kh, Vh = hgrn_shape
    rank = w_in_l.shape[1] - (2 * Hg * Kg + 2 * Hg * Vg + 2 * Hh * Kh + 2 * Hh * Vh + 2 * D)
    widths = (Hg * Kg, Hg * Kg, Hg * Vg, Hg * Vg, rank, Hh * Kh, Hh * Kh, Hh * Vh, Hh * Vh, 2 * D)
    out, start = [], 0
    for w in widths:
        out.append(w_in_l[:, start:start + w].astype(BF16))
        start += w
    return out


def kernel(x_prompt, x_sample, state_gla, state_hgrn, c_prompt, c_sample, w_ada, b_ada, w_in, w_gk2, b_gk,
           hgrn_lb, gla_norm_w, hgrn_norm_w, w_proj_a, w_proj_b, w_out, ln1_g, ln1_b, w_router, router_bias,
           w_exp_gate, w_exp_up, w_exp_down, w_sh_gate, w_sh_up, w_sh_down, ln2_g, ln2_b):
    depth = w_in.shape[0]
    BP, L, D = x_prompt.shape
    NS = x_sample.shape[0]
    assert x_sample.shape[1] == 1
    gla_shape = state_gla.shape[2:]
    hgrn_shape = state_hgrn.shape[2:]
    alpha = (2.0 * depth) ** 0.25

    xp = x_prompt
    xs = x_sample.reshape(NS, D)
    c_all = jnp.concatenate([c_prompt, c_sample], axis=0)
    new_gla_p, new_hgrn_p, new_gla_s, new_hgrn_s = [], [], [], []
    for l in range(depth):
        mod = _ada_mod(c_all, w_ada[l], b_ada[l])
        mod_p = mod[:BP].reshape(BP, 1, 6 * D)
        mod_s = mod[BP:]
        (wqa, wka, wva, wga, wgk1, wqb, wfb, wib, wgb, wuab) = _split_w_in(w_in[l], gla_shape, hgrn_shape, D)
        gla_w = [wqa, wka, wva, wga, wgk1, w_gk2[l].astype(BF16), b_gk[l].reshape(1, -1)]
        hgrn_w = [wqb, wfb, wib, wgb, hgrn_lb]
        wpa = w_proj_a[l].astype(BF16)
        wpb = w_proj_b[l].astype(BF16)
        wo = w_out[l].astype(BF16)
        wrT = w_router[l].T.astype(BF16)
        eg, eu, ed = w_exp_gate[l].astype(BF16), w_exp_up[l].astype(BF16), w_exp_down[l].astype(BF16)
        sg, su, sd = w_sh_gate[l].astype(BF16), w_sh_up[l].astype(BF16), w_sh_down[l].astype(BF16)

        def tail(x3, ya, yb, mod3):
            x1 = _merge(x3, ya, yb, mod3, wuab, wo, ln1_g[l], ln1_b[l], alpha)
            gates = _router(x1, mod3, wrT, router_bias[l])
            return _moe(x1, mod3, gates, eg, eu, ed, sg, su, sd, ln2_g[l], ln2_b[l], alpha)

        ya, sg_p = _branch_prompt("gla", xp, mod_p, gla_w, gla_norm_w[l], wpa, *gla_shape, layer=l)
        yb, sh_p = _branch_prompt("hgrn", xp, mod_p, hgrn_w, hgrn_norm_w[l], wpb, *hgrn_shape, layer=l)
        xp = tail(xp, ya, yb, mod_p)
        new_gla_p.append(sg_p)
        new_hgrn_p.append(sh_p)

        ya, sg_s = _branch_sample("gla", xs, mod_s, gla_w, gla_norm_w[l], wpa, state_gla[l], layer=l)
        yb, sh_s = _branch_sample("hgrn", xs, mod_s, hgrn_w, hgrn_norm_w[l], wpb, state_hgrn[l], layer=l)
        xs = tail(xs[None], ya[None], yb[None], mod_s[None])[0]
        new_gla_s.append(sg_s)
        new_hgrn_s.append(sh_s)

    return (xp, xs.reshape(NS, 1, D), jnp.stack(new_gla_p), jnp.stack(new_hgrn_p),
            jnp.stack(new_gla_s), jnp.stack(new_hgrn_s))
```

```python
import functools

import jax
import jax.numpy as jnp
from jax import lax
from jax.experimental import pallas as pl
from jax.experimental.pallas import tpu as pltpu

F32 = jnp.float32
BF16 = jnp.bfloat16

GLA_GATE_NORMALIZER = 16.0
N_GROUPS = 8
TOPK_GROUPS = 4
TOP_K = 8
ROUTED_SCALE = 2.5
EPS = 1e-5

SUBLANES = 8
LANES = 128
VMEM_LIMIT_BYTES = 56 * 1024 * 1024

TOKEN_TILE = 512
MOE_BLOCK = 4096
MOE_ROWS = 256
MOE_ROWS_SMALL = 32
RMW_BATCH = 4
CHUNK = 128
SUB = SUBLANES
NEG_BIG = -1e30


def _cparams(sem):
    return pltpu.CompilerParams(dimension_semantics=sem, vmem_limit_bytes=VMEM_LIMIT_BYTES)


def _dot(a, b):
    return jnp.dot(a, b, preferred_element_type=F32)


def _dot_nt(a, b):
    return lax.dot_general(a, b, (((1,), (1,)), ((), ())), preferred_element_type=F32)


def _silu(x):
    return x * jax.nn.sigmoid(x)


def _log_sigmoid(x):
    return jnp.minimum(x, 0.0) - jnp.log1p(jnp.exp(-jnp.abs(x)))


def _layernorm(r, g, b):
    mu = jnp.mean(r, axis=-1, keepdims=True)
    d = r - mu
    var = jnp.mean(d * d, axis=-1, keepdims=True)
    return d * lax.rsqrt(var + EPS) * g + b


def _ada_kernel(c_ref, w_ref, b_ref, o_ref):
    c = c_ref[...]
    o_ref[...] = _dot(_silu(c).astype(BF16), w_ref[...].astype(BF16)) + b_ref[...]


def _ada_mod(c, w_ada, b_ada):
    R, D = c.shape
    N = w_ada.shape[1]
    tn = D
    return pl.pallas_call(
        _ada_kernel,
        grid=(N // tn,),
        in_specs=[pl.BlockSpec((R, D), lambda j: (0, 0)),
                  pl.BlockSpec((D, tn), lambda j: (0, j)),
                  pl.BlockSpec((1, tn), lambda j: (0, j))],
        out_specs=pl.BlockSpec((R, tn), lambda j: (0, j)),
        out_shape=jax.ShapeDtypeStruct((R, N), F32),
        compiler_params=_cparams(("arbitrary",)),
        name="ada_mod",
    )(c, w_ada, b_ada.reshape(1, N))


def _chunk_masks(C):
    row = lax.broadcasted_iota(jnp.int32, (C, 1), 0)
    ri = lax.broadcasted_iota(jnp.int32, (C, C), 0)
    ci = lax.broadcasted_iota(jnp.int32, (C, C), 1)
    levels = []
    s = SUB
    while s < C:
        right = ((row // s) % 2) == 1
        same_group = (ri // (2 * s)) == (ci // (2 * s))
        levels.append((s, right, same_group))
        s *= 2
    diag = (ri // SUB) == (ci // SUB)
    return row, levels, diag


def _bcast_rows(x, group, idx):
    C, K = x.shape
    G = C // group
    x3 = x.reshape(G, group, K)
    return jnp.broadcast_to(x3[:, idx:idx + 1, :], (G, group, K)).reshape(C, K)


def _chunk_head(q, k, la, v, st, sel, masks):
    C, K = q.shape
    row, levels, diag = masks
    rmod = row % SUB

    x = la
    sh = 1
    while sh < SUB:
        x = x + jnp.where(rmod >= sh, pltpu.roll(x, sh, 0), 0.0)
        sh *= 2
    x_sub = x

    sc = jnp.zeros((C, C), F32)
    for s, right, same_group in levels:
        y = _bcast_rows(x, 2 * s, s - 1)
        f = jnp.exp(jnp.where(right, x, y - x))
        ql = jnp.where(right, q * f, 0.0).astype(BF16)
        kl = jnp.where(right, 0.0, k * f).astype(BF16)
        sc = sc + jnp.where(same_group, _dot_nt(ql, kl), 0.0)
        x = x + jnp.where(right, y, 0.0)
    b = x

    terms = []
    for jj in range(SUB):
        kb = _bcast_rows(k, SUB, jj)
        xb = _bcast_rows(x_sub, SUB, jj)
        e = jnp.where(rmod >= jj, x_sub - xb, NEG_BIG)
        terms.append((q * kb * jnp.exp(e)).astype(BF16))
    d = _dot(jnp.concatenate(terms, axis=1), sel)
    sc = sc + jnp.where(diag, d, 0.0)

    vb = v.astype(BF16)
    o = _dot(sc.astype(BF16), vb) + _dot_nt((q * jnp.exp(b)).astype(BF16), st.astype(BF16))
    b_last = b[C - 1:C, :]
    kd = (k * jnp.exp(b_last - b)).astype(BF16)
    st_new = st * jnp.exp(b_last) + _dot(v.T.astype(BF16), kd)
    return o, st_new


def _recurrence_tile(q_ref, k_ref, la_ref, v_ref, o_ref, st_ref, sel_ref, n_heads, K, V, T):
    C = CHUNK
    masks = _chunk_masks(C)
    sel = sel_ref[...]

    def body(c, carry):
        r0 = pl.multiple_of(c * C, C)
        for h in range(n_heads):
            ks = slice(h * K, (h + 1) * K)
            vs = slice(h * V, (h + 1) * V)
            o, st_new = _chunk_head(q_ref[pl.ds(r0, C), ks], k_ref[pl.ds(r0, C), ks],
                                    la_ref[pl.ds(r0, C), ks], v_ref[pl.ds(r0, C), vs],
                                    st_ref[h], sel, masks)
            o_ref[pl.ds(r0, C), vs] = o
            st_ref[h] = st_new
        return carry

    lax.fori_loop(0, T // C, body, 0)


def _branch_kernel(*refs, kind, n_heads, K, V, T, layer):
    if kind == "gla":
        (x_ref, sh_ref, sc_ref, wq_ref, wk_ref, wv_ref, wg_ref, wgk1_ref, wgk2_ref, bgk_ref,
         nw_ref, wp_ref, sel_ref, y_ref, sout_ref,
         q_s, k_s, la_s, v_s, g_s, o_s, st_s) = refs
    else:
        (x_ref, sh_ref, sc_ref, wq_ref, wk_ref, wv_ref, wg_ref, lb_ref,
         nw_ref, wp_ref, sel_ref, y_ref, sout_ref,
         q_s, k_s, la_s, v_s, g_s, o_s, st_s) = refs
    lt = pl.program_id(1)

    @pl.when(lt == 0)
    def _():
        st_s[...] = jnp.zeros_like(st_s)

    h = (x_ref[0] * (1.0 + sc_ref[0]) + sh_ref[0]).astype(BF16)
    scale = K ** -0.5
    if kind == "gla":
        q_s[...] = _dot(h, wq_ref[...]) * scale
        k_s[...] = _dot(h, wk_ref[...])
        lr = _dot(h, wgk1_ref[...]).astype(BF16)
        la_s[...] = _log_sigmoid(_dot(lr, wgk2_ref[...]) + bgk_ref[...]) * (1.0 / GLA_GATE_NORMALIZER)
    else:
        q_s[...] = _silu(_dot(h, wq_ref[...])) * scale
        lbp = lb_ref[...]
        e = jnp.exp(lbp - jnp.max(lbp, axis=0, keepdims=True))
        lb = jnp.sum(e[:layer + 1], axis=0, keepdims=True) / jnp.sum(e, axis=0, keepdims=True)
        forget = lb + (1.0 - lb) * jax.nn.sigmoid(_dot(h, wk_ref[...]))
        k_s[...] = 1.0 - forget
        la_s[...] = jnp.log(forget)
    v_s[...] = _dot(h, wv_ref[...])
    g_s[...] = _dot(h, wg_ref[...])

    _recurrence_tile(q_s, k_s, la_s, v_s, o_s, st_s, sel_ref, n_heads, K, V, T)

    nw = nw_ref[...]
    outs = []
    for hd in range(n_heads):
        vs = slice(hd * V, (hd + 1) * V)
        o = o_s[:, vs]
        g = g_s[:, vs]
        gate = _silu(g) if kind == "gla" else jax.nn.sigmoid(g)
        o = o * lax.rsqrt(jnp.mean(o * o, axis=-1, keepdims=True) + EPS) * nw * gate
        outs.append(o.astype(BF16))
    y_ref[0] = _dot(jnp.concatenate(outs, axis=1), wp_ref[...])

    @pl.when(lt == pl.num_programs(1) - 1)
    def _():
        for hd in range(n_heads):
            sout_ref[0, hd] = st_s[hd].T


def _sel_matrix(K, C):
    r = jnp.arange(SUB * K, dtype=jnp.int32)[:, None] // K
    c = jnp.arange(C, dtype=jnp.int32)[None, :] % SUB
    return (r == c).astype(BF16)


def _const_spec(shape):
    nd = len(shape)
    return pl.BlockSpec(shape, lambda b, l: (0,) * nd)


def _branch_prompt(kind, x, mod3, weights, norm_w, w_proj, n_heads, K, V, layer):
    B, L, D = x.shape
    T = min(TOKEN_TILE, L)
    HK, HV = n_heads * K, n_heads * V
    sel = _sel_matrix(K, CHUNK)
    x_spec = pl.BlockSpec((1, T, D), lambda b, l: (b, l, 0))
    sh_spec = pl.BlockSpec((1, 1, D), lambda b, l: (b, 0, 0))
    sc_spec = pl.BlockSpec((1, 1, D), lambda b, l: (b, 0, 1))
    w_specs = [_const_spec(w.shape) for w in weights]
    nw2 = norm_w.reshape(1, V)
    in_specs = [x_spec, sh_spec, sc_spec] + w_specs + [_const_spec(nw2.shape), _const_spec(w_proj.shape),
                                                       _const_spec(sel.shape)]
    kern = functools.partial(_branch_kernel, kind=kind, n_heads=n_heads, K=K, V=V, T=T, layer=layer)
    return pl.pallas_call(
        kern,
        grid=(B, L // T),
        in_specs=in_specs,
        out_specs=[pl.BlockSpec((1, T, D), lambda b, l: (b, l, 0)),
                   pl.BlockSpec((1, n_heads, K, V), lambda b, l: (b, 0, 0, 0))],
        out_shape=[jax.ShapeDtypeStruct((B, L, D), F32),
                   jax.ShapeDtypeStruct((B, n_heads, K, V), F32)],
        scratch_shapes=[pltpu.VMEM((T, HK), F32), pltpu.VMEM((T, HK), F32), pltpu.VMEM((T, HK), F32),
                        pltpu.VMEM((T, HV), F32), pltpu.VMEM((T, HV), F32), pltpu.VMEM((T, HV), F32),
                        pltpu.VMEM((n_heads, V, K), F32)],
        compiler_params=_cparams(("arbitrary", "arbitrary")),
        name=f"{kind}_prompt",
    )(x, mod3, mod3, *weights, nw2, w_proj, sel)


def _sample_kernel(*refs, kind, n_heads, K, V, TB, layer):
    if kind == "gla":
        (x_ref, sh_ref, sc_ref, wq_ref, wk_ref, wv_ref, wg_ref, wgk1_ref, wgk2_ref, bgk_ref,
         nw_ref, wp_ref, s_ref, y_ref, sout_ref, qT_s, kT_s, aT_s, v_s, g_s, o_s) = refs
    else:
        (x_ref, sh_ref, sc_ref, wq_ref, wk_ref, wv_ref, wg_ref, lb_ref,
         nw_ref, wp_ref, s_ref, y_ref, sout_ref, qT_s, kT_s, aT_s, v_s, g_s, o_s) = refs
    step = pl.program_id(0)
    NT = x_ref.shape[0]

    @pl.when(step == 0)
    def _():
        h = (x_ref[...] * (1.0 + sc_ref[...]) + sh_ref[...]).astype(BF16)
        scale = K ** -0.5
        if kind == "gla":
            q = _dot(h, wq_ref[...]) * scale
            k = _dot(h, wk_ref[...])
            lr = _dot(h, wgk1_ref[...]).astype(BF16)
            a = jnp.exp(_log_sigmoid(_dot(lr, wgk2_ref[...]) + bgk_ref[...]) * (1.0 / GLA_GATE_NORMALIZER))
        else:
            q = _silu(_dot(h, wq_ref[...])) * scale
            lbp = lb_ref[...]
            e = jnp.exp(lbp - jnp.max(lbp, axis=0, keepdims=True))
            lb = jnp.sum(e[:layer + 1], axis=0, keepdims=True) / jnp.sum(e, axis=0, keepdims=True)
            a = lb + (1.0 - lb) * jax.nn.sigmoid(_dot(h, wk_ref[...]))
            k = 1.0 - a
        for hd in range(n_heads):
            ks = slice(hd * K, (hd + 1) * K)
            qT_s[ks, :] = q[:, ks].T
            kT_s[ks, :] = k[:, ks].T
            aT_s[ks, :] = a[:, ks].T
        v_s[...] = _dot(h, wv_ref[...])
        g_s[...] = _dot(h, wg_ref[...])

    lane = lax.broadcasted_iota(jnp.int32, (1, NT), 1)
    sub = lax.broadcasted_iota(jnp.int32, (TB, 1), 0)
    t0 = pl.multiple_of(step * TB, TB)
    for hd in range(n_heads):
        ks = slice(hd * K, (hd + 1) * K)
        vs = slice(hd * V, (hd + 1) * V)
        v_rows = v_s[pl.ds(t0, TB), vs]
        o_rows = jnp.zeros((TB, V), F32)
        for j in range(TB):
            pick = lane == t0 + j
            acol = jnp.sum(jnp.where(pick, aT_s[ks, :], 0.0), axis=1, keepdims=True)
            kcol = jnp.sum(jnp.where(pick, kT_s[ks, :], 0.0), axis=1, keepdims=True)
            qcol = jnp.sum(jnp.where(pick, qT_s[ks, :], 0.0), axis=1, keepdims=True)
            s1 = acol * s_ref[j, hd] + kcol * v_rows[j:j + 1, :]
            sout_ref[j, hd] = s1
            o_rows = jnp.where(sub == j, jnp.sum(qcol * s1, axis=0, keepdims=True), o_rows)
        o_s[pl.ds(t0, TB), vs] = o_rows

    @pl.when(step == pl.num_programs(0) - 1)
    def _():
        nw = nw_ref[...]
        outs = []
        for hd in range(n_heads):
            vs = slice(hd * V, (hd + 1) * V)
            o = o_s[:, vs]
            g = g_s[:, vs]
            gate = _silu(g) if kind == "gla" else jax.nn.sigmoid(g)
            o = o * lax.rsqrt(jnp.mean(o * o, axis=-1, keepdims=True) + EPS) * nw * gate
            outs.append(o.astype(BF16))
        y_ref[...] = _dot(jnp.concatenate(outs, axis=1), wp_ref[...])


def _branch_sample(kind, x, mod, weights, norm_w, w_proj, state, layer):
    NT, D = x.shape
    _, n_heads, K, V = state.shape
    HK, HV = n_heads * K, n_heads * V
    TB = SUBLANES
    c1 = lambda s: pl.BlockSpec(s, lambda i: (0,) * len(s))
    nw2 = norm_w.reshape(1, V)
    in_specs = ([c1((NT, D)), pl.BlockSpec((NT, D), lambda i: (0, 0)), pl.BlockSpec((NT, D), lambda i: (0, 1))]
                + [c1(w.shape) for w in weights] + [c1(nw2.shape), c1(w_proj.shape),
                                                    pl.BlockSpec((TB, n_heads, K, V), lambda i: (i, 0, 0, 0))])
    kern = functools.partial(_sample_kernel, kind=kind, n_heads=n_heads, K=K, V=V, TB=TB, layer=layer)
    return pl.pallas_call(
        kern,
        grid=(NT // TB,),
        in_specs=in_specs,
        out_specs=[c1((NT, D)), pl.BlockSpec((TB, n_heads, K, V), lambda i: (i, 0, 0, 0))],
        out_shape=[jax.ShapeDtypeStruct((NT, D), F32), jax.ShapeDtypeStruct(state.shape, F32)],
        scratch_shapes=[pltpu.VMEM((HK, NT), F32), pltpu.VMEM((HK, NT), F32), pltpu.VMEM((HK, NT), F32),
                        pltpu.VMEM((NT, HV), F32), pltpu.VMEM((NT, HV), F32), pltpu.VMEM((NT, HV), F32)],
        compiler_params=_cparams(("arbitrary",)),
        name=f"{kind}_sample",
    )(x, mod, mod, *weights, nw2, w_proj, state)


def _merge_kernel(x_ref, ya_ref, yb_ref, sh_ref, sc_ref, g_ref, wu_ref, wo_ref, lg_ref, lb_ref, o_ref, *, alpha):
    x = x_ref[0]
    D = x.shape[-1]
    h = (x * (1.0 + sc_ref[0]) + sh_ref[0]).astype(BF16)
    u = _dot(h, wu_ref[...])
    merged = jax.nn.sigmoid(u[:, :D]) * ya_ref[0] + jax.nn.sigmoid(u[:, D:]) * yb_ref[0]
    mix = _dot(merged.astype(BF16), wo_ref[...])
    o_ref[0] = _layernorm(alpha * x + g_ref[0] * mix, lg_ref[...], lb_ref[...])


def _mod_specs(mod3, cols, T):
    D = mod3.shape[-1] // 6
    if mod3.shape[1] == 1:
        return [pl.BlockSpec((1, 1, D), functools.partial(lambda b, l, *_, c: (b, 0, c), c=c)) for c in cols]
    return [pl.BlockSpec((1, T, D), functools.partial(lambda b, l, *_, c: (b, l, c), c=c)) for c in cols]


def _merge(x, ya, yb, mod3, wu, w_out, ln_g, ln_b, alpha):
    B, L, D = x.shape
    T = min(TOKEN_TILE, L)
    tok = pl.BlockSpec((1, T, D), lambda b, l: (b, l, 0))
    return pl.pallas_call(
        functools.partial(_merge_kernel, alpha=alpha),
        grid=(B, L // T),
        in_specs=[tok, tok, tok] + _mod_specs(mod3, (0, 1, 2), T)
        + [_const_spec(wu.shape), _const_spec(w_out.shape), _const_spec((1, D)), _const_spec((1, D))],
        out_specs=tok,
        out_shape=jax.ShapeDtypeStruct((B, L, D), F32),
        compiler_params=_cparams(("arbitrary", "arbitrary")),
        name="merge",
    )(x, ya, yb, mod3, mod3, mod3, wu, w_out, ln_g.reshape(1, D), ln_b.reshape(1, D))


def _first_argmax(vals, iota, n, axis):
    m = jnp.max(vals, axis=axis, keepdims=True)
    idx = jnp.min(jnp.where(vals == m, iota, n), axis=axis, keepdims=True)
    return m, idx


def _router_kernel(x_ref, sh_ref, sc_ref, wrT_ref, bias_ref, eidx_ref, egate_ref, xg_ref, *, n_experts):
    E = n_experts
    per = E // N_GROUPS
    hf = x_ref[0] * (1.0 + sc_ref[0]) + sh_ref[0]
    h = hf.astype(BF16)
    T = h.shape[0]
    for s in range(hf.shape[1] // LANES):
        xg_ref[pl.ds(s, T, stride=SUBLANES), :] = hf[:, s * LANES:(s + 1) * LANES]
    scores = jax.nn.sigmoid(_dot_nt(wrT_ref[...], h))
    biased = scores + bias_ref[...]
    b3 = biased.reshape(N_GROUPS, per, T)
    i3 = lax.broadcasted_iota(jnp.int32, (N_GROUPS, per, T), 1)
    m1, a1 = _first_argmax(b3, i3, per, 1)
    m2 = jnp.max(jnp.where(i3 == a1, -jnp.inf, b3), axis=1, keepdims=True)
    gscore = (m1 + m2).reshape(N_GROUPS, T)
    gi = lax.broadcasted_iota(jnp.int32, (N_GROUPS, T), 0)
    gsel = jnp.zeros((N_GROUPS, T), jnp.bool_)
    for _ in range(TOPK_GROUPS):
        _, a = _first_argmax(gscore, gi, N_GROUPS, 0)
        hit = gi == a
        gsel = jnp.logical_or(gsel, hit)
        gscore = jnp.where(hit, -jnp.inf, gscore)
    emask = jnp.broadcast_to(gsel.reshape(N_GROUPS, 1, T), (N_GROUPS, per, T)).reshape(E, T)
    cand = jnp.where(emask, biased, -jnp.inf)
    ei = lax.broadcasted_iota(jnp.int32, (E, T), 0)
    picks, weights = [], []
    for _ in range(TOP_K):
        _, a = _first_argmax(cand, ei, E, 0)
        hit = ei == a
        picks.append(a)
        weights.append(jnp.sum(jnp.where(hit, scores, 0.0), axis=0, keepdims=True))
        cand = jnp.where(hit, -jnp.inf, cand)
    w = jnp.concatenate(weights, axis=0)
    egate_ref[...] = w / jnp.sum(w, axis=0, keepdims=True) * ROUTED_SCALE
    eidx_ref[...] = jnp.concatenate(picks, axis=0)


def _router(x1, mod3, wrT, bias):
    B, L, D = x1.shape
    assert D == SUBLANES * LANES
    E = wrT.shape[0]
    T = min(TOKEN_TILE, L)
    nl = L // T
    N = B * L
    tok = pl.BlockSpec((1, T, D), lambda b, l: (b, l, 0))
    pick_spec = pl.BlockSpec((TOP_K, T), lambda b, l: (0, b * nl + l))
    return pl.pallas_call(
        functools.partial(_router_kernel, n_experts=E),
        grid=(B, nl),
        in_specs=[tok] + _mod_specs(mod3, (3, 4), T) + [_const_spec(wrT.shape), _const_spec((E, 1))],
        out_specs=[pick_spec, pick_spec, pl.BlockSpec((T * SUBLANES, LANES), lambda b, l: (b * nl + l, 0))],
        out_shape=[jax.ShapeDtypeStruct((TOP_K, N), jnp.int32), jax.ShapeDtypeStruct((TOP_K, N), F32),
                   jax.ShapeDtypeStruct((N * SUBLANES, LANES), F32)],
        compiler_params=_cparams(("arbitrary", "arbitrary")),
        name="router",
    )(x1, mod3, mod3, wrT, bias.reshape(E, 1))


def _tile_schedule(eidx, egate, NB, E, R):
    Kp, N = eidx.shape
    nb = N // NB
    A = Kp * NB
    S = SUBLANES
    tok = jnp.broadcast_to(jnp.arange(N, dtype=jnp.int32)[None, :], (Kp, N))
    key = (tok // NB) * E + eidx
    skey, stok, sgate = lax.sort((key.reshape(-1), tok.reshape(-1), egate.reshape(-1)), num_keys=1)
    ids = jnp.arange(nb * E + 1, dtype=jnp.int32)
    bounds = jnp.sum((skey[None, :] < ids[:, None]).astype(jnp.int32), axis=1)
    cnt = (bounds[1:] - bounds[:-1]).reshape(nb, E)
    seg0 = bounds[:-1].reshape(nb, E)
    nt = (cnt + R - 1) // R
    cum = jnp.cumsum(nt, axis=1)
    ntiles = cum[:, -1]
    SL = A // R + E + 4
    q = jnp.arange(SL, dtype=jnp.int32)[None, :] - 2
    qc = jnp.clip(q, 0, ntiles[:, None] - 1)
    e_q = jnp.minimum(jnp.sum((cum[:, None, :] <= qc[:, :, None]).astype(jnp.int32), axis=2), E - 1)
    pick = lambda a: jnp.take_along_axis(a, e_q, axis=1)
    j = qc - pick(cum - nt)
    row0 = pick(seg0) + j * R
    real = jnp.logical_and(q >= 0, q < ntiles[:, None])
    nvalid = jnp.where(real, jnp.clip(pick(cnt) - j * R, 0, R), 0)
    stok_p = jnp.concatenate([stok, jnp.zeros((R,), jnp.int32)])
    sgate_p = jnp.concatenate([sgate, jnp.zeros((R,), F32)])
    starts = row0.reshape(-1)
    rows_t = jax.vmap(lambda s: lax.dynamic_slice(stok_p, (s,), (R,)))(starts)
    rows_g = jax.vmap(lambda s: lax.dynamic_slice(sgate_p, (s,), (R,)))(starts)
    valid = jnp.arange(R, dtype=jnp.int32)[None, :] < nvalid.reshape(-1, 1)
    t_loc = (rows_t % NB) * S
    shp = (nb * SL, 1, R)
    src = jnp.where(valid, t_loc, 0).reshape(shp)
    dst = jnp.where(valid, t_loc, NB * S).reshape(shp)
    gate = jnp.where(valid, rows_g, 0.0).reshape(shp)
    return e_q.reshape(-1), ntiles, src, dst, gate, SL


def _moe_step(src_ref, dst_ref, gate_ref, xg_s, acc_s, wg_ref, wu_ref, wd_ref, gbuf, cbuf, cy, sy, R):
    S = SUBLANES
    for r in range(R):
        t0 = pl.multiple_of(src_ref[0, 0, r], S)
        gbuf[r * S:(r + 1) * S, :] = xg_s[pl.ds(t0, S), :]

    x = jnp.concatenate([cbuf[pl.ds(s, R, stride=S), :] for s in range(S)], axis=1).astype(BF16)
    a = _silu(_dot(x, wg_ref[0])) * _dot(x, wu_ref[0])
    y = _dot(a.astype(BF16), wd_ref[0])
    for s in range(S):
        cy[pl.ds(s, R, stride=S), :] = y[:, s * LANES:(s + 1) * LANES]

    for r0 in range(0, R, RMW_BATCH):
        upd = []
        for r in range(r0, r0 + RMW_BATCH):
            a0 = pl.multiple_of(dst_ref[0, 0, r], S)
            upd.append((a0, acc_s[pl.ds(a0, S), :] + gate_ref[0, 0, r] * sy[r * S:(r + 1) * S, :]))
        for a0, val in upd:
            acc_s[pl.ds(a0, S), :] = val


def _moe_kernel(te_ref, nt_ref, src_ref, dst_ref, gate_ref, xg_hbm, wg_ref, wu_ref, wd_ref, out_hbm,
                xg_s, acc_s, buf0, buf1, y0, y1, *, NB, R):
    del te_ref
    b = pl.program_id(0)
    q = pl.program_id(1)

    @pl.when(jnp.logical_and(b == 0, q == 0))
    def _():
        for ref in (buf0, buf1, y0, y1):
            ref[...] = jnp.zeros_like(ref)

    @pl.when(q == 0)
    def _():
        pltpu.sync_copy(xg_hbm.at[b], xg_s)
        acc_s[...] = jnp.zeros_like(acc_s)

    active = q < nt_ref[b] + 2
    args = (src_ref, dst_ref, gate_ref, xg_s, acc_s, wg_ref, wu_ref, wd_ref)

    @pl.when(jnp.logical_and(active, q % 2 == 0))
    def _():
        _moe_step(*args, buf0, buf1, y1, y0, R)

    @pl.when(jnp.logical_and(active, q % 2 == 1))
    def _():
        _moe_step(*args, buf1, buf0, y0, y1, R)

    @pl.when(q == pl.num_programs(1) - 1)
    def _():
        pltpu.sync_copy(acc_s.at[pl.ds(0, NB * SUBLANES)], out_hbm.at[b])


def _moe_routed(xg, eidx, egate, wg, wu, wd):
    N = eidx.shape[1]
    E, D, DE = wg.shape
    NB = min(MOE_BLOCK, N)
    nb = N // NB
    S = SUBLANES
    R = MOE_ROWS if NB >= MOE_BLOCK else MOE_ROWS_SMALL
    te, ntiles, src, dst, gate, SL = _tile_schedule(eidx, egate, NB, E, R)
    smem = lambda shift: pl.BlockSpec((1, 1, R), lambda b, q, *_: (b * SL + q + shift, 0, 0),
                                      memory_space=pltpu.SMEM)
    w_map = lambda b, q, te_ref, nt_ref: (te_ref[b * SL + q + 1], 0, 0)
    tile_rows = pltpu.VMEM((R * S, LANES), F32)
    grid_spec = pltpu.PrefetchScalarGridSpec(
        num_scalar_prefetch=2,
        grid=(nb, SL - 2),
        in_specs=[smem(2), smem(0), smem(0),
                  pl.BlockSpec(memory_space=pl.ANY),
                  pl.BlockSpec((1, D, DE), w_map), pl.BlockSpec((1, D, DE), w_map), pl.BlockSpec((1, DE, D), w_map)],
        out_specs=pl.BlockSpec(memory_space=pl.ANY),
        scratch_shapes=[pltpu.VMEM((NB * S, LANES), F32), pltpu.VMEM(((NB + 1) * S, LANES), F32),
                        tile_rows, tile_rows, tile_rows, tile_rows],
    )
    return pl.pallas_call(
        functools.partial(_moe_kernel, NB=NB, R=R),
        grid_spec=grid_spec,
        out_shape=jax.ShapeDtypeStruct((nb, NB * S, LANES), F32),
        compiler_params=_cparams(("arbitrary", "arbitrary")),
        name="moe_routed",
    )(te, ntiles, src, dst, gate, xg.reshape(nb, NB * S, LANES), wg, wu, wd)


def _combine_kernel(x_ref, sh_ref, sc_ref, g2_ref, r_ref, sg_ref, su_ref, sd_ref, lg_ref, lb_ref, o_ref, *, alpha):
    x = x_ref[0]
    T = x.shape[0]
    h = (x * (1.0 + sc_ref[0]) + sh_ref[0]).astype(BF16)
    a = _silu(_dot(h, sg_ref[...])) * _dot(h, su_ref[...])
    shared = _dot(a.astype(BF16), sd_ref[...])
    routed = jnp.concatenate([r_ref[0, pl.ds(s, T, stride=SUBLANES), :] for s in range(SUBLANES)], axis=1)
    o_ref[0] = _layernorm(alpha * x + g2_ref[0] * (routed + shared), lg_ref[...], lb_ref[...])


def _combine(x1, mod3, routed, sg, su, sd, ln_g, ln_b, alpha):
    B, L, D = x1.shape
    T = min(TOKEN_TILE, L)
    nl = L // T
    NB = routed.shape[1] // SUBLANES
    per = NB // T
    tok = pl.BlockSpec((1, T, D), lambda b, l: (b, l, 0))
    r_spec = pl.BlockSpec((1, T * SUBLANES, LANES), lambda b, l: ((b * nl + l) // per, (b * nl + l) % per, 0))
    return pl.pallas_call(
        functools.partial(_combine_kernel, alpha=alpha),
        grid=(B, nl),
        in_specs=[tok] + _mod_specs(mod3, (3, 4, 5), T) + [r_spec]
        + [_const_spec(sg.shape), _const_spec(su.shape), _const_spec(sd.shape), _const_spec((1, D)),
           _const_spec((1, D))],
        out_specs=tok,
        out_shape=jax.ShapeDtypeStruct((B, L, D), F32),
        compiler_params=_cparams(("arbitrary", "arbitrary")),
        name="combine",
    )(x1, mod3, mod3, mod3, routed, sg, su, sd, ln_g.reshape(1, D), ln_b.reshape(1, D))


def _split_w_in(w_in_l, gla_shape, hgrn_shape, D):
    Hg, Kg, Vg = gla_shape
    Hh, Kh, Vh = hgrn_shape
    rank = w_in_l.shape[1] - (2 * Hg * Kg + 2 * Hg * Vg + 2 * Hh * Kh + 2 * Hh * Vh + 2 * D)
    widths = (Hg * Kg, Hg * Kg, Hg * Vg, Hg * Vg, rank, Hh * Kh, Hh * Kh, Hh * Vh, Hh * Vh, 2 * D)
    out, start = [], 0
    for w in widths:
        out.append(w_in_l[:, start:start + w].astype(BF16))
        start += w
    return out


def kernel(x_prompt, x_sample, state_gla, state_hgrn, c_prompt, c_sample, w_ada, b_ada, w_in, w_gk2, b_gk,
           hgrn_lb, gla_norm_w, hgrn_norm_w, w_proj_a, w_proj_b, w_out, ln1_g, ln1_b, w_router, router_bias,
           w_exp_gate, w_exp_up, w_exp_down, w_sh_gate, w_sh_up, w_sh_down, ln2_g, ln2_b):
    depth = w_in.shape[0]
    BP, L, D = x_prompt.shape
    NS = x_sample.shape[0]
    assert x_sample.shape[1] == 1
    gla_shape = state_gla.shape[2:]
    hgrn_shape = state_hgrn.shape[2:]
    alpha = (2.0 * depth) ** 0.25

    xp = x_prompt
    xs = x_sample.reshape(NS, D)
    c_all = jnp.concatenate([c_prompt, c_sample], axis=0)
    new_gla_p, new_hgrn_p, new_gla_s, new_hgrn_s = [], [], [], []
    for l in range(depth):
        mod = _ada_mod(c_all, w_ada[l], b_ada[l])
        mod_p = mod[:BP].reshape(BP, 1, 6 * D)
        mod_s = mod[BP:]
        (wqa, wka, wva, wga, wgk1, wqb, wfb, wib, wgb, wuab) = _split_w_in(w_in[l], gla_shape, hgrn_shape, D)
        gla_w = [wqa, wka, wva, wga, wgk1, w_gk2[l].astype(BF16), b_gk[l].reshape(1, -1)]
        hgrn_w = [wqb, wfb, wib, wgb, hgrn_lb]
        wpa = w_proj_a[l].astype(BF16)
        wpb = w_proj_b[l].astype(BF16)
        wo = w_out[l].astype(BF16)
        wrT = w_router[l].T.astype(BF16)
        eg, eu, ed = w_exp_gate[l].astype(BF16), w_exp_up[l].astype(BF16), w_exp_down[l].astype(BF16)
        sg, su, sd = w_sh_gate[l].astype(BF16), w_sh_up[l].astype(BF16), w_sh_down[l].astype(BF16)

        def tail(x3, ya, yb, mod3):
            x1 = _merge(x3, ya, yb, mod3, wuab, wo, ln1_g[l], ln1_b[l], alpha)
            eidx, egate, xg = _router(x1, mod3, wrT, router_bias[l])
            routed = _moe_routed(xg, eidx, egate, eg, eu, ed)
            return _combine(x1, mod3, routed, sg, su, sd, ln2_g[l], ln2_b[l], alpha)

        ya, sg_p = _branch_prompt("gla", xp, mod_p, gla_w, gla_norm_w[l], wpa, *gla_shape, layer=l)
        yb, sh_p = _branch_prompt("hgrn", xp, mod_p, hgrn_w, hgrn_norm_w[l], wpb, *hgrn_shape, layer=l)
        xp = tail(xp, ya, yb, mod_p)
        new_gla_p.append(sg_p)
        new_hgrn_p.append(sh_p)

        ya, sg_s = _branch_sample("gla", xs, mod_s, gla_w, gla_norm_w[l], wpa, state_gla[l], layer=l)
        yb, sh_s = _branch_sample("hgrn", xs, mod_s, hgrn_w, hgrn_norm_w[l], wpb, state_hgrn[l], layer=l)
        xs = tail(xs[None], ya[None], yb[None], mod_s[None])[0]
        new_gla_s.append(sg_s)
        new_hgrn_s.append(sh_s)

    return (xp, xs.reshape(NS, 1, D), jnp.stack(new_gla_p), jnp.stack(new_hgrn_p),
            jnp.stack(new_gla_s), jnp.stack(new_hgrn_s))
```

```python
import functools

import jax
import jax.numpy as jnp
from jax import lax
from jax.experimental import pallas as pl
from jax.experimental.pallas import tpu as pltpu

F32 = jnp.float32
BF16 = jnp.bfloat16

GLA_GATE_NORMALIZER = 16.0
N_GROUPS = 8
TOPK_GROUPS = 4
TOP_K = 8
ROUTED_SCALE = 2.5
EPS = 1e-5

SUBLANES = 8
LANES = 128
VMEM_LIMIT_BYTES = 56 * 1024 * 1024

TOKEN_TILE = 512
MOE_BLOCK = 4096
MOE_ROWS = 256
MOE_ROWS_SMALL = 32
RMW_BATCH = 4
CHUNK = 128
SUB = SUBLANES
NEG_BIG = -1e30


def _cparams(sem):
    return pltpu.CompilerParams(dimension_semantics=sem, vmem_limit_bytes=VMEM_LIMIT_BYTES)


def _dot(a, b):
    return jnp.dot(a, b, preferred_element_type=F32)


def _dot_nt(a, b):
    return lax.dot_general(a, b, (((1,), (1,)), ((), ())), preferred_element_type=F32)


def _silu(x):
    return x * jax.nn.sigmoid(x)


def _log_sigmoid(x):
    return jnp.minimum(x, 0.0) - jnp.log1p(jnp.exp(-jnp.abs(x)))


def _layernorm(r, g, b):
    mu = jnp.mean(r, axis=-1, keepdims=True)
    d = r - mu
    var = jnp.mean(d * d, axis=-1, keepdims=True)
    return d * lax.rsqrt(var + EPS) * g + b


def _ada_kernel(c_ref, w_ref, b_ref, o_ref):
    c = c_ref[...]
    o_ref[...] = _dot(_silu(c).astype(BF16), w_ref[...].astype(BF16)) + b_ref[...]


def _ada_mod(c, w_ada, b_ada):
    R, D = c.shape
    N = w_ada.shape[1]
    tn = D
    return pl.pallas_call(
        _ada_kernel,
        grid=(N // tn,),
        in_specs=[pl.BlockSpec((R, D), lambda j: (0, 0)),
                  pl.BlockSpec((D, tn), lambda j: (0, j)),
                  pl.BlockSpec((1, tn), lambda j: (0, j))],
        out_specs=pl.BlockSpec((R, tn), lambda j: (0, j)),
        out_shape=jax.ShapeDtypeStruct((R, N), F32),
        compiler_params=_cparams(("arbitrary",)),
        name="ada_mod",
    )(c, w_ada, b_ada.reshape(1, N))


def _chunk_masks(C):
    row = lax.broadcasted_iota(jnp.int32, (C, 1), 0)
    ri = lax.broadcasted_iota(jnp.int32, (C, C), 0)
    ci = lax.broadcasted_iota(jnp.int32, (C, C), 1)
    levels = []
    s = SUB
    while s < C:
        right = ((row // s) % 2) == 1
        same_group = (ri // (2 * s)) == (ci // (2 * s))
        levels.append((s, right, same_group))
        s *= 2
    diag = (ri // SUB) == (ci // SUB)
    return row, levels, diag


def _bcast_rows(x, group, idx):
    C, K = x.shape
    G = C // group
    x3 = x.reshape(G, group, K)
    return jnp.broadcast_to(x3[:, idx:idx + 1, :], (G, group, K)).reshape(C, K)


def _chunk_head(q, k, la, v, st, sel, masks):
    C, K = q.shape
    row, levels, diag = masks
    rmod = row % SUB

    x = la
    sh = 1
    while sh < SUB:
        x = x + jnp.where(rmod >= sh, pltpu.roll(x, sh, 0), 0.0)
        sh *= 2
    x_sub = x

    sc = jnp.zeros((C, C), F32)
    for s, right, same_group in levels:
        y = _bcast_rows(x, 2 * s, s - 1)
        f = jnp.exp(jnp.where(right, x, y - x))
        ql = jnp.where(right, q * f, 0.0).astype(BF16)
        kl = jnp.where(right, 0.0, k * f).astype(BF16)
        sc = sc + jnp.where(same_group, _dot_nt(ql, kl), 0.0)
        x = x + jnp.where(right, y, 0.0)
    b = x

    terms = []
    for jj in range(SUB):
        kb = _bcast_rows(k, SUB, jj)
        xb = _bcast_rows(x_sub, SUB, jj)
        e = jnp.where(rmod >= jj, x_sub - xb, NEG_BIG)
        terms.append((q * kb * jnp.exp(e)).astype(BF16))
    d = _dot(jnp.concatenate(terms, axis=1), sel)
    sc = sc + jnp.where(diag, d, 0.0)

    vb = v.astype(BF16)
    o = _dot(sc.astype(BF16), vb) + _dot_nt((q * jnp.exp(b)).astype(BF16), st.astype(BF16))
    b_last = b[C - 1:C, :]
    kd = (k * jnp.exp(b_last - b)).astype(BF16)
    st_new = st * jnp.exp(b_last) + _dot(v.T.astype(BF16), kd)
    return o, st_new


def _recurrence_tile(q_ref, k_ref, la_ref, v_ref, o_ref, st_ref, sel_ref, n_heads, K, V, T):
    C = CHUNK
    masks = _chunk_masks(C)
    sel = sel_ref[...]

    def body(c, carry):
        r0 = pl.multiple_of(c * C, C)
        for h in range(n_heads):
            ks = slice(h * K, (h + 1) * K)
            vs = slice(h * V, (h + 1) * V)
            o, st_new = _chunk_head(q_ref[pl.ds(r0, C), ks], k_ref[pl.ds(r0, C), ks],
                                    la_ref[pl.ds(r0, C), ks], v_ref[pl.ds(r0, C), vs],
                                    st_ref[h], sel, masks)
            o_ref[pl.ds(r0, C), vs] = o
            st_ref[h] = st_new
        return carry

    lax.fori_loop(0, T // C, body, 0)


def _branch_kernel(*refs, kind, n_heads, K, V, T, layer):
    if kind == "gla":
        (x_ref, sh_ref, sc_ref, wq_ref, wk_ref, wv_ref, wg_ref, wgk1_ref, wgk2_ref, bgk_ref,
         nw_ref, wp_ref, sel_ref, y_ref, sout_ref,
         q_s, k_s, la_s, v_s, g_s, o_s, st_s) = refs
    else:
        (x_ref, sh_ref, sc_ref, wq_ref, wk_ref, wv_ref, wg_ref, lb_ref,
         nw_ref, wp_ref, sel_ref, y_ref, sout_ref,
         q_s, k_s, la_s, v_s, g_s, o_s, st_s) = refs
    lt = pl.program_id(1)

    @pl.when(lt == 0)
    def _():
        st_s[...] = jnp.zeros_like(st_s)

    h = (x_ref[0] * (1.0 + sc_ref[0]) + sh_ref[0]).astype(BF16)
    scale = K ** -0.5
    if kind == "gla":
        q_s[...] = _dot(h, wq_ref[...]) * scale
        k_s[...] = _dot(h, wk_ref[...])
        lr = _dot(h, wgk1_ref[...]).astype(BF16)
        la_s[...] = _log_sigmoid(_dot(lr, wgk2_ref[...]) + bgk_ref[...]) * (1.0 / GLA_GATE_NORMALIZER)
    else:
        q_s[...] = _silu(_dot(h, wq_ref[...])) * scale
        lbp = lb_ref[...]
        e = jnp.exp(lbp - jnp.max(lbp, axis=0, keepdims=True))
        lb = jnp.sum(e[:layer + 1], axis=0, keepdims=True) / jnp.sum(e, axis=0, keepdims=True)
        forget = lb + (1.0 - lb) * jax.nn.sigmoid(_dot(h, wk_ref[...]))
        k_s[...] = 1.0 - forget
        la_s[...] = jnp.log(forget)
    v_s[...] = _dot(h, wv_ref[...])
    g_s[...] = _dot(h, wg_ref[...])

    _recurrence_tile(q_s, k_s, la_s, v_s, o_s, st_s, sel_ref, n_heads, K, V, T)

    nw = nw_ref[...]
    outs = []
    for hd in range(n_heads):
        vs = slice(hd * V, (hd + 1) * V)
        o = o_s[:, vs]
        g = g_s[:, vs]
        gate = _silu(g) if kind == "gla" else jax.nn.sigmoid(g)
        o = o * lax.rsqrt(jnp.mean(o * o, axis=-1, keepdims=True) + EPS) * nw * gate
        outs.append(o.astype(BF16))
    y_ref[0] = _dot(jnp.concatenate(outs, axis=1), wp_ref[...])

    @pl.when(lt == pl.num_programs(1) - 1)
    def _():
        for hd in range(n_heads):
            sout_ref[0, hd] = st_s[hd].T


def _sel_matrix(K, C):
    r = jnp.arange(SUB * K, dtype=jnp.int32)[:, None] // K
    c = jnp.arange(C, dtype=jnp.int32)[None, :] % SUB
    return (r == c).astype(BF16)


def _const_spec(shape):
    nd = len(shape)
    return pl.BlockSpec(shape, lambda b, l: (0,) * nd)


def _branch_prompt(kind, x, mod3, weights, norm_w, w_proj, n_heads, K, V, layer):
    B, L, D = x.shape
    T = min(TOKEN_TILE, L)
    HK, HV = n_heads * K, n_heads * V
    sel = _sel_matrix(K, CHUNK)
    x_spec = pl.BlockSpec((1, T, D), lambda b, l: (b, l, 0))
    sh_spec = pl.BlockSpec((1, 1, D), lambda b, l: (b, 0, 0))
    sc_spec = pl.BlockSpec((1, 1, D), lambda b, l: (b, 0, 1))
    w_specs = [_const_spec(w.shape) for w in weights]
    nw2 = norm_w.reshape(1, V)
    in_specs = [x_spec, sh_spec, sc_spec] + w_specs + [_const_spec(nw2.shape), _const_spec(w_proj.shape),
                                                       _const_spec(sel.shape)]
    kern = functools.partial(_branch_kernel, kind=kind, n_heads=n_heads, K=K, V=V, T=T, layer=layer)
    return pl.pallas_call(
        kern,
        grid=(B, L // T),
        in_specs=in_specs,
        out_specs=[pl.BlockSpec((1, T, D), lambda b, l: (b, l, 0)),
                   pl.BlockSpec((1, n_heads, K, V), lambda b, l: (b, 0, 0, 0))],
        out_shape=[jax.ShapeDtypeStruct((B, L, D), F32),
                   jax.ShapeDtypeStruct((B, n_heads, K, V), F32)],
        scratch_shapes=[pltpu.VMEM((T, HK), F32), pltpu.VMEM((T, HK), F32), pltpu.VMEM((T, HK), F32),
                        pltpu.VMEM((T, HV), F32), pltpu.VMEM((T, HV), F32), pltpu.VMEM((T, HV), F32),
                        pltpu.VMEM((n_heads, V, K), F32)],
        compiler_params=_cparams(("arbitrary", "arbitrary")),
        name=f"{kind}_prompt",
    )(x, mod3, mod3, *weights, nw2, w_proj, sel)


def _sample_kernel(*refs, kind, n_heads, K, V, TB, layer):
    if kind == "gla":
        (x_ref, sh_ref, sc_ref, wq_ref, wk_ref, wv_ref, wg_ref, wgk1_ref, wgk2_ref, bgk_ref,
         nw_ref, wp_ref, s_ref, y_ref, sout_ref, qT_s, kT_s, aT_s, v_s, g_s, o_s) = refs
    else:
        (x_ref, sh_ref, sc_ref, wq_ref, wk_ref, wv_ref, wg_ref, lb_ref,
         nw_ref, wp_ref, s_ref, y_ref, sout_ref, qT_s, kT_s, aT_s, v_s, g_s, o_s) = refs
    step = pl.program_id(0)
    NT = x_ref.shape[0]

    @pl.when(step == 0)
    def _():
        h = (x_ref[...] * (1.0 + sc_ref[...]) + sh_ref[...]).astype(BF16)
        scale = K ** -0.5
        if kind == "gla":
            q = _dot(h, wq_ref[...]) * scale
            k = _dot(h, wk_ref[...])
            lr = _dot(h, wgk1_ref[...]).astype(BF16)
            a = jnp.exp(_log_sigmoid(_dot(lr, wgk2_ref[...]) + bgk_ref[...]) * (1.0 / GLA_GATE_NORMALIZER))
        else:
            q = _silu(_dot(h, wq_ref[...])) * scale
            lbp = lb_ref[...]
            e = jnp.exp(lbp - jnp.max(lbp, axis=0, keepdims=True))
            lb = jnp.sum(e[:layer + 1], axis=0, keepdims=True) / jnp.sum(e, axis=0, keepdims=True)
            a = lb + (1.0 - lb) * jax.nn.sigmoid(_dot(h, wk_ref[...]))
            k = 1.0 - a
        for hd in range(n_heads):
            ks = slice(hd * K, (hd + 1) * K)
            qT_s[ks, :] = q[:, ks].T
            kT_s[ks, :] = k[:, ks].T
            aT_s[ks, :] = a[:, ks].T
        v_s[...] = _dot(h, wv_ref[...])
        g_s[...] = _dot(h, wg_ref[...])

    lane = lax.broadcasted_iota(jnp.int32, (1, NT), 1)
    sub = lax.broadcasted_iota(jnp.int32, (TB, 1), 0)
    t0 = pl.multiple_of(step * TB, TB)
    for hd in range(n_heads):
        ks = slice(hd * K, (hd + 1) * K)
        vs = slice(hd * V, (hd + 1) * V)
        v_rows = v_s[pl.ds(t0, TB), vs]
        o_rows = jnp.zeros((TB, V), F32)
        for j in range(TB):
            pick = lane == t0 + j
            acol = jnp.sum(jnp.where(pick, aT_s[ks, :], 0.0), axis=1, keepdims=True)
            kcol = jnp.sum(jnp.where(pick, kT_s[ks, :], 0.0), axis=1, keepdims=True)
            qcol = jnp.sum(jnp.where(pick, qT_s[ks, :], 0.0), axis=1, keepdims=True)
            s1 = acol * s_ref[j, hd] + kcol * v_rows[j:j + 1, :]
            sout_ref[j, hd] = s1
            o_rows = jnp.where(sub == j, jnp.sum(qcol * s1, axis=0, keepdims=True), o_rows)
        o_s[pl.ds(t0, TB), vs] = o_rows

    @pl.when(step == pl.num_programs(0) - 1)
    def _():
        nw = nw_ref[...]
        outs = []
        for hd in range(n_heads):
            vs = slice(hd * V, (hd + 1) * V)
            o = o_s[:, vs]
            g = g_s[:, vs]
            gate = _silu(g) if kind == "gla" else jax.nn.sigmoid(g)
            o = o * lax.rsqrt(jnp.mean(o * o, axis=-1, keepdims=True) + EPS) * nw * gate
            outs.append(o.astype(BF16))
        y_ref[...] = _dot(jnp.concatenate(outs, axis=1), wp_ref[...])


def _branch_sample(kind, x, mod, weights, norm_w, w_proj, state, layer):
    NT, D = x.shape
    _, n_heads, K, V = state.shape
    HK, HV = n_heads * K, n_heads * V
    TB = SUBLANES
    c1 = lambda s: pl.BlockSpec(s, lambda i: (0,) * len(s))
    nw2 = norm_w.reshape(1, V)
    in_specs = ([c1((NT, D)), pl.BlockSpec((NT, D), lambda i: (0, 0)), pl.BlockSpec((NT, D), lambda i: (0, 1))]
                + [c1(w.shape) for w in weights] + [c1(nw2.shape), c1(w_proj.shape),
                                                    pl.BlockSpec((TB, n_heads, K, V), lambda i: (i, 0, 0, 0))])
    kern = functools.partial(_sample_kernel, kind=kind, n_heads=n_heads, K=K, V=V, TB=TB, layer=layer)
    return pl.pallas_call(
        kern,
        grid=(NT // TB,),
        in_specs=in_specs,
        out_specs=[c1((NT, D)), pl.BlockSpec((TB, n_heads, K, V), lambda i: (i, 0, 0, 0))],
        out_shape=[jax.ShapeDtypeStruct((NT, D), F32), jax.ShapeDtypeStruct(state.shape, F32)],
        scratch_shapes=[pltpu.VMEM((HK, NT), F32), pltpu.VMEM((HK, NT), F32), pltpu.VMEM((HK, NT), F32),
                        pltpu.VMEM((NT, HV), F32), pltpu.VMEM((NT, HV), F32), pltpu.VMEM((NT, HV), F32)],
        compiler_params=_cparams(("arbitrary",)),
        name=f"{kind}_sample",
    )(x, mod, mod, *weights, nw2, w_proj, state)


def _merge_kernel(x_ref, ya_ref, yb_ref, sh_ref, sc_ref, g_ref, wu_ref, wo_ref, lg_ref, lb_ref, o_ref, *, alpha):
    x = x_ref[0]
    D = x.shape[-1]
    h = (x * (1.0 + sc_ref[0]) + sh_ref[0]).astype(BF16)
    u = _dot(h, wu_ref[...])
    merged = jax.nn.sigmoid(u[:, :D]) * ya_ref[0] + jax.nn.sigmoid(u[:, D:]) * yb_ref[0]
    mix = _dot(merged.astype(BF16), wo_ref[...])
    o_ref[0] = _layernorm(alpha * x + g_ref[0] * mix, lg_ref[...], lb_ref[...])


def _mod_specs(mod3, cols, T):
    D = mod3.shape[-1] // 6
    if mod3.shape[1] == 1:
        return [pl.BlockSpec((1, 1, D), functools.partial(lambda b, l, *_, c: (b, 0, c), c=c)) for c in cols]
    return [pl.BlockSpec((1, T, D), functools.partial(lambda b, l, *_, c: (b, l, c), c=c)) for c in cols]


def _merge(x, ya, yb, mod3, wu, w_out, ln_g, ln_b, alpha):
    B, L, D = x.shape
    T = min(TOKEN_TILE, L)
    tok = pl.BlockSpec((1, T, D), lambda b, l: (b, l, 0))
    return pl.pallas_call(
        functools.partial(_merge_kernel, alpha=alpha),
        grid=(B, L // T),
        in_specs=[tok, tok, tok] + _mod_specs(mod3, (0, 1, 2), T)
        + [_const_spec(wu.shape), _const_spec(w_out.shape), _const_spec((1, D)), _const_spec((1, D))],
        out_specs=tok,
        out_shape=jax.ShapeDtypeStruct((B, L, D), F32),
        compiler_params=_cparams(("arbitrary", "arbitrary")),
        name="merge",
    )(x, ya, yb, mod3, mod3, mod3, wu, w_out, ln_g.reshape(1, D), ln_b.reshape(1, D))


def _first_argmax(vals, iota, n, axis):
    m = jnp.max(vals, axis=axis, keepdims=True)
    idx = jnp.min(jnp.where(vals == m, iota, n), axis=axis, keepdims=True)
    return m, idx


def _router_kernel(x_ref, sh_ref, sc_ref, wrT_ref, bias_ref, eidx_ref, egate_ref, xg_ref, *, n_experts):
    E = n_experts
    per = E // N_GROUPS
    hf = x_ref[0] * (1.0 + sc_ref[0]) + sh_ref[0]
    h = hf.astype(BF16)
    T = h.shape[0]
    for s in range(hf.shape[1] // LANES):
        xg_ref[pl.ds(s, T, stride=SUBLANES), :] = hf[:, s * LANES:(s + 1) * LANES]
    scores = jax.nn.sigmoid(_dot_nt(wrT_ref[...], h))
    biased = scores + bias_ref[...]
    b3 = biased.reshape(N_GROUPS, per, T)
    i3 = lax.broadcasted_iota(jnp.int32, (N_GROUPS, per, T), 1)
    m1, a1 = _first_argmax(b3, i3, per, 1)
    m2 = jnp.max(jnp.where(i3 == a1, -jnp.inf, b3), axis=1, keepdims=True)
    gscore = (m1 + m2).reshape(N_GROUPS, T)
    gi = lax.broadcasted_iota(jnp.int32, (N_GROUPS, T), 0)
    gsel = jnp.zeros((N_GROUPS, T), jnp.bool_)
    for _ in range(TOPK_GROUPS):
        _, a = _first_argmax(gscore, gi, N_GROUPS, 0)
        hit = gi == a
        gsel = jnp.logical_or(gsel, hit)
        gscore = jnp.where(hit, -jnp.inf, gscore)
    emask = jnp.broadcast_to(gsel.reshape(N_GROUPS, 1, T), (N_GROUPS, per, T)).reshape(E, T)
    cand = jnp.where(emask, biased, -jnp.inf)
    ei = lax.broadcasted_iota(jnp.int32, (E, T), 0)
    picks, weights = [], []
    for _ in range(TOP_K):
        _, a = _first_argmax(cand, ei, E, 0)
        hit = ei == a
        picks.append(a)
        weights.append(jnp.sum(jnp.where(hit, scores, 0.0), axis=0, keepdims=True))
        cand = jnp.where(hit, -jnp.inf, cand)
    w = jnp.concatenate(weights, axis=0)
    egate_ref[...] = w / jnp.sum(w, axis=0, keepdims=True) * ROUTED_SCALE
    eidx_ref[...] = jnp.concatenate(picks, axis=0)


def _router(x1, mod3, wrT, bias):
    B, L, D = x1.shape
    assert D == SUBLANES * LANES
    E = wrT.shape[0]
    T = min(TOKEN_TILE, L)
    nl = L // T
    N = B * L
    tok = pl.BlockSpec((1, T, D), lambda b, l: (b, l, 0))
    pick_spec = pl.BlockSpec((TOP_K, T), lambda b, l: (0, b * nl + l))
    return pl.pallas_call(
        functools.partial(_router_kernel, n_experts=E),
        grid=(B, nl),
        in_specs=[tok] + _mod_specs(mod3, (3, 4), T) + [_const_spec(wrT.shape), _const_spec((E, 1))],
        out_specs=[pick_spec, pick_spec, pl.BlockSpec((T * SUBLANES, LANES), lambda b, l: (b * nl + l, 0))],
        out_shape=[jax.ShapeDtypeStruct((TOP_K, N), jnp.int32), jax.ShapeDtypeStruct((TOP_K, N), F32),
                   jax.ShapeDtypeStruct((N * SUBLANES, LANES), F32)],
        compiler_params=_cparams(("arbitrary", "arbitrary")),
        name="router",
    )(x1, mod3, mod3, wrT, bias.reshape(E, 1))


def _tile_schedule(eidx, egate, NB, E, R):
    Kp, N = eidx.shape
    nb = N // NB
    A = Kp * NB
    S = SUBLANES
    assert A % R == 0
    NW = A // R
    tok = jnp.broadcast_to(jnp.arange(N, dtype=jnp.int32)[None, :], (Kp, N))
    key = (tok // NB) * E + eidx
    skey, stok, sgate = lax.sort((key.reshape(-1), tok.reshape(-1), egate.reshape(-1)), num_keys=1)
    w_e = (skey % E).reshape(nb, NW, R)
    w_t = ((stok % NB) * S).reshape(nb, NW, R)
    w_g = sgate.reshape(nb, NW, R)
    first = w_e[:, :, 0]
    npair = w_e[:, :, R - 1] - first + 1
    cum = jnp.cumsum(npair, axis=1)
    ntiles = cum[:, -1]
    SL = NW + E + 3
    q = jnp.arange(SL, dtype=jnp.int32)[None, :] - 2
    qc = jnp.clip(q, 0, ntiles[:, None] - 1)
    k_q = jnp.minimum(jnp.sum((cum[:, None, :] <= qc[:, :, None]).astype(jnp.int32), axis=2), NW - 1)
    onehot = (k_q[:, :, None] == jnp.arange(NW, dtype=jnp.int32)[None, None, :]).astype(F32)
    sel = lambda a: jnp.einsum('bsk,bkr->bsr', onehot, a.astype(F32), precision=lax.Precision.HIGHEST)
    selk = lambda a: jnp.sum(onehot * a.astype(F32)[:, None, :], axis=2).astype(jnp.int32)
    e_q = jnp.clip(selk(first) + qc - selk(cum - npair), 0, E - 1)
    real = jnp.logical_and(q >= 0, q < ntiles[:, None])
    match = jnp.logical_and(real[:, :, None], sel(w_e).astype(jnp.int32) == e_q[:, :, None])
    rows = jnp.where(match, sel(w_t).astype(jnp.int32), NB * S).reshape(nb * SL, 1, R)
    gate = jnp.where(match, sel(w_g), 0.0).reshape(nb * SL, 1, R)
    return e_q.reshape(-1), ntiles, rows, gate, SL


def _moe_step(src_ref, dst_ref, gate_ref, xg_s, acc_s, wg_ref, wu_ref, wd_ref, gbuf, cbuf, cy, sy, R):
    S = SUBLANES
    for r in range(R):
        t0 = pl.multiple_of(src_ref[0, 0, r], S)
        gbuf[r * S:(r + 1) * S, :] = xg_s[pl.ds(t0, S), :]

    x = jnp.concatenate([cbuf[pl.ds(s, R, stride=S), :] for s in range(S)], axis=1).astype(BF16)
    a = _silu(_dot(x, wg_ref[0])) * _dot(x, wu_ref[0])
    y = _dot(a.astype(BF16), wd_ref[0])
    for s in range(S):
        cy[pl.ds(s, R, stride=S), :] = y[:, s * LANES:(s + 1) * LANES]

    for r0 in range(0, R, RMW_BATCH):
        upd = []
        for r in range(r0, r0 + RMW_BATCH):
            a0 = pl.multiple_of(dst_ref[0, 0, r], S)
            upd.append((a0, acc_s[pl.ds(a0, S), :] + gate_ref[0, 0, r] * sy[r * S:(r + 1) * S, :]))
        for a0, val in upd:
            acc_s[pl.ds(a0, S), :] = val


def _moe_kernel(te_ref, nt_ref, src_ref, dst_ref, gate_ref, xg_hbm, wg_ref, wu_ref, wd_ref, out_hbm,
                xg_s, acc_s, buf0, buf1, y0, y1, *, NB, R):
    del te_ref
    b = pl.program_id(0)
    q = pl.program_id(1)

    @pl.when(jnp.logical_and(b == 0, q == 0))
    def _():
        for ref in (buf0, buf1, y0, y1):
            ref[...] = jnp.zeros_like(ref)

    @pl.when(q == 0)
    def _():
        pltpu.sync_copy(xg_hbm.at[b], xg_s.at[pl.ds(0, NB * SUBLANES)])
        xg_s[pl.ds(NB * SUBLANES, SUBLANES), :] = jnp.zeros((SUBLANES, LANES), F32)
        acc_s[...] = jnp.zeros_like(acc_s)

    active = q < nt_ref[b] + 2
    args = (src_ref, dst_ref, gate_ref, xg_s, acc_s, wg_ref, wu_ref, wd_ref)

    @pl.when(jnp.logical_and(active, q % 2 == 0))
    def _():
        _moe_step(*args, buf0, buf1, y1, y0, R)

    @pl.when(jnp.logical_and(active, q % 2 == 1))
    def _():
        _moe_step(*args, buf1, buf0, y0, y1, R)

    @pl.when(q == pl.num_programs(1) - 1)
    def _():
        pltpu.sync_copy(acc_s.at[pl.ds(0, NB * SUBLANES)], out_hbm.at[b])


def _moe_routed(xg, eidx, egate, wg, wu, wd):
    N = eidx.shape[1]
    E, D, DE = wg.shape
    NB = min(MOE_BLOCK, N)
    nb = N // NB
    S = SUBLANES
    R = MOE_ROWS if NB >= MOE_BLOCK else MOE_ROWS_SMALL
    te, ntiles, rows, gate, SL = _tile_schedule(eidx, egate, NB, E, R)
    smem = lambda shift: pl.BlockSpec((1, 1, R), lambda b, q, *_: (b * SL + q + shift, 0, 0),
                                      memory_space=pltpu.SMEM)
    w_map = lambda b, q, te_ref, nt_ref: (te_ref[b * SL + q + 1], 0, 0)
    tile_rows = pltpu.VMEM((R * S, LANES), F32)
    grid_spec = pltpu.PrefetchScalarGridSpec(
        num_scalar_prefetch=2,
        grid=(nb, SL - 2),
        in_specs=[smem(2), smem(0), smem(0),
                  pl.BlockSpec(memory_space=pl.ANY),
                  pl.BlockSpec((1, D, DE), w_map), pl.BlockSpec((1, D, DE), w_map), pl.BlockSpec((1, DE, D), w_map)],
        out_specs=pl.BlockSpec(memory_space=pl.ANY),
        scratch_shapes=[pltpu.VMEM(((NB + 1) * S, LANES), F32), pltpu.VMEM(((NB + 1) * S, LANES), F32),
                        tile_rows, tile_rows, tile_rows, tile_rows],
    )
    return pl.pallas_call(
        functools.partial(_moe_kernel, NB=NB, R=R),
        grid_spec=grid_spec,
        out_shape=jax.ShapeDtypeStruct((nb, NB * S, LANES), F32),
        compiler_params=_cparams(("arbitrary", "arbitrary")),
        name="moe_routed",
    )(te, ntiles, rows, rows, gate, xg.reshape(nb, NB * S, LANES), wg, wu, wd)


def _combine_kernel(x_ref, sh_ref, sc_ref, g2_ref, r_ref, sg_ref, su_ref, sd_ref, lg_ref, lb_ref, o_ref, *, alpha):
    x = x_ref[0]
    T = x.shape[0]
    h = (x * (1.0 + sc_ref[0]) + sh_ref[0]).astype(BF16)
    a = _silu(_dot(h, sg_ref[...])) * _dot(h, su_ref[...])
    shared = _dot(a.astype(BF16), sd_ref[...])
    routed = jnp.concatenate([r_ref[0, pl.ds(s, T, stride=SUBLANES), :] for s in range(SUBLANES)], axis=1)
    o_ref[0] = _layernorm(alpha * x + g2_ref[0] * (routed + shared), lg_ref[...], lb_ref[...])


def _combine(x1, mod3, routed, sg, su, sd, ln_g, ln_b, alpha):
    B, L, D = x1.shape
    T = min(TOKEN_TILE, L)
    nl = L // T
    NB = routed.shape[1] // SUBLANES
    per = NB // T
    tok = pl.BlockSpec((1, T, D), lambda b, l: (b, l, 0))
    r_spec = pl.BlockSpec((1, T * SUBLANES, LANES), lambda b, l: ((b * nl + l) // per, (b * nl + l) % per, 0))
    return pl.pallas_call(
        functools.partial(_combine_kernel, alpha=alpha),
        grid=(B, nl),
        in_specs=[tok] + _mod_specs(mod3, (3, 4, 5), T) + [r_spec]
        + [_const_spec(sg.shape), _const_spec(su.shape), _const_spec(sd.shape), _const_spec((1, D)),
           _const_spec((1, D))],
        out_specs=tok,
        out_shape=jax.ShapeDtypeStruct((B, L, D), F32),
        compiler_params=_cparams(("arbitrary", "arbitrary")),
        name="combine",
    )(x1, mod3, mod3, mod3, routed, sg, su, sd, ln_g.reshape(1, D), ln_b.reshape(1, D))


def _split_w_in(w_in_l, gla_shape, hgrn_shape, D):
    Hg, Kg, Vg = gla_shape
    Hh, Kh, Vh = hgrn_shape
    rank = w_in_l.shape[1] - (2 * Hg * Kg + 2 * Hg * Vg + 2 * Hh * Kh + 2 * Hh * Vh + 2 * D)
    widths = (Hg * Kg, Hg * Kg, Hg * Vg, Hg * Vg, rank, Hh * Kh, Hh * Kh, Hh * Vh, Hh * Vh, 2 * D)
    out, start = [], 0
    for w in widths:
        out.append(w_in_l[:, start:start + w].astype(BF16))
        start += w
    return out


def kernel(x_prompt, x_sample, state_gla, state_hgrn, c_prompt, c_sample, w_ada, b_ada, w_in, w_gk2, b_gk,
           hgrn_lb, gla_norm_w, hgrn_norm_w, w_proj_a, w_proj_b, w_out, ln1_g, ln1_b, w_router, router_bias,
           w_exp_gate, w_exp_up, w_exp_down, w_sh_gate, w_sh_up, w_sh_down, ln2_g, ln2_b):
    depth = w_in.shape[0]
    BP, L, D = x_prompt.shape
    NS = x_sample.shape[0]
    assert x_sample.shape[1] == 1
    gla_shape = state_gla.shape[2:]
    hgrn_shape = state_hgrn.shape[2:]
    alpha = (2.0 * depth) ** 0.25

    xp = x_prompt
    xs = x_sample.reshape(NS, D)
    c_all = jnp.concatenate([c_prompt, c_sample], axis=0)
    new_gla_p, new_hgrn_p, new_gla_s, new_hgrn_s = [], [], [], []
    for l in range(depth):
        mod = _ada_mod(c_all, w_ada[l], b_ada[l])
        mod_p = mod[:BP].reshape(BP, 1, 6 * D)
        mod_s = mod[BP:]
        (wqa, wka, wva, wga, wgk1, wqb, wfb, wib, wgb, wuab) = _split_w_in(w_in[l], gla_shape, hgrn_shape, D)
        gla_w = [wqa, wka, wva, wga, wgk1, w_gk2[l].astype(BF16), b_gk[l].reshape(1, -1)]
        hgrn_w = [wqb, wfb, wib, wgb, hgrn_lb]
        wpa = w_proj_a[l].astype(BF16)
        wpb = w_proj_b[l].astype(BF16)
        wo = w_out[l].astype(BF16)
        wrT = w_router[l].T.astype(BF16)
        eg, eu, ed = w_exp_gate[l].astype(BF16), w_exp_up[l].astype(BF16), w_exp_down[l].astype(BF16)
        sg, su, sd = w_sh_gate[l].astype(BF16), w_sh_up[l].astype(BF16), w_sh_down[l].astype(BF16)

        def tail(x3, ya, yb, mod3):
            x1 = _merge(x3, ya, yb, mod3, wuab, wo, ln1_g[l], ln1_b[l], alpha)
            eidx, egate, xg = _router(x1, mod3, wrT, router_bias[l])
            routed = _moe_routed(xg, eidx, egate, eg, eu, ed)
            return _combine(x1, mod3, routed, sg, su, sd, ln2_g[l], ln2_b[l], alpha)

        ya, sg_p = _branch_prompt("gla", xp, mod_p, gla_w, gla_norm_w[l], wpa, *gla_shape, layer=l)
        yb, sh_p = _branch_prompt("hgrn", xp, mod_p, hgrn_w, hgrn_norm_w[l], wpb, *hgrn_shape, layer=l)
        xp = tail(xp, ya, yb, mod_p)
        new_gla_p.append(sg_p)
        new_hgrn_p.append(sh_p)

        ya, sg_s = _branch_sample("gla", xs, mod_s, gla_w, gla_norm_w[l], wpa, state_gla[l], layer=l)
        yb, sh_s = _branch_sample("hgrn", xs, mod_s, hgrn_w, hgrn_norm_w[l], wpb, state_hgrn[l], layer=l)
        xs = tail(xs[None], ya[None], yb[None], mod_s[None])[0]
        new_gla_s.append(sg_s)
        new_hgrn_s.append(sh_s)

    return (xp, xs.reshape(NS, 1, D), jnp.stack(new_gla_p), jnp.stack(new_hgrn_p),
            jnp.stack(new_gla_s), jnp.stack(new_hgrn_s))
```

```python
import functools

import jax
import jax.numpy as jnp
from jax import lax
from jax.experimental import pallas as pl
from jax.experimental.pallas import tpu as pltpu

F32 = jnp.float32
BF16 = jnp.bfloat16

GLA_GATE_NORMALIZER = 16.0
N_GROUPS = 8
TOPK_GROUPS = 4
TOP_K = 8
ROUTED_SCALE = 2.5
EPS = 1e-5

SUBLANES = 8
LANES = 128
VMEM_LIMIT_BYTES = 56 * 1024 * 1024

TOKEN_TILE = 512
MOE_BLOCK = 4096
MOE_ROWS = 256
MOE_ROWS_SMALL = 32
RMW_BATCH = 4
CHUNK = 128
SUB = SUBLANES
NEG_BIG = -1e30


def _cparams(sem):
    return pltpu.CompilerParams(dimension_semantics=sem, vmem_limit_bytes=VMEM_LIMIT_BYTES)


def _dot(a, b):
    return jnp.dot(a, b, preferred_element_type=F32)


def _dot_nt(a, b):
    return lax.dot_general(a, b, (((1,), (1,)), ((), ())), preferred_element_type=F32)


def _silu(x):
    return x * jax.nn.sigmoid(x)


def _log_sigmoid(x):
    return jnp.minimum(x, 0.0) - jnp.log1p(jnp.exp(-jnp.abs(x)))


def _layernorm(r, g, b):
    mu = jnp.mean(r, axis=-1, keepdims=True)
    d = r - mu
    var = jnp.mean(d * d, axis=-1, keepdims=True)
    return d * lax.rsqrt(var + EPS) * g + b


def _ada_kernel(c_ref, w_ref, b_ref, o_ref):
    c = c_ref[...]
    o_ref[...] = _dot(_silu(c).astype(BF16), w_ref[...].astype(BF16)) + b_ref[...]


def _ada_mod(c, w_ada, b_ada):
    R, D = c.shape
    N = w_ada.shape[1]
    tn = D
    return pl.pallas_call(
        _ada_kernel,
        grid=(N // tn,),
        in_specs=[pl.BlockSpec((R, D), lambda j: (0, 0)),
                  pl.BlockSpec((D, tn), lambda j: (0, j)),
                  pl.BlockSpec((1, tn), lambda j: (0, j))],
        out_specs=pl.BlockSpec((R, tn), lambda j: (0, j)),
        out_shape=jax.ShapeDtypeStruct((R, N), F32),
        compiler_params=_cparams(("arbitrary",)),
        name="ada_mod",
    )(c, w_ada, b_ada.reshape(1, N))


def _chunk_masks(C):
    row = lax.broadcasted_iota(jnp.int32, (C, 1), 0)
    ri = lax.broadcasted_iota(jnp.int32, (C, C), 0)
    ci = lax.broadcasted_iota(jnp.int32, (C, C), 1)
    levels = []
    s = SUB
    while s < C:
        same_group = (ri // (2 * s)) == (ci // (2 * s))
        levels.append((s, same_group))
        s *= 2
    diag = (ri // SUB) == (ci // SUB)
    return row, levels, diag


def _bcast_rows(x, group, idx):
    C, K = x.shape
    G = C // group
    x3 = x.reshape(G, group, K)
    return jnp.broadcast_to(x3[:, idx:idx + 1, :], (G, group, K)).reshape(C, K)


def _chunk_head(q, k, la, v, st, sel, masks):
    C, K = q.shape
    row, levels, diag = masks
    rmod = row % SUB

    x3 = la.reshape(C // SUB, SUB, K)
    sub3 = lax.broadcasted_iota(jnp.int32, (1, SUB, 1), 1)
    sh = 1
    while sh < SUB:
        x3 = x3 + jnp.where(sub3 >= sh, pltpu.roll(x3, sh, 1), 0.0)
        sh *= 2
    x = x3.reshape(C, K)
    x_sub = x

    sc = jnp.zeros((C, C), F32)
    for s, same_group in levels:
        G = C // (2 * s)
        x4 = x.reshape(G, 2, s, K)
        xl, xr = x4[:, 0], x4[:, 1]
        yl = jnp.broadcast_to(xl[:, s - 1:s, :], (G, s, K))
        qr = q.reshape(G, 2, s, K)[:, 1] * jnp.exp(xr)
        kl = k.reshape(G, 2, s, K)[:, 0] * jnp.exp(yl - xl)
        zero = jnp.zeros((G, s, K), F32)
        qf = jnp.stack([zero, qr], axis=1).reshape(C, K).astype(BF16)
        kf = jnp.stack([kl, zero], axis=1).reshape(C, K).astype(BF16)
        sc = sc + jnp.where(same_group, _dot_nt(qf, kf), 0.0)
        x = jnp.stack([xl, xr + yl], axis=1).reshape(C, K)
    b = x

    terms = []
    for jj in range(SUB):
        kb = _bcast_rows(k, SUB, jj)
        xb = _bcast_rows(x_sub, SUB, jj)
        e = jnp.where(rmod >= jj, x_sub - xb, NEG_BIG)
        terms.append((q * kb * jnp.exp(e)).astype(BF16))
    d = _dot(jnp.concatenate(terms, axis=1), sel)
    sc = sc + jnp.where(diag, d, 0.0)

    vb = v.astype(BF16)
    o = _dot(sc.astype(BF16), vb) + _dot_nt((q * jnp.exp(b)).astype(BF16), st.astype(BF16))
    b_last = b[C - 1:C, :]
    kd = (k * jnp.exp(b_last - b)).astype(BF16)
    st_new = st * jnp.exp(b_last) + _dot(v.T.astype(BF16), kd)
    return o, st_new


def _recurrence_tile(q_ref, k_ref, la_ref, v_ref, o_ref, st_ref, sel_ref, n_heads, K, V, T):
    C = CHUNK
    masks = _chunk_masks(C)
    sel = sel_ref[...]

    def body(c, carry):
        r0 = pl.multiple_of(c * C, C)
        for h in range(n_heads):
            ks = slice(h * K, (h + 1) * K)
            vs = slice(h * V, (h + 1) * V)
            o, st_new = _chunk_head(q_ref[pl.ds(r0, C), ks], k_ref[pl.ds(r0, C), ks],
                                    la_ref[pl.ds(r0, C), ks], v_ref[pl.ds(r0, C), vs],
                                    st_ref[h], sel, masks)
            o_ref[pl.ds(r0, C), vs] = o
            st_ref[h] = st_new
        return carry

    lax.fori_loop(0, T // C, body, 0)


def _branch_kernel(*refs, kind, n_heads, K, V, T, layer):
    if kind == "gla":
        (x_ref, sh_ref, sc_ref, wq_ref, wk_ref, wv_ref, wg_ref, wgk1_ref, wgk2_ref, bgk_ref,
         nw_ref, wp_ref, sel_ref, y_ref, sout_ref,
         q_s, k_s, la_s, v_s, g_s, o_s, st_s) = refs
    else:
        (x_ref, sh_ref, sc_ref, wq_ref, wk_ref, wv_ref, wg_ref, lb_ref,
         nw_ref, wp_ref, sel_ref, y_ref, sout_ref,
         q_s, k_s, la_s, v_s, g_s, o_s, st_s) = refs
    lt = pl.program_id(1)

    @pl.when(lt == 0)
    def _():
        st_s[...] = jnp.zeros_like(st_s)

    h = (x_ref[0] * (1.0 + sc_ref[0]) + sh_ref[0]).astype(BF16)
    scale = K ** -0.5
    if kind == "gla":
        q_s[...] = _dot(h, wq_ref[...]) * scale
        k_s[...] = _dot(h, wk_ref[...])
        lr = _dot(h, wgk1_ref[...]).astype(BF16)
        la_s[...] = _log_sigmoid(_dot(lr, wgk2_ref[...]) + bgk_ref[...]) * (1.0 / GLA_GATE_NORMALIZER)
    else:
        q_s[...] = _silu(_dot(h, wq_ref[...])) * scale
        lbp = lb_ref[...]
        e = jnp.exp(lbp - jnp.max(lbp, axis=0, keepdims=True))
        lb = jnp.sum(e[:layer + 1], axis=0, keepdims=True) / jnp.sum(e, axis=0, keepdims=True)
        forget = lb + (1.0 - lb) * jax.nn.sigmoid(_dot(h, wk_ref[...]))
        k_s[...] = 1.0 - forget
        la_s[...] = jnp.log(forget)
    v_s[...] = _dot(h, wv_ref[...])
    g_s[...] = _dot(h, wg_ref[...])

    _recurrence_tile(q_s, k_s, la_s, v_s, o_s, st_s, sel_ref, n_heads, K, V, T)

    nw = nw_ref[...]
    outs = []
    for hd in range(n_heads):
        vs = slice(hd * V, (hd + 1) * V)
        o = o_s[:, vs]
        g = g_s[:, vs]
        gate = _silu(g) if kind == "gla" else jax.nn.sigmoid(g)
        o = o * lax.rsqrt(jnp.mean(o * o, axis=-1, keepdims=True) + EPS) * nw * gate
        outs.append(o.astype(BF16))
    y_ref[0] = _dot(jnp.concatenate(outs, axis=1), wp_ref[...])

    @pl.when(lt == pl.num_programs(1) - 1)
    def _():
        for hd in range(n_heads):
            sout_ref[0, hd] = st_s[hd].T


def _sel_matrix(K, C):
    r = jnp.arange(SUB * K, dtype=jnp.int32)[:, None] // K
    c = jnp.arange(C, dtype=jnp.int32)[None, :] % SUB
    return (r == c).astype(BF16)


def _const_spec(shape):
    nd = len(shape)
    return pl.BlockSpec(shape, lambda b, l: (0,) * nd)


def _branch_prompt(kind, x, mod3, weights, norm_w, w_proj, n_heads, K, V, layer):
    B, L, D = x.shape
    T = min(TOKEN_TILE, L)
    HK, HV = n_heads * K, n_heads * V
    sel = _sel_matrix(K, CHUNK)
    x_spec = pl.BlockSpec((1, T, D), lambda b, l: (b, l, 0))
    sh_spec = pl.BlockSpec((1, 1, D), lambda b, l: (b, 0, 0))
    sc_spec = pl.BlockSpec((1, 1, D), lambda b, l: (b, 0, 1))
    w_specs = [_const_spec(w.shape) for w in weights]
    nw2 = norm_w.reshape(1, V)
    in_specs = [x_spec, sh_spec, sc_spec] + w_specs + [_const_spec(nw2.shape), _const_spec(w_proj.shape),
                                                       _const_spec(sel.shape)]
    kern = functools.partial(_branch_kernel, kind=kind, n_heads=n_heads, K=K, V=V, T=T, layer=layer)
    return pl.pallas_call(
        kern,
        grid=(B, L // T),
        in_specs=in_specs,
        out_specs=[pl.BlockSpec((1, T, D), lambda b, l: (b, l, 0)),
                   pl.BlockSpec((1, n_heads, K, V), lambda b, l: (b, 0, 0, 0))],
        out_shape=[jax.ShapeDtypeStruct((B, L, D), F32),
                   jax.ShapeDtypeStruct((B, n_heads, K, V), F32)],
        scratch_shapes=[pltpu.VMEM((T, HK), F32), pltpu.VMEM((T, HK), F32), pltpu.VMEM((T, HK), F32),
                        pltpu.VMEM((T, HV), F32), pltpu.VMEM((T, HV), F32), pltpu.VMEM((T, HV), F32),
                        pltpu.VMEM((n_heads, V, K), F32)],
        compiler_params=_cparams(("arbitrary", "arbitrary")),
        name=f"{kind}_prompt",
    )(x, mod3, mod3, *weights, nw2, w_proj, sel)


def _sample_kernel(*refs, kind, n_heads, K, V, TB, layer):
    if kind == "gla":
        (x_ref, sh_ref, sc_ref, wq_ref, wk_ref, wv_ref, wg_ref, wgk1_ref, wgk2_ref, bgk_ref,
         nw_ref, wp_ref, s_ref, y_ref, sout_ref, qT_s, kT_s, aT_s, v_s, g_s, o_s) = refs
    else:
        (x_ref, sh_ref, sc_ref, wq_ref, wk_ref, wv_ref, wg_ref, lb_ref,
         nw_ref, wp_ref, s_ref, y_ref, sout_ref, qT_s, kT_s, aT_s, v_s, g_s, o_s) = refs
    step = pl.program_id(0)
    NT = x_ref.shape[0]

    @pl.when(step == 0)
    def _():
        h = (x_ref[...] * (1.0 + sc_ref[...]) + sh_ref[...]).astype(BF16)
        scale = K ** -0.5
        if kind == "gla":
            q = _dot(h, wq_ref[...]) * scale
            k = _dot(h, wk_ref[...])
            lr = _dot(h, wgk1_ref[...]).astype(BF16)
            a = jnp.exp(_log_sigmoid(_dot(lr, wgk2_ref[...]) + bgk_ref[...]) * (1.0 / GLA_GATE_NORMALIZER))
        else:
            q = _silu(_dot(h, wq_ref[...])) * scale
            lbp = lb_ref[...]
            e = jnp.exp(lbp - jnp.max(lbp, axis=0, keepdims=True))
            lb = jnp.sum(e[:layer + 1], axis=0, keepdims=True) / jnp.sum(e, axis=0, keepdims=True)
            a = lb + (1.0 - lb) * jax.nn.sigmoid(_dot(h, wk_ref[...]))
            k = 1.0 - a
        for hd in range(n_heads):
            ks = slice(hd * K, (hd + 1) * K)
            qT_s[ks, :] = q[:, ks].T
            kT_s[ks, :] = k[:, ks].T
            aT_s[ks, :] = a[:, ks].T
        v_s[...] = _dot(h, wv_ref[...])
        g_s[...] = _dot(h, wg_ref[...])

    lane = lax.broadcasted_iota(jnp.int32, (1, NT), 1)
    sub = lax.broadcasted_iota(jnp.int32, (TB, 1), 0)
    t0 = pl.multiple_of(step * TB, TB)
    for hd in range(n_heads):
        ks = slice(hd * K, (hd + 1) * K)
        vs = slice(hd * V, (hd + 1) * V)
        v_rows = v_s[pl.ds(t0, TB), vs]
        o_rows = jnp.zeros((TB, V), F32)
        for j in range(TB):
            pick = lane == t0 + j
            acol = jnp.sum(jnp.where(pick, aT_s[ks, :], 0.0), axis=1, keepdims=True)
            kcol = jnp.sum(jnp.where(pick, kT_s[ks, :], 0.0), axis=1, keepdims=True)
            qcol = jnp.sum(jnp.where(pick, qT_s[ks, :], 0.0), axis=1, keepdims=True)
            s1 = acol * s_ref[j, hd] + kcol * v_rows[j:j + 1, :]
            sout_ref[j, hd] = s1
            o_rows = jnp.where(sub == j, jnp.sum(qcol * s1, axis=0, keepdims=True), o_rows)
        o_s[pl.ds(t0, TB), vs] = o_rows

    @pl.when(step == pl.num_programs(0) - 1)
    def _():
        nw = nw_ref[...]
        outs = []
        for hd in range(n_heads):
            vs = slice(hd * V, (hd + 1) * V)
            o = o_s[:, vs]
            g = g_s[:, vs]
            gate = _silu(g) if kind == "gla" else jax.nn.sigmoid(g)
            o = o * lax.rsqrt(jnp.mean(o * o, axis=-1, keepdims=True) + EPS) * nw * gate
            outs.append(o.astype(BF16))
        y_ref[...] = _dot(jnp.concatenate(outs, axis=1), wp_ref[...])


def _branch_sample(kind, x, mod, weights, norm_w, w_proj, state, layer):
    NT, D = x.shape
    _, n_heads, K, V = state.shape
    HK, HV = n_heads * K, n_heads * V
    TB = SUBLANES
    c1 = lambda s: pl.BlockSpec(s, lambda i: (0,) * len(s))
    nw2 = norm_w.reshape(1, V)
    in_specs = ([c1((NT, D)), pl.BlockSpec((NT, D), lambda i: (0, 0)), pl.BlockSpec((NT, D), lambda i: (0, 1))]
                + [c1(w.shape) for w in weights] + [c1(nw2.shape), c1(w_proj.shape),
                                                    pl.BlockSpec((TB, n_heads, K, V), lambda i: (i, 0, 0, 0))])
    kern = functools.partial(_sample_kernel, kind=kind, n_heads=n_heads, K=K, V=V, TB=TB, layer=layer)
    return pl.pallas_call(
        kern,
        grid=(NT // TB,),
        in_specs=in_specs,
        out_specs=[c1((NT, D)), pl.BlockSpec((TB, n_heads, K, V), lambda i: (i, 0, 0, 0))],
        out_shape=[jax.ShapeDtypeStruct((NT, D), F32), jax.ShapeDtypeStruct(state.shape, F32)],
        scratch_shapes=[pltpu.VMEM((HK, NT), F32), pltpu.VMEM((HK, NT), F32), pltpu.VMEM((HK, NT), F32),
                        pltpu.VMEM((NT, HV), F32), pltpu.VMEM((NT, HV), F32), pltpu.VMEM((NT, HV), F32)],
        compiler_params=_cparams(("arbitrary",)),
        name=f"{kind}_sample",
    )(x, mod, mod, *weights, nw2, w_proj, state)


def _merge_kernel(x_ref, ya_ref, yb_ref, sh_ref, sc_ref, g_ref, wu_ref, wo_ref, lg_ref, lb_ref, o_ref, *, alpha):
    x = x_ref[0]
    D = x.shape[-1]
    h = (x * (1.0 + sc_ref[0]) + sh_ref[0]).astype(BF16)
    u = _dot(h, wu_ref[...])
    merged = jax.nn.sigmoid(u[:, :D]) * ya_ref[0] + jax.nn.sigmoid(u[:, D:]) * yb_ref[0]
    mix = _dot(merged.astype(BF16), wo_ref[...])
    o_ref[0] = _layernorm(alpha * x + g_ref[0] * mix, lg_ref[...], lb_ref[...])


def _mod_specs(mod3, cols, T):
    D = mod3.shape[-1] // 6
    if mod3.shape[1] == 1:
        return [pl.BlockSpec((1, 1, D), functools.partial(lambda b, l, *_, c: (b, 0, c), c=c)) for c in cols]
    return [pl.BlockSpec((1, T, D), functools.partial(lambda b, l, *_, c: (b, l, c), c=c)) for c in cols]


def _merge(x, ya, yb, mod3, wu, w_out, ln_g, ln_b, alpha):
    B, L, D = x.shape
    T = min(TOKEN_TILE, L)
    tok = pl.BlockSpec((1, T, D), lambda b, l: (b, l, 0))
    return pl.pallas_call(
        functools.partial(_merge_kernel, alpha=alpha),
        grid=(B, L // T),
        in_specs=[tok, tok, tok] + _mod_specs(mod3, (0, 1, 2), T)
        + [_const_spec(wu.shape), _const_spec(w_out.shape), _const_spec((1, D)), _const_spec((1, D))],
        out_specs=tok,
        out_shape=jax.ShapeDtypeStruct((B, L, D), F32),
        compiler_params=_cparams(("arbitrary", "arbitrary")),
        name="merge",
    )(x, ya, yb, mod3, mod3, mod3, wu, w_out, ln_g.reshape(1, D), ln_b.reshape(1, D))


def _first_argmax(vals, iota, n, axis):
    m = jnp.max(vals, axis=axis, keepdims=True)
    idx = jnp.min(jnp.where(vals == m, iota, n), axis=axis, keepdims=True)
    return m, idx


def _router_kernel(x_ref, sh_ref, sc_ref, wrT_ref, bias_ref, eidx_ref, egate_ref, xg_ref, *, n_experts):
    E = n_experts
    per = E // N_GROUPS
    hf = x_ref[0] * (1.0 + sc_ref[0]) + sh_ref[0]
    h = hf.astype(BF16)
    T = h.shape[0]
    for s in range(hf.shape[1] // LANES):
        xg_ref[pl.ds(s, T, stride=SUBLANES), :] = hf[:, s * LANES:(s + 1) * LANES]
    scores = jax.nn.sigmoid(_dot_nt(wrT_ref[...], h))
    biased = scores + bias_ref[...]
    b3 = biased.reshape(N_GROUPS, per, T)
    i3 = lax.broadcasted_iota(jnp.int32, (N_GROUPS, per, T), 1)
    m1, a1 = _first_argmax(b3, i3, per, 1)
    m2 = jnp.max(jnp.where(i3 == a1, -jnp.inf, b3), axis=1, keepdims=True)
    gscore = (m1 + m2).reshape(N_GROUPS, T)
    gi = lax.broadcasted_iota(jnp.int32, (N_GROUPS, T), 0)
    gsel = jnp.zeros((N_GROUPS, T), jnp.bool_)
    for _ in range(TOPK_GROUPS):
        _, a = _first_argmax(gscore, gi, N_GROUPS, 0)
        hit = gi == a
        gsel = jnp.logical_or(gsel, hit)
        gscore = jnp.where(hit, -jnp.inf, gscore)
    emask = jnp.broadcast_to(gsel.reshape(N_GROUPS, 1, T), (N_GROUPS, per, T)).reshape(E, T)
    cand = jnp.where(emask, biased, -jnp.inf)
    ei = lax.broadcasted_iota(jnp.int32, (E, T), 0)
    picks, weights = [], []
    for _ in range(TOP_K):
        _, a = _first_argmax(cand, ei, E, 0)
        hit = ei == a
        picks.append(a)
        weights.append(jnp.sum(jnp.where(hit, scores, 0.0), axis=0, keepdims=True))
        cand = jnp.where(hit, -jnp.inf, cand)
    w = jnp.concatenate(weights, axis=0)
    egate_ref[...] = w / jnp.sum(w, axis=0, keepdims=True) * ROUTED_SCALE
    eidx_ref[...] = jnp.concatenate(picks, axis=0)


def _router(x1, mod3, wrT, bias):
    B, L, D = x1.shape
    assert D == SUBLANES * LANES
    E = wrT.shape[0]
    T = min(TOKEN_TILE, L)
    nl = L // T
    N = B * L
    tok = pl.BlockSpec((1, T, D), lambda b, l: (b, l, 0))
    pick_spec = pl.BlockSpec((TOP_K, T), lambda b, l: (0, b * nl + l))
    return pl.pallas_call(
        functools.partial(_router_kernel, n_experts=E),
        grid=(B, nl),
        in_specs=[tok] + _mod_specs(mod3, (3, 4), T) + [_const_spec(wrT.shape), _const_spec((E, 1))],
        out_specs=[pick_spec, pick_spec, pl.BlockSpec((T * SUBLANES, LANES), lambda b, l: (b * nl + l, 0))],
        out_shape=[jax.ShapeDtypeStruct((TOP_K, N), jnp.int32), jax.ShapeDtypeStruct((TOP_K, N), F32),
                   jax.ShapeDtypeStruct((N * SUBLANES, LANES), F32)],
        compiler_params=_cparams(("arbitrary", "arbitrary")),
        name="router",
    )(x1, mod3, mod3, wrT, bias.reshape(E, 1))


def _tile_schedule(eidx, egate, NB, E, R):
    Kp, N = eidx.shape
    nb = N // NB
    A = Kp * NB
    S = SUBLANES
    assert A % R == 0
    NW = A // R
    per_block = lambda a: a.reshape(Kp, nb, NB).transpose(1, 0, 2).reshape(nb, A)
    tok = jnp.broadcast_to(jnp.arange(NB, dtype=jnp.int32)[None, None, :], (Kp, nb, NB)).reshape(Kp, N)
    skey, stok, sgate = lax.sort((per_block(eidx), per_block(tok), per_block(egate)), dimension=1, num_keys=1)
    w_e = skey.reshape(nb, NW, R)
    w_t = (stok * S).reshape(nb, NW, R)
    w_g = sgate.reshape(nb, NW, R)
    first = w_e[:, :, 0]
    npair = w_e[:, :, R - 1] - first + 1
    cum = jnp.cumsum(npair, axis=1)
    ntiles = cum[:, -1]
    SL = NW + E + 3
    q = jnp.arange(SL, dtype=jnp.int32)[None, :] - 2
    qc = jnp.clip(q, 0, ntiles[:, None] - 1)
    k_q = jnp.minimum(jnp.sum((cum[:, None, :] <= qc[:, :, None]).astype(jnp.int32), axis=2), NW - 1)
    onehot = (k_q[:, :, None] == jnp.arange(NW, dtype=jnp.int32)[None, None, :]).astype(F32)
    sel = lambda a: jnp.einsum('bsk,bkr->bsr', onehot, a.astype(F32), precision=lax.Precision.HIGHEST)
    selk = lambda a: jnp.sum(onehot * a.astype(F32)[:, None, :], axis=2).astype(jnp.int32)
    e_q = jnp.clip(selk(first) + qc - selk(cum - npair), 0, E - 1)
    real = jnp.logical_and(q >= 0, q < ntiles[:, None])
    match = jnp.logical_and(real[:, :, None], sel(w_e).astype(jnp.int32) == e_q[:, :, None])
    rows = jnp.where(match, sel(w_t).astype(jnp.int32), NB * S).reshape(nb * SL, 1, R)
    gate = jnp.where(match, sel(w_g), 0.0).reshape(nb * SL, 1, R)
    return e_q.reshape(-1), ntiles, rows, gate, SL


def _moe_step(src_ref, dst_ref, gate_ref, xg_s, acc_s, wg_ref, wu_ref, wd_ref, gbuf, cbuf, cy, sy, R):
    S = SUBLANES
    for r in range(R):
        t0 = pl.multiple_of(src_ref[0, 0, r], S)
        gbuf[r * S:(r + 1) * S, :] = xg_s[pl.ds(t0, S), :]

    x = jnp.concatenate([cbuf[pl.ds(s, R, stride=S), :] for s in range(S)], axis=1).astype(BF16)
    a = _silu(_dot(x, wg_ref[0])) * _dot(x, wu_ref[0])
    y = _dot(a.astype(BF16), wd_ref[0])
    for s in range(S):
        cy[pl.ds(s, R, stride=S), :] = y[:, s * LANES:(s + 1) * LANES]

    for r0 in range(0, R, RMW_BATCH):
        upd = []
        for r in range(r0, r0 + RMW_BATCH):
            a0 = pl.multiple_of(dst_ref[0, 0, r], S)
            upd.append((a0, acc_s[pl.ds(a0, S), :] + gate_ref[0, 0, r] * sy[r * S:(r + 1) * S, :]))
        for a0, val in upd:
            acc_s[pl.ds(a0, S), :] = val


def _moe_kernel(te_ref, nt_ref, src_ref, dst_ref, gate_ref, xg_hbm, wg_ref, wu_ref, wd_ref, out_hbm,
                xg_s, acc_s, buf0, buf1, y0, y1, *, NB, R):
    del te_ref
    b = pl.program_id(0)
    q = pl.program_id(1)

    @pl.when(jnp.logical_and(b == 0, q == 0))
    def _():
        for ref in (buf0, buf1, y0, y1):
            ref[...] = jnp.zeros_like(ref)

    @pl.when(q == 0)
    def _():
        pltpu.sync_copy(xg_hbm.at[b], xg_s.at[pl.ds(0, NB * SUBLANES)])
        xg_s[pl.ds(NB * SUBLANES, SUBLANES), :] = jnp.zeros((SUBLANES, LANES), F32)
        acc_s[...] = jnp.zeros_like(acc_s)

    active = q < nt_ref[b] + 2
    args = (src_ref, dst_ref, gate_ref, xg_s, acc_s, wg_ref, wu_ref, wd_ref)

    @pl.when(jnp.logical_and(active, q % 2 == 0))
    def _():
        _moe_step(*args, buf0, buf1, y1, y0, R)

    @pl.when(jnp.logical_and(active, q % 2 == 1))
    def _():
        _moe_step(*args, buf1, buf0, y0, y1, R)

    @pl.when(q == pl.num_programs(1) - 1)
    def _():
        pltpu.sync_copy(acc_s.at[pl.ds(0, NB * SUBLANES)], out_hbm.at[b])


def _moe_routed(xg, eidx, egate, wg, wu, wd):
    N = eidx.shape[1]
    E, D, DE = wg.shape
    NB = min(MOE_BLOCK, N)
    nb = N // NB
    S = SUBLANES
    R = MOE_ROWS if NB >= MOE_BLOCK else MOE_ROWS_SMALL
    te, ntiles, rows, gate, SL = _tile_schedule(eidx, egate, NB, E, R)
    smem = lambda shift: pl.BlockSpec((1, 1, R), lambda b, q, *_: (b * SL + q + shift, 0, 0),
                                      memory_space=pltpu.SMEM)
    w_map = lambda b, q, te_ref, nt_ref: (te_ref[b * SL + q + 1], 0, 0)
    tile_rows = pltpu.VMEM((R * S, LANES), F32)
    grid_spec = pltpu.PrefetchScalarGridSpec(
        num_scalar_prefetch=2,
        grid=(nb, SL - 2),
        in_specs=[smem(2), smem(0), smem(0),
                  pl.BlockSpec(memory_space=pl.ANY),
                  pl.BlockSpec((1, D, DE), w_map), pl.BlockSpec((1, D, DE), w_map), pl.BlockSpec((1, DE, D), w_map)],
        out_specs=pl.BlockSpec(memory_space=pl.ANY),
        scratch_shapes=[pltpu.VMEM(((NB + 1) * S, LANES), F32), pltpu.VMEM(((NB + 1) * S, LANES), F32),
                        tile_rows, tile_rows, tile_rows, tile_rows],
    )
    return pl.pallas_call(
        functools.partial(_moe_kernel, NB=NB, R=R),
        grid_spec=grid_spec,
        out_shape=jax.ShapeDtypeStruct((nb, NB * S, LANES), F32),
        compiler_params=_cparams(("arbitrary", "arbitrary")),
        name="moe_routed",
    )(te, ntiles, rows, rows, gate, xg.reshape(nb, NB * S, LANES), wg, wu, wd)


def _combine_kernel(x_ref, sh_ref, sc_ref, g2_ref, r_ref, sg_ref, su_ref, sd_ref, lg_ref, lb_ref, o_ref, *, alpha):
    x = x_ref[0]
    T = x.shape[0]
    h = (x * (1.0 + sc_ref[0]) + sh_ref[0]).astype(BF16)
    a = _silu(_dot(h, sg_ref[...])) * _dot(h, su_ref[...])
    shared = _dot(a.astype(BF16), sd_ref[...])
    routed = jnp.concatenate([r_ref[0, pl.ds(s, T, stride=SUBLANES), :] for s in range(SUBLANES)], axis=1)
    o_ref[0] = _layernorm(alpha * x + g2_ref[0] * (routed + shared), lg_ref[...], lb_ref[...])


def _combine(x1, mod3, routed, sg, su, sd, ln_g, ln_b, alpha):
    B, L, D = x1.shape
    T = min(TOKEN_TILE, L)
    nl = L // T
    NB = routed.shape[1] // SUBLANES
    per = NB // T
    tok = pl.BlockSpec((1, T, D), lambda b, l: (b, l, 0))
    r_spec = pl.BlockSpec((1, T * SUBLANES, LANES), lambda b, l: ((b * nl + l) // per, (b * nl + l) % per, 0))
    return pl.pallas_call(
        functools.partial(_combine_kernel, alpha=alpha),
        grid=(B, nl),
        in_specs=[tok] + _mod_specs(mod3, (3, 4, 5), T) + [r_spec]
        + [_const_spec(sg.shape), _const_spec(su.shape), _const_spec(sd.shape), _const_spec((1, D)),
           _const_spec((1, D))],
        out_specs=tok,
        out_shape=jax.ShapeDtypeStruct((B, L, D), F32),
        compiler_params=_cparams(("arbitrary", "arbitrary")),
        name="combine",
    )(x1, mod3, mod3, mod3, routed, sg, su, sd, ln_g.reshape(1, D), ln_b.reshape(1, D))


def _split_w_in(w_in_l, gla_shape, hgrn_shape, D):
    Hg, Kg, Vg = gla_shape
    Hh, Kh, Vh = hgrn_shape
    rank = w_in_l.shape[1] - (2 * Hg * Kg + 2 * Hg * Vg + 2 * Hh * Kh + 2 * Hh * Vh + 2 * D)
    widths = (Hg * Kg, Hg * Kg, Hg * Vg, Hg * Vg, rank, Hh * Kh, Hh * Kh, Hh * Vh, Hh * Vh, 2 * D)
    out, start = [], 0
    for w in widths:
        out.append(w_in_l[:, start:start + w].astype(BF16))
        start += w
    return out


def kernel(x_prompt, x_sample, state_gla, state_hgrn, c_prompt, c_sample, w_ada, b_ada, w_in, w_gk2, b_gk,
           hgrn_lb, gla_norm_w, hgrn_norm_w, w_proj_a, w_proj_b, w_out, ln1_g, ln1_b, w_router, router_bias,
           w_exp_gate, w_exp_up, w_exp_down, w_sh_gate, w_sh_up, w_sh_down, ln2_g, ln2_b):
    depth = w_in.shape[0]
    BP, L, D = x_prompt.shape
    NS = x_sample.shape[0]
    assert x_sample.shape[1] == 1
    gla_shape = state_gla.shape[2:]
    hgrn_shape = state_hgrn.shape[2:]
    alpha = (2.0 * depth) ** 0.25

    xp = x_prompt
    xs = x_sample.reshape(NS, D)
    c_all = jnp.concatenate([c_prompt, c_sample], axis=0)
    new_gla_p, new_hgrn_p, new_gla_s, new_hgrn_s = [], [], [], []
    for l in range(depth):
        mod = _ada_mod(c_all, w_ada[l], b_ada[l])
        mod_p = mod[:BP].reshape(BP, 1, 6 * D)
        mod_s = mod[BP:]
        (wqa, wka, wva, wga, wgk1, wqb, wfb, wib, wgb, wuab) = _split_w_in(w_in[l], gla_shape, hgrn_shape, D)
        gla_w = [wqa, wka, wva, wga, wgk1, w_gk2[l].astype(BF16), b_gk[l].reshape(1, -1)]
        hgrn_w = [wqb, wfb, wib, wgb, hgrn_lb]
        wpa = w_proj_a[l].astype(BF16)
        wpb = w_proj_b[l].astype(BF16)
        wo = w_out[l].astype(BF16)
        wrT = w_router[l].T.astype(BF16)
        eg, eu, ed = w_exp_gate[l].astype(BF16), w_exp_up[l].astype(BF16), w_exp_down[l].astype(BF16)
        sg, su, sd = w_sh_gate[l].astype(BF16), w_sh_up[l].astype(BF16), w_sh_down[l].astype(BF16)

        def tail(x3, ya, yb, mod3):
            x1 = _merge(x3, ya, yb, mod3, wuab, wo, ln1_g[l], ln1_b[l], alpha)
            eidx, egate, xg = _router(x1, mod3, wrT, router_bias[l])
            routed = _moe_routed(xg, eidx, egate, eg, eu, ed)
            return _combine(x1, mod3, routed, sg, su, sd, ln2_g[l], ln2_b[l], alpha)

        ya, sg_p = _branch_prompt("gla", xp, mod_p, gla_w, gla_norm_w[l], wpa, *gla_shape, layer=l)
        yb, sh_p = _branch_prompt("hgrn", xp, mod_p, hgrn_w, hgrn_norm_w[l], wpb, *hgrn_shape, layer=l)
        xp = tail(xp, ya, yb, mod_p)
        new_gla_p.append(sg_p)
        new_hgrn_p.append(sh_p)

        ya, sg_s = _branch_sample("gla", xs, mod_s, gla_w, gla_norm_w[l], wpa, state_gla[l], layer=l)
        yb, sh_s = _branch_sample("hgrn", xs, mod_s, hgrn_w, hgrn_norm_w[l], wpb, state_hgrn[l], layer=l)
        xs = tail(xs[None], ya[None], yb[None], mod_s[None])[0]
        new_gla_s.append(sg_s)
        new_hgrn_s.append(sh_s)

    return (xp, xs.reshape(NS, 1, D), jnp.stack(new_gla_p), jnp.stack(new_hgrn_p),
            jnp.stack(new_gla_s), jnp.stack(new_hgrn_s))
```

```python
import functools

import jax
import jax.numpy as jnp
from jax import lax
from jax.experimental import pallas as pl
from jax.experimental.pallas import tpu as pltpu

F32 = jnp.float32
BF16 = jnp.bfloat16

GLA_GATE_NORMALIZER = 16.0
N_GROUPS = 8
TOPK_GROUPS = 4
TOP_K = 8
ROUTED_SCALE = 2.5
EPS = 1e-5

SUBLANES = 8
LANES = 128
VMEM_LIMIT_BYTES = 56 * 1024 * 1024

TOKEN_TILE = 512
MOE_BLOCK = 4096
MOE_ROWS = 256
MOE_ROWS_SMALL = 32
RMW_BATCH = 8
CHUNK = 128
SUB = SUBLANES
NEG_BIG = -1e30


def _cparams(sem):
    return pltpu.CompilerParams(dimension_semantics=sem, vmem_limit_bytes=VMEM_LIMIT_BYTES)


def _dot(a, b):
    return jnp.dot(a, b, preferred_element_type=F32)


def _dot_nt(a, b):
    return lax.dot_general(a, b, (((1,), (1,)), ((), ())), preferred_element_type=F32)


def _silu(x):
    return x * jax.nn.sigmoid(x)


def _log_sigmoid(x):
    return jnp.minimum(x, 0.0) - jnp.log1p(jnp.exp(-jnp.abs(x)))


def _layernorm(r, g, b):
    mu = jnp.mean(r, axis=-1, keepdims=True)
    d = r - mu
    var = jnp.mean(d * d, axis=-1, keepdims=True)
    return d * lax.rsqrt(var + EPS) * g + b


def _ada_kernel(c_ref, w_ref, b_ref, o_ref):
    c = c_ref[...]
    o_ref[...] = _dot(_silu(c).astype(BF16), w_ref[...].astype(BF16)) + b_ref[...]


def _ada_mod(c, w_ada, b_ada):
    R, D = c.shape
    N = w_ada.shape[1]
    tn = D
    return pl.pallas_call(
        _ada_kernel,
        grid=(N // tn,),
        in_specs=[pl.BlockSpec((R, D), lambda j: (0, 0)),
                  pl.BlockSpec((D, tn), lambda j: (0, j)),
                  pl.BlockSpec((1, tn), lambda j: (0, j))],
        out_specs=pl.BlockSpec((R, tn), lambda j: (0, j)),
        out_shape=jax.ShapeDtypeStruct((R, N), F32),
        compiler_params=_cparams(("arbitrary",)),
        name="ada_mod",
    )(c, w_ada, b_ada.reshape(1, N))


def _chunk_masks(C):
    row = lax.broadcasted_iota(jnp.int32, (C, 1), 0)
    ri = lax.broadcasted_iota(jnp.int32, (C, C), 0)
    ci = lax.broadcasted_iota(jnp.int32, (C, C), 1)
    levels = []
    s = SUB
    while s < C:
        same_group = (ri // (2 * s)) == (ci // (2 * s))
        levels.append((s, same_group))
        s *= 2
    diag = (ri // SUB) == (ci // SUB)
    return row, levels, diag


def _bcast_rows(x, group, idx):
    C, K = x.shape
    G = C // group
    x3 = x.reshape(G, group, K)
    return jnp.broadcast_to(x3[:, idx:idx + 1, :], (G, group, K)).reshape(C, K)


def _chunk_head(q, k, la, v, st, sel, masks):
    C, K = q.shape
    row, levels, diag = masks
    rmod = row % SUB

    x3 = la.reshape(C // SUB, SUB, K)
    sub3 = lax.broadcasted_iota(jnp.int32, (1, SUB, 1), 1)
    sh = 1
    while sh < SUB:
        x3 = x3 + jnp.where(sub3 >= sh, pltpu.roll(x3, sh, 1), 0.0)
        sh *= 2
    x = x3.reshape(C, K)
    x_sub = x

    sc = jnp.zeros((C, C), F32)
    for s, same_group in levels:
        G = C // (2 * s)
        x4 = x.reshape(G, 2, s, K)
        xl, xr = x4[:, 0], x4[:, 1]
        yl = jnp.broadcast_to(xl[:, s - 1:s, :], (G, s, K))
        qr = q.reshape(G, 2, s, K)[:, 1] * jnp.exp(xr)
        kl = k.reshape(G, 2, s, K)[:, 0] * jnp.exp(yl - xl)
        zero = jnp.zeros((G, s, K), F32)
        qf = jnp.stack([zero, qr], axis=1).reshape(C, K).astype(BF16)
        kf = jnp.stack([kl, zero], axis=1).reshape(C, K).astype(BF16)
        sc = sc + jnp.where(same_group, _dot_nt(qf, kf), 0.0)
        x = jnp.stack([xl, xr + yl], axis=1).reshape(C, K)
    b = x

    terms = []
    for jj in range(SUB):
        kb = _bcast_rows(k, SUB, jj)
        xb = _bcast_rows(x_sub, SUB, jj)
        e = jnp.where(rmod >= jj, x_sub - xb, NEG_BIG)
        terms.append((q * kb * jnp.exp(e)).astype(BF16))
    d = _dot(jnp.concatenate(terms, axis=1), sel)
    sc = sc + jnp.where(diag, d, 0.0)

    vb = v.astype(BF16)
    o = _dot(sc.astype(BF16), vb) + _dot_nt((q * jnp.exp(b)).astype(BF16), st.astype(BF16))
    b_last = b[C - 1:C, :]
    kd = (k * jnp.exp(b_last - b)).astype(BF16)
    st_new = st * jnp.exp(b_last) + _dot(v.T.astype(BF16), kd)
    return o, st_new


def _recurrence_tile(q_ref, k_ref, la_ref, v_ref, o_ref, st_ref, sel_ref, n_heads, K, V, T):
    C = CHUNK
    masks = _chunk_masks(C)
    sel = sel_ref[...]

    def body(c, carry):
        r0 = pl.multiple_of(c * C, C)
        for h in range(n_heads):
            ks = slice(h * K, (h + 1) * K)
            vs = slice(h * V, (h + 1) * V)
            o, st_new = _chunk_head(q_ref[pl.ds(r0, C), ks], k_ref[pl.ds(r0, C), ks],
                                    la_ref[pl.ds(r0, C), ks], v_ref[pl.ds(r0, C), vs],
                                    st_ref[h], sel, masks)
            o_ref[pl.ds(r0, C), vs] = o
            st_ref[h] = st_new
        return carry

    lax.fori_loop(0, T // C, body, 0)


def _branch_kernel(*refs, kind, n_heads, K, V, T, layer):
    if kind == "gla":
        (x_ref, sh_ref, sc_ref, wq_ref, wk_ref, wv_ref, wg_ref, wgk1_ref, wgk2_ref, bgk_ref,
         nw_ref, wp_ref, sel_ref, y_ref, sout_ref,
         q_s, k_s, la_s, v_s, g_s, o_s, st_s) = refs
    else:
        (x_ref, sh_ref, sc_ref, wq_ref, wk_ref, wv_ref, wg_ref, lb_ref,
         nw_ref, wp_ref, sel_ref, y_ref, sout_ref,
         q_s, k_s, la_s, v_s, g_s, o_s, st_s) = refs
    lt = pl.program_id(1)

    @pl.when(lt == 0)
    def _():
        st_s[...] = jnp.zeros_like(st_s)

    h = (x_ref[0] * (1.0 + sc_ref[0]) + sh_ref[0]).astype(BF16)
    scale = K ** -0.5
    if kind == "gla":
        q_s[...] = _dot(h, wq_ref[...]) * scale
        k_s[...] = _dot(h, wk_ref[...])
        lr = _dot(h, wgk1_ref[...]).astype(BF16)
        la_s[...] = _log_sigmoid(_dot(lr, wgk2_ref[...]) + bgk_ref[...]) * (1.0 / GLA_GATE_NORMALIZER)
    else:
        q_s[...] = _silu(_dot(h, wq_ref[...])) * scale
        lbp = lb_ref[...]
        e = jnp.exp(lbp - jnp.max(lbp, axis=0, keepdims=True))
        lb = jnp.sum(e[:layer + 1], axis=0, keepdims=True) / jnp.sum(e, axis=0, keepdims=True)
        forget = lb + (1.0 - lb) * jax.nn.sigmoid(_dot(h, wk_ref[...]))
        k_s[...] = 1.0 - forget
        la_s[...] = jnp.log(forget)
    v_s[...] = _dot(h, wv_ref[...])
    g_s[...] = _dot(h, wg_ref[...])

    _recurrence_tile(q_s, k_s, la_s, v_s, o_s, st_s, sel_ref, n_heads, K, V, T)

    nw = nw_ref[...]
    outs = []
    for hd in range(n_heads):
        vs = slice(hd * V, (hd + 1) * V)
        o = o_s[:, vs]
        g = g_s[:, vs]
        gate = _silu(g) if kind == "gla" else jax.nn.sigmoid(g)
        o = o * lax.rsqrt(jnp.mean(o * o, axis=-1, keepdims=True) + EPS) * nw * gate
        outs.append(o.astype(BF16))
    y_ref[0] = _dot(jnp.concatenate(outs, axis=1), wp_ref[...])

    @pl.when(lt == pl.num_programs(1) - 1)
    def _():
        for hd in range(n_heads):
            sout_ref[0, hd] = st_s[hd].T


def _sel_matrix(K, C):
    r = jnp.arange(SUB * K, dtype=jnp.int32)[:, None] // K
    c = jnp.arange(C, dtype=jnp.int32)[None, :] % SUB
    return (r == c).astype(BF16)


def _const_spec(shape):
    nd = len(shape)
    return pl.BlockSpec(shape, lambda b, l: (0,) * nd)


def _branch_prompt(kind, x, mod3, weights, norm_w, w_proj, n_heads, K, V, layer):
    B, L, D = x.shape
    T = min(TOKEN_TILE, L)
    HK, HV = n_heads * K, n_heads * V
    sel = _sel_matrix(K, CHUNK)
    x_spec = pl.BlockSpec((1, T, D), lambda b, l: (b, l, 0))
    sh_spec = pl.BlockSpec((1, 1, D), lambda b, l: (b, 0, 0))
    sc_spec = pl.BlockSpec((1, 1, D), lambda b, l: (b, 0, 1))
    w_specs = [_const_spec(w.shape) for w in weights]
    nw2 = norm_w.reshape(1, V)
    in_specs = [x_spec, sh_spec, sc_spec] + w_specs + [_const_spec(nw2.shape), _const_spec(w_proj.shape),
                                                       _const_spec(sel.shape)]
    kern = functools.partial(_branch_kernel, kind=kind, n_heads=n_heads, K=K, V=V, T=T, layer=layer)
    return pl.pallas_call(
        kern,
        grid=(B, L // T),
        in_specs=in_specs,
        out_specs=[pl.BlockSpec((1, T, D), lambda b, l: (b, l, 0)),
                   pl.BlockSpec((1, n_heads, K, V), lambda b, l: (b, 0, 0, 0))],
        out_shape=[jax.ShapeDtypeStruct((B, L, D), F32),
                   jax.ShapeDtypeStruct((B, n_heads, K, V), F32)],
        scratch_shapes=[pltpu.VMEM((T, HK), F32), pltpu.VMEM((T, HK), F32), pltpu.VMEM((T, HK), F32),
                        pltpu.VMEM((T, HV), F32), pltpu.VMEM((T, HV), F32), pltpu.VMEM((T, HV), F32),
                        pltpu.VMEM((n_heads, V, K), F32)],
        compiler_params=_cparams(("arbitrary", "arbitrary")),
        name=f"{kind}_prompt",
    )(x, mod3, mod3, *weights, nw2, w_proj, sel)


def _sample_kernel(*refs, kind, n_heads, K, V, TB, layer):
    if kind == "gla":
        (x_ref, sh_ref, sc_ref, wq_ref, wk_ref, wv_ref, wg_ref, wgk1_ref, wgk2_ref, bgk_ref,
         nw_ref, wp_ref, s_ref, y_ref, sout_ref, qT_s, kT_s, aT_s, v_s, g_s, o_s) = refs
    else:
        (x_ref, sh_ref, sc_ref, wq_ref, wk_ref, wv_ref, wg_ref, lb_ref,
         nw_ref, wp_ref, s_ref, y_ref, sout_ref, qT_s, kT_s, aT_s, v_s, g_s, o_s) = refs
    step = pl.program_id(0)
    NT = x_ref.shape[0]

    @pl.when(step == 0)
    def _():
        h = (x_ref[...] * (1.0 + sc_ref[...]) + sh_ref[...]).astype(BF16)
        scale = K ** -0.5
        if kind == "gla":
            q = _dot(h, wq_ref[...]) * scale
            k = _dot(h, wk_ref[...])
            lr = _dot(h, wgk1_ref[...]).astype(BF16)
            a = jnp.exp(_log_sigmoid(_dot(lr, wgk2_ref[...]) + bgk_ref[...]) * (1.0 / GLA_GATE_NORMALIZER))
        else:
            q = _silu(_dot(h, wq_ref[...])) * scale
            lbp = lb_ref[...]
            e = jnp.exp(lbp - jnp.max(lbp, axis=0, keepdims=True))
            lb = jnp.sum(e[:layer + 1], axis=0, keepdims=True) / jnp.sum(e, axis=0, keepdims=True)
            a = lb + (1.0 - lb) * jax.nn.sigmoid(_dot(h, wk_ref[...]))
            k = 1.0 - a
        for hd in range(n_heads):
            ks = slice(hd * K, (hd + 1) * K)
            qT_s[ks, :] = q[:, ks].T
            kT_s[ks, :] = k[:, ks].T
            aT_s[ks, :] = a[:, ks].T
        v_s[...] = _dot(h, wv_ref[...])
        g_s[...] = _dot(h, wg_ref[...])

    lane = lax.broadcasted_iota(jnp.int32, (1, NT), 1)
    sub = lax.broadcasted_iota(jnp.int32, (TB, 1), 0)
    t0 = pl.multiple_of(step * TB, TB)
    for hd in range(n_heads):
        ks = slice(hd * K, (hd + 1) * K)
        vs = slice(hd * V, (hd + 1) * V)
        v_rows = v_s[pl.ds(t0, TB), vs]
        o_rows = jnp.zeros((TB, V), F32)
        for j in range(TB):
            pick = lane == t0 + j
            acol = jnp.sum(jnp.where(pick, aT_s[ks, :], 0.0), axis=1, keepdims=True)
            kcol = jnp.sum(jnp.where(pick, kT_s[ks, :], 0.0), axis=1, keepdims=True)
            qcol = jnp.sum(jnp.where(pick, qT_s[ks, :], 0.0), axis=1, keepdims=True)
            s1 = acol * s_ref[j, hd] + kcol * v_rows[j:j + 1, :]
            sout_ref[j, hd] = s1
            o_rows = jnp.where(sub == j, jnp.sum(qcol * s1, axis=0, keepdims=True), o_rows)
        o_s[pl.ds(t0, TB), vs] = o_rows

    @pl.when(step == pl.num_programs(0) - 1)
    def _():
        nw = nw_ref[...]
        outs = []
        for hd in range(n_heads):
            vs = slice(hd * V, (hd + 1) * V)
            o = o_s[:, vs]
            g = g_s[:, vs]
            gate = _silu(g) if kind == "gla" else jax.nn.sigmoid(g)
            o = o * lax.rsqrt(jnp.mean(o * o, axis=-1, keepdims=True) + EPS) * nw * gate
            outs.append(o.astype(BF16))
        y_ref[...] = _dot(jnp.concatenate(outs, axis=1), wp_ref[...])


def _branch_sample(kind, x, mod, weights, norm_w, w_proj, state, layer):
    NT, D = x.shape
    _, n_heads, K, V = state.shape
    HK, HV = n_heads * K, n_heads * V
    TB = SUBLANES
    c1 = lambda s: pl.BlockSpec(s, lambda i: (0,) * len(s))
    nw2 = norm_w.reshape(1, V)
    in_specs = ([c1((NT, D)), pl.BlockSpec((NT, D), lambda i: (0, 0)), pl.BlockSpec((NT, D), lambda i: (0, 1))]
                + [c1(w.shape) for w in weights] + [c1(nw2.shape), c1(w_proj.shape),
                                                    pl.BlockSpec((TB, n_heads, K, V), lambda i: (i, 0, 0, 0))])
    kern = functools.partial(_sample_kernel, kind=kind, n_heads=n_heads, K=K, V=V, TB=TB, layer=layer)
    return pl.pallas_call(
        kern,
        grid=(NT // TB,),
        in_specs=in_specs,
        out_specs=[c1((NT, D)), pl.BlockSpec((TB, n_heads, K, V), lambda i: (i, 0, 0, 0))],
        out_shape=[jax.ShapeDtypeStruct((NT, D), F32), jax.ShapeDtypeStruct(state.shape, F32)],
        scratch_shapes=[pltpu.VMEM((HK, NT), F32), pltpu.VMEM((HK, NT), F32), pltpu.VMEM((HK, NT), F32),
                        pltpu.VMEM((NT, HV), F32), pltpu.VMEM((NT, HV), F32), pltpu.VMEM((NT, HV), F32)],
        compiler_params=_cparams(("arbitrary",)),
        name=f"{kind}_sample",
    )(x, mod, mod, *weights, nw2, w_proj, state)


def _merge_kernel(x_ref, ya_ref, yb_ref, sh_ref, sc_ref, g_ref, wu_ref, wo_ref, lg_ref, lb_ref, o_ref, *, alpha):
    x = x_ref[0]
    D = x.shape[-1]
    h = (x * (1.0 + sc_ref[0]) + sh_ref[0]).astype(BF16)
    u = _dot(h, wu_ref[...])
    merged = jax.nn.sigmoid(u[:, :D]) * ya_ref[0] + jax.nn.sigmoid(u[:, D:]) * yb_ref[0]
    mix = _dot(merged.astype(BF16), wo_ref[...])
    o_ref[0] = _layernorm(alpha * x + g_ref[0] * mix, lg_ref[...], lb_ref[...])


def _mod_specs(mod3, cols, T):
    D = mod3.shape[-1] // 6
    if mod3.shape[1] == 1:
        return [pl.BlockSpec((1, 1, D), functools.partial(lambda b, l, *_, c: (b, 0, c), c=c)) for c in cols]
    return [pl.BlockSpec((1, T, D), functools.partial(lambda b, l, *_, c: (b, l, c), c=c)) for c in cols]


def _merge(x, ya, yb, mod3, wu, w_out, ln_g, ln_b, alpha):
    B, L, D = x.shape
    T = min(TOKEN_TILE, L)
    tok = pl.BlockSpec((1, T, D), lambda b, l: (b, l, 0))
    return pl.pallas_call(
        functools.partial(_merge_kernel, alpha=alpha),
        grid=(B, L // T),
        in_specs=[tok, tok, tok] + _mod_specs(mod3, (0, 1, 2), T)
        + [_const_spec(wu.shape), _const_spec(w_out.shape), _const_spec((1, D)), _const_spec((1, D))],
        out_specs=tok,
        out_shape=jax.ShapeDtypeStruct((B, L, D), F32),
        compiler_params=_cparams(("arbitrary", "arbitrary")),
        name="merge",
    )(x, ya, yb, mod3, mod3, mod3, wu, w_out, ln_g.reshape(1, D), ln_b.reshape(1, D))


def _first_argmax(vals, iota, n, axis):
    m = jnp.max(vals, axis=axis, keepdims=True)
    idx = jnp.min(jnp.where(vals == m, iota, n), axis=axis, keepdims=True)
    return m, idx


def _router_kernel(x_ref, sh_ref, sc_ref, wrT_ref, bias_ref, eidx_ref, egate_ref, xg_ref, *, n_experts):
    E = n_experts
    per = E // N_GROUPS
    hf = x_ref[0] * (1.0 + sc_ref[0]) + sh_ref[0]
    h = hf.astype(BF16)
    T = h.shape[0]
    for s in range(hf.shape[1] // LANES):
        xg_ref[pl.ds(s, T, stride=SUBLANES), :] = hf[:, s * LANES:(s + 1) * LANES]
    scores = jax.nn.sigmoid(_dot_nt(wrT_ref[...], h))
    biased = scores + bias_ref[...]
    b3 = biased.reshape(N_GROUPS, per, T)
    i3 = lax.broadcasted_iota(jnp.int32, (N_GROUPS, per, T), 1)
    m1, a1 = _first_argmax(b3, i3, per, 1)
    m2 = jnp.max(jnp.where(i3 == a1, -jnp.inf, b3), axis=1, keepdims=True)
    gscore = (m1 + m2).reshape(N_GROUPS, T)
    gi = lax.broadcasted_iota(jnp.int32, (N_GROUPS, T), 0)
    gsel = jnp.zeros((N_GROUPS, T), jnp.bool_)
    for _ in range(TOPK_GROUPS):
        _, a = _first_argmax(gscore, gi, N_GROUPS, 0)
        hit = gi == a
        gsel = jnp.logical_or(gsel, hit)
        gscore = jnp.where(hit, -jnp.inf, gscore)
    emask = jnp.broadcast_to(gsel.reshape(N_GROUPS, 1, T), (N_GROUPS, per, T)).reshape(E, T)
    cand = jnp.where(emask, biased, -jnp.inf)
    ei = lax.broadcasted_iota(jnp.int32, (E, T), 0)
    picks, weights = [], []
    for _ in range(TOP_K):
        _, a = _first_argmax(cand, ei, E, 0)
        hit = ei == a
        picks.append(a)
        weights.append(jnp.sum(jnp.where(hit, scores, 0.0), axis=0, keepdims=True))
        cand = jnp.where(hit, -jnp.inf, cand)
    w = jnp.concatenate(weights, axis=0)
    egate_ref[...] = w / jnp.sum(w, axis=0, keepdims=True) * ROUTED_SCALE
    eidx_ref[...] = jnp.concatenate(picks, axis=0)


def _router(x1, mod3, wrT, bias):
    B, L, D = x1.shape
    assert D == SUBLANES * LANES
    E = wrT.shape[0]
    T = min(TOKEN_TILE, L)
    nl = L // T
    N = B * L
    tok = pl.BlockSpec((1, T, D), lambda b, l: (b, l, 0))
    pick_spec = pl.BlockSpec((TOP_K, T), lambda b, l: (0, b * nl + l))
    return pl.pallas_call(
        functools.partial(_router_kernel, n_experts=E),
        grid=(B, nl),
        in_specs=[tok] + _mod_specs(mod3, (3, 4), T) + [_const_spec(wrT.shape), _const_spec((E, 1))],
        out_specs=[pick_spec, pick_spec, pl.BlockSpec((T * SUBLANES, LANES), lambda b, l: (b * nl + l, 0))],
        out_shape=[jax.ShapeDtypeStruct((TOP_K, N), jnp.int32), jax.ShapeDtypeStruct((TOP_K, N), F32),
                   jax.ShapeDtypeStruct((N * SUBLANES, LANES), F32)],
        compiler_params=_cparams(("arbitrary", "arbitrary")),
        name="router",
    )(x1, mod3, mod3, wrT, bias.reshape(E, 1))


def _tile_schedule(eidx, egate, NB, E, R):
    Kp, N = eidx.shape
    nb = N // NB
    A = Kp * NB
    S = SUBLANES
    assert A % R == 0
    NW = A // R
    tok = jnp.broadcast_to(jnp.arange(N, dtype=jnp.int32)[None, :], (Kp, N))
    key = (tok // NB) * E + eidx
    skey, stok, sgate = lax.sort((key.reshape(-1), tok.reshape(-1), egate.reshape(-1)), num_keys=1)
    w_e = (skey % E).reshape(nb, NW, R)
    w_t = ((stok % NB) * S).reshape(nb, NW, R)
    w_g = sgate.reshape(nb, NW, R)
    first = w_e[:, :, 0]
    npair = w_e[:, :, R - 1] - first + 1
    cum = jnp.cumsum(npair, axis=1)
    ntiles = cum[:, -1]
    SL = NW + E + 3
    q = jnp.arange(SL, dtype=jnp.int32)[None, :] - 2
    qc = jnp.clip(q, 0, ntiles[:, None] - 1)
    k_q = jnp.minimum(jnp.sum((cum[:, None, :] <= qc[:, :, None]).astype(jnp.int32), axis=2), NW - 1)
    onehot = (k_q[:, :, None] == jnp.arange(NW, dtype=jnp.int32)[None, None, :]).astype(F32)
    sel = lambda a: jnp.einsum('bsk,bkr->bsr', onehot, a.astype(F32), precision=lax.Precision.HIGHEST)
    selk = lambda a: jnp.sum(onehot * a.astype(F32)[:, None, :], axis=2).astype(jnp.int32)
    e_q = jnp.clip(selk(first) + qc - selk(cum - npair), 0, E - 1)
    real = jnp.logical_and(q >= 0, q < ntiles[:, None])
    match = jnp.logical_and(real[:, :, None], sel(w_e).astype(jnp.int32) == e_q[:, :, None])
    rows = jnp.where(match, sel(w_t).astype(jnp.int32), NB * S).reshape(nb * SL, 1, R)
    gate = jnp.where(match, sel(w_g), 0.0).reshape(nb * SL, 1, R)
    return e_q.reshape(-1), ntiles, rows, gate, SL


def _moe_step(src_ref, dst_ref, gate_ref, xg_s, acc_s, wg_ref, wu_ref, wd_ref, gbuf, cbuf, cy, sy, R):
    S = SUBLANES
    for r in range(R):
        t0 = pl.multiple_of(src_ref[0, 0, r], S)
        gbuf[r * S:(r + 1) * S, :] = xg_s[pl.ds(t0, S), :]

    x = jnp.concatenate([cbuf[pl.ds(s, R, stride=S), :] for s in range(S)], axis=1).astype(BF16)
    a = _silu(_dot(x, wg_ref[0])) * _dot(x, wu_ref[0])
    y = _dot(a.astype(BF16), wd_ref[0])
    for s in range(S):
        cy[pl.ds(s, R, stride=S), :] = y[:, s * LANES:(s + 1) * LANES]

    for r0 in range(0, R, RMW_BATCH):
        upd = []
        for r in range(r0, r0 + RMW_BATCH):
            a0 = pl.multiple_of(dst_ref[0, 0, r], S)
            upd.append((a0, acc_s[pl.ds(a0, S), :] + gate_ref[0, 0, r] * sy[r * S:(r + 1) * S, :]))
        for a0, val in upd:
            acc_s[pl.ds(a0, S), :] = val


def _moe_kernel(te_ref, nt_ref, src_ref, dst_ref, gate_ref, xg_hbm, wg_ref, wu_ref, wd_ref, out_hbm,
                xg_s, acc_s, buf0, buf1, y0, y1, *, NB, R):
    del te_ref
    b = pl.program_id(0)
    q = pl.program_id(1)

    @pl.when(jnp.logical_and(b == 0, q == 0))
    def _():
        for ref in (buf0, buf1, y0, y1):
            ref[...] = jnp.zeros_like(ref)

    @pl.when(q == 0)
    def _():
        pltpu.sync_copy(xg_hbm.at[b], xg_s.at[pl.ds(0, NB * SUBLANES)])
        xg_s[pl.ds(NB * SUBLANES, SUBLANES), :] = jnp.zeros((SUBLANES, LANES), F32)
        acc_s[...] = jnp.zeros_like(acc_s)

    active = q < nt_ref[b] + 2
    args = (src_ref, dst_ref, gate_ref, xg_s, acc_s, wg_ref, wu_ref, wd_ref)

    @pl.when(jnp.logical_and(active, q % 2 == 0))
    def _():
        _moe_step(*args, buf0, buf1, y1, y0, R)

    @pl.when(jnp.logical_and(active, q % 2 == 1))
    def _():
        _moe_step(*args, buf1, buf0, y0, y1, R)

    @pl.when(q == pl.num_programs(1) - 1)
    def _():
        pltpu.sync_copy(acc_s.at[pl.ds(0, NB * SUBLANES)], out_hbm.at[b])


def _moe_routed(xg, eidx, egate, wg, wu, wd):
    N = eidx.shape[1]
    E, D, DE = wg.shape
    NB = min(MOE_BLOCK, N)
    nb = N // NB
    S = SUBLANES
    R = MOE_ROWS if NB >= MOE_BLOCK else MOE_ROWS_SMALL
    te, ntiles, rows, gate, SL = _tile_schedule(eidx, egate, NB, E, R)
    smem = lambda shift: pl.BlockSpec((1, 1, R), lambda b, q, *_: (b * SL + q + shift, 0, 0),
                                      memory_space=pltpu.SMEM)
    w_map = lambda b, q, te_ref, nt_ref: (te_ref[b * SL + q + 1], 0, 0)
    tile_rows = pltpu.VMEM((R * S, LANES), F32)
    grid_spec = pltpu.PrefetchScalarGridSpec(
        num_scalar_prefetch=2,
        grid=(nb, SL - 2),
        in_specs=[smem(2), smem(0), smem(0),
                  pl.BlockSpec(memory_space=pl.ANY),
                  pl.BlockSpec((1, D, DE), w_map), pl.BlockSpec((1, D, DE), w_map), pl.BlockSpec((1, DE, D), w_map)],
        out_specs=pl.BlockSpec(memory_space=pl.ANY),
        scratch_shapes=[pltpu.VMEM(((NB + 1) * S, LANES), F32), pltpu.VMEM(((NB + 1) * S, LANES), F32),
                        tile_rows, tile_rows, tile_rows, tile_rows],
    )
    return pl.pallas_call(
        functools.partial(_moe_kernel, NB=NB, R=R),
        grid_spec=grid_spec,
        out_shape=jax.ShapeDtypeStruct((nb, NB * S, LANES), F32),
        compiler_params=_cparams(("arbitrary", "arbitrary")),
        name="moe_routed",
    )(te, ntiles, rows, rows, gate, xg.reshape(nb, NB * S, LANES), wg, wu, wd)


def _combine_kernel(x_ref, sh_ref, sc_ref, g2_ref, r_ref, sg_ref, su_ref, sd_ref, lg_ref, lb_ref, o_ref, *, alpha):
    x = x_ref[0]
    T = x.shape[0]
    h = (x * (1.0 + sc_ref[0]) + sh_ref[0]).astype(BF16)
    a = _silu(_dot(h, sg_ref[...])) * _dot(h, su_ref[...])
    shared = _dot(a.astype(BF16), sd_ref[...])
    routed = jnp.concatenate([r_ref[0, pl.ds(s, T, stride=SUBLANES), :] for s in range(SUBLANES)], axis=1)
    o_ref[0] = _layernorm(alpha * x + g2_ref[0] * (routed + shared), lg_ref[...], lb_ref[...])


def _combine(x1, mod3, routed, sg, su, sd, ln_g, ln_b, alpha):
    B, L, D = x1.shape
    T = min(TOKEN_TILE, L)
    nl = L // T
    NB = routed.shape[1] // SUBLANES
    per = NB // T
    tok = pl.BlockSpec((1, T, D), lambda b, l: (b, l, 0))
    r_spec = pl.BlockSpec((1, T * SUBLANES, LANES), lambda b, l: ((b * nl + l) // per, (b * nl + l) % per, 0))
    return pl.pallas_call(
        functools.partial(_combine_kernel, alpha=alpha),
        grid=(B, nl),
        in_specs=[tok] + _mod_specs(mod3, (3, 4, 5), T) + [r_spec]
        + [_const_spec(sg.shape), _const_spec(su.shape), _const_spec(sd.shape), _const_spec((1, D)),
           _const_spec((1, D))],
        out_specs=tok,
        out_shape=jax.ShapeDtypeStruct((B, L, D), F32),
        compiler_params=_cparams(("arbitrary", "arbitrary")),
        name="combine",
    )(x1, mod3, mod3, mod3, routed, sg, su, sd, ln_g.reshape(1, D), ln_b.reshape(1, D))


def _split_w_in(w_in_l, gla_shape, hgrn_shape, D):
    Hg, Kg, Vg = gla_shape
    Hh, Kh, Vh = hgrn_shape
    rank = w_in_l.shape[1] - (2 * Hg * Kg + 2 * Hg * Vg + 2 * Hh * Kh + 2 * Hh * Vh + 2 * D)
    widths = (Hg * Kg, Hg * Kg, Hg * Vg, Hg * Vg, rank, Hh * Kh, Hh * Kh, Hh * Vh, Hh * Vh, 2 * D)
    out, start = [], 0
    for w in widths:
        out.append(w_in_l[:, start:start + w].astype(BF16))
        start += w
    return out


def kernel(x_prompt, x_sample, state_gla, state_hgrn, c_prompt, c_sample, w_ada, b_ada, w_in, w_gk2, b_gk,
           hgrn_lb, gla_norm_w, hgrn_norm_w, w_proj_a, w_proj_b, w_out, ln1_g, ln1_b, w_router, router_bias,
           w_exp_gate, w_exp_up, w_exp_down, w_sh_gate, w_sh_up, w_sh_down, ln2_g, ln2_b):
    depth = w_in.shape[0]
    BP, L, D = x_prompt.shape
    NS = x_sample.shape[0]
    assert x_sample.shape[1] == 1
    gla_shape = state_gla.shape[2:]
    hgrn_shape = state_hgrn.shape[2:]
    alpha = (2.0 * depth) ** 0.25

    xp = x_prompt
    xs = x_sample.reshape(NS, D)
    c_all = jnp.concatenate([c_prompt, c_sample], axis=0)
    new_gla_p, new_hgrn_p, new_gla_s, new_hgrn_s = [], [], [], []
    for l in range(depth):
        mod = _ada_mod(c_all, w_ada[l], b_ada[l])
        mod_p = mod[:BP].reshape(BP, 1, 6 * D)
        mod_s = mod[BP:]
        (wqa, wka, wva, wga, wgk1, wqb, wfb, wib, wgb, wuab) = _split_w_in(w_in[l], gla_shape, hgrn_shape, D)
        gla_w = [wqa, wka, wva, wga, wgk1, w_gk2[l].astype(BF16), b_gk[l].reshape(1, -1)]
        hgrn_w = [wqb, wfb, wib, wgb, hgrn_lb]
        wpa = w_proj_a[l].astype(BF16)
        wpb = w_proj_b[l].astype(BF16)
        wo = w_out[l].astype(BF16)
        wrT = w_router[l].T.astype(BF16)
        eg, eu, ed = w_exp_gate[l].astype(BF16), w_exp_up[l].astype(BF16), w_exp_down[l].astype(BF16)
        sg, su, sd = w_sh_gate[l].astype(BF16), w_sh_up[l].astype(BF16), w_sh_down[l].astype(BF16)

        def tail(x3, ya, yb, mod3):
            x1 = _merge(x3, ya, yb, mod3, wuab, wo, ln1_g[l], ln1_b[l], alpha)
            eidx, egate, xg = _router(x1, mod3, wrT, router_bias[l])
            routed = _moe_routed(xg, eidx, egate, eg, eu, ed)
            return _combine(x1, mod3, routed, sg, su, sd, ln2_g[l], ln2_b[l], alpha)

        ya, sg_p = _branch_prompt("gla", xp, mod_p, gla_w, gla_norm_w[l], wpa, *gla_shape, layer=l)
        yb, sh_p = _branch_prompt("hgrn", xp, mod_p, hgrn_w, hgrn_norm_w[l], wpb, *hgrn_shape, layer=l)
        xp = tail(xp, ya, yb, mod_p)
        new_gla_p.append(sg_p)
        new_hgrn_p.append(sh_p)

        ya, sg_s = _branch_sample("gla", xs, mod_s, gla_w, gla_norm_w[l], wpa, state_gla[l], layer=l)
        yb, sh_s = _branch_sample("hgrn", xs, mod_s, hgrn_w, hgrn_norm_w[l], wpb, state_hgrn[l], layer=l)
        xs = tail(xs[None], ya[None], yb[None], mod_s[None])[0]
        new_gla_s.append(sg_s)
        new_hgrn_s.append(sh_s)

    return (xp, xs.reshape(NS, 1, D), jnp.stack(new_gla_p), jnp.stack(new_hgrn_p),
            jnp.stack(new_gla_s), jnp.stack(new_hgrn_s))
```

```python
import functools

import jax
import jax.numpy as jnp
from jax import lax
from jax.experimental import pallas as pl
from jax.experimental.pallas import tpu as pltpu

F32 = jnp.float32
BF16 = jnp.bfloat16

GLA_GATE_NORMALIZER = 16.0
N_GROUPS = 8
TOPK_GROUPS = 4
TOP_K = 8
ROUTED_SCALE = 2.5
EPS = 1e-5

SUBLANES = 8
LANES = 128
VMEM_LIMIT_BYTES = 56 * 1024 * 1024

TOKEN_TILE = 512
MOE_BLOCK = 4096
MOE_ROWS = 256
MOE_ROWS_SMALL = 32
RMW_BATCH = 16
CHUNK = 128
SUB = SUBLANES
NEG_BIG = -1e30


def _cparams(sem):
    return pltpu.CompilerParams(dimension_semantics=sem, vmem_limit_bytes=VMEM_LIMIT_BYTES)


def _dot(a, b):
    return jnp.dot(a, b, preferred_element_type=F32)


def _dot_nt(a, b):
    return lax.dot_general(a, b, (((1,), (1,)), ((), ())), preferred_element_type=F32)


def _silu(x):
    return x * jax.nn.sigmoid(x)


def _log_sigmoid(x):
    return jnp.minimum(x, 0.0) - jnp.log1p(jnp.exp(-jnp.abs(x)))


def _layernorm(r, g, b):
    mu = jnp.mean(r, axis=-1, keepdims=True)
    d = r - mu
    var = jnp.mean(d * d, axis=-1, keepdims=True)
    return d * lax.rsqrt(var + EPS) * g + b


def _ada_kernel(c_ref, w_ref, b_ref, o_ref):
    c = c_ref[...]
    o_ref[...] = _dot(_silu(c).astype(BF16), w_ref[...].astype(BF16)) + b_ref[...]


def _ada_mod(c, w_ada, b_ada):
    R, D = c.shape
    N = w_ada.shape[1]
    tn = D
    return pl.pallas_call(
        _ada_kernel,
        grid=(N // tn,),
        in_specs=[pl.BlockSpec((R, D), lambda j: (0, 0)),
                  pl.BlockSpec((D, tn), lambda j: (0, j)),
                  pl.BlockSpec((1, tn), lambda j: (0, j))],
        out_specs=pl.BlockSpec((R, tn), lambda j: (0, j)),
        out_shape=jax.ShapeDtypeStruct((R, N), F32),
        compiler_params=_cparams(("arbitrary",)),
        name="ada_mod",
    )(c, w_ada, b_ada.reshape(1, N))


def _chunk_masks(C):
    row = lax.broadcasted_iota(jnp.int32, (C, 1), 0)
    ri = lax.broadcasted_iota(jnp.int32, (C, C), 0)
    ci = lax.broadcasted_iota(jnp.int32, (C, C), 1)
    levels = []
    s = SUB
    while s < C:
        same_group = (ri // (2 * s)) == (ci // (2 * s))
        levels.append((s, same_group))
        s *= 2
    diag = (ri // SUB) == (ci // SUB)
    return row, levels, diag


def _bcast_rows(x, group, idx):
    C, K = x.shape
    G = C // group
    x3 = x.reshape(G, group, K)
    return jnp.broadcast_to(x3[:, idx:idx + 1, :], (G, group, K)).reshape(C, K)


def _chunk_head(q, k, la, v, st, sel, masks):
    C, K = q.shape
    row, levels, diag = masks
    rmod = row % SUB

    x3 = la.reshape(C // SUB, SUB, K)
    sub3 = lax.broadcasted_iota(jnp.int32, (1, SUB, 1), 1)
    sh = 1
    while sh < SUB:
        x3 = x3 + jnp.where(sub3 >= sh, pltpu.roll(x3, sh, 1), 0.0)
        sh *= 2
    x = x3.reshape(C, K)
    x_sub = x

    sc = jnp.zeros((C, C), F32)
    for s, same_group in levels:
        G = C // (2 * s)
        x4 = x.reshape(G, 2, s, K)
        xl, xr = x4[:, 0], x4[:, 1]
        yl = jnp.broadcast_to(xl[:, s - 1:s, :], (G, s, K))
        qr = q.reshape(G, 2, s, K)[:, 1] * jnp.exp(xr)
        kl = k.reshape(G, 2, s, K)[:, 0] * jnp.exp(yl - xl)
        zero = jnp.zeros((G, s, K), F32)
        qf = jnp.stack([zero, qr], axis=1).reshape(C, K).astype(BF16)
        kf = jnp.stack([kl, zero], axis=1).reshape(C, K).astype(BF16)
        sc = sc + jnp.where(same_group, _dot_nt(qf, kf), 0.0)
        x = jnp.stack([xl, xr + yl], axis=1).reshape(C, K)
    b = x

    terms = []
    for jj in range(SUB):
        kb = _bcast_rows(k, SUB, jj)
        xb = _bcast_rows(x_sub, SUB, jj)
        e = jnp.where(rmod >= jj, x_sub - xb, NEG_BIG)
        terms.append((q * kb * jnp.exp(e)).astype(BF16))
    d = _dot(jnp.concatenate(terms, axis=1), sel)
    sc = sc + jnp.where(diag, d, 0.0)

    vb = v.astype(BF16)
    o = _dot(sc.astype(BF16), vb) + _dot_nt((q * jnp.exp(b)).astype(BF16), st.astype(BF16))
    b_last = b[C - 1:C, :]
    kd = (k * jnp.exp(b_last - b)).astype(BF16)
    st_new = st * jnp.exp(b_last) + _dot(v.T.astype(BF16), kd)
    return o, st_new


def _recurrence_tile(q_ref, k_ref, la_ref, v_ref, o_ref, st_ref, sel_ref, n_heads, K, V, T):
    C = CHUNK
    masks = _chunk_masks(C)
    sel = sel_ref[...]

    def body(c, carry):
        r0 = pl.multiple_of(c * C, C)
        for h in range(n_heads):
            ks = slice(h * K, (h + 1) * K)
            vs = slice(h * V, (h + 1) * V)
            o, st_new = _chunk_head(q_ref[pl.ds(r0, C), ks], k_ref[pl.ds(r0, C), ks],
                                    la_ref[pl.ds(r0, C), ks], v_ref[pl.ds(r0, C), vs],
                                    st_ref[h], sel, masks)
            o_ref[pl.ds(r0, C), vs] = o
            st_ref[h] = st_new
        return carry

    lax.fori_loop(0, T // C, body, 0)


def _branch_kernel(*refs, kind, n_heads, K, V, T, layer):
    if kind == "gla":
        (x_ref, sh_ref, sc_ref, wq_ref, wk_ref, wv_ref, wg_ref, wgk1_ref, wgk2_ref, bgk_ref,
         nw_ref, wp_ref, sel_ref, y_ref, sout_ref,
         q_s, k_s, la_s, v_s, g_s, o_s, st_s) = refs
    else:
        (x_ref, sh_ref, sc_ref, wq_ref, wk_ref, wv_ref, wg_ref, lb_ref,
         nw_ref, wp_ref, sel_ref, y_ref, sout_ref,
         q_s, k_s, la_s, v_s, g_s, o_s, st_s) = refs
    lt = pl.program_id(1)

    @pl.when(lt == 0)
    def _():
        st_s[...] = jnp.zeros_like(st_s)

    h = (x_ref[0] * (1.0 + sc_ref[0]) + sh_ref[0]).astype(BF16)
    scale = K ** -0.5
    if kind == "gla":
        q_s[...] = _dot(h, wq_ref[...]) * scale
        k_s[...] = _dot(h, wk_ref[...])
        lr = _dot(h, wgk1_ref[...]).astype(BF16)
        la_s[...] = _log_sigmoid(_dot(lr, wgk2_ref[...]) + bgk_ref[...]) * (1.0 / GLA_GATE_NORMALIZER)
    else:
        q_s[...] = _silu(_dot(h, wq_ref[...])) * scale
        lbp = lb_ref[...]
        e = jnp.exp(lbp - jnp.max(lbp, axis=0, keepdims=True))
        lb = jnp.sum(e[:layer + 1], axis=0, keepdims=True) / jnp.sum(e, axis=0, keepdims=True)
        forget = lb + (1.0 - lb) * jax.nn.sigmoid(_dot(h, wk_ref[...]))
        k_s[...] = 1.0 - forget
        la_s[...] = jnp.log(forget)
    v_s[...] = _dot(h, wv_ref[...])
    g_s[...] = _dot(h, wg_ref[...])

    _recurrence_tile(q_s, k_s, la_s, v_s, o_s, st_s, sel_ref, n_heads, K, V, T)

    nw = nw_ref[...]
    outs = []
    for hd in range(n_heads):
        vs = slice(hd * V, (hd + 1) * V)
        o = o_s[:, vs]
        g = g_s[:, vs]
        gate = _silu(g) if kind == "gla" else jax.nn.sigmoid(g)
        o = o * lax.rsqrt(jnp.mean(o * o, axis=-1, keepdims=True) + EPS) * nw * gate
        outs.append(o.astype(BF16))
    y_ref[0] = _dot(jnp.concatenate(outs, axis=1), wp_ref[...])

    @pl.when(lt == pl.num_programs(1) - 1)
    def _():
        for hd in range(n_heads):
            sout_ref[0, hd] = st_s[hd].T


def _sel_matrix(K, C):
    r = jnp.arange(SUB * K, dtype=jnp.int32)[:, None] // K
    c = jnp.arange(C, dtype=jnp.int32)[None, :] % SUB
    return (r == c).astype(BF16)


def _const_spec(shape):
    nd = len(shape)
    return pl.BlockSpec(shape, lambda b, l: (0,) * nd)


def _branch_prompt(kind, x, mod3, weights, norm_w, w_proj, n_heads, K, V, layer):
    B, L, D = x.shape
    T = min(TOKEN_TILE, L)
    HK, HV = n_heads * K, n_heads * V
    sel = _sel_matrix(K, CHUNK)
    x_spec = pl.BlockSpec((1, T, D), lambda b, l: (b, l, 0))
    sh_spec = pl.BlockSpec((1, 1, D), lambda b, l: (b, 0, 0))
    sc_spec = pl.BlockSpec((1, 1, D), lambda b, l: (b, 0, 1))
    w_specs = [_const_spec(w.shape) for w in weights]
    nw2 = norm_w.reshape(1, V)
    in_specs = [x_spec, sh_spec, sc_spec] + w_specs + [_const_spec(nw2.shape), _const_spec(w_proj.shape),
                                                       _const_spec(sel.shape)]
    kern = functools.partial(_branch_kernel, kind=kind, n_heads=n_heads, K=K, V=V, T=T, layer=layer)
    return pl.pallas_call(
        kern,
        grid=(B, L // T),
        in_specs=in_specs,
        out_specs=[pl.BlockSpec((1, T, D), lambda b, l: (b, l, 0)),
                   pl.BlockSpec((1, n_heads, K, V), lambda b, l: (b, 0, 0, 0))],
        out_shape=[jax.ShapeDtypeStruct((B, L, D), F32),
                   jax.ShapeDtypeStruct((B, n_heads, K, V), F32)],
        scratch_shapes=[pltpu.VMEM((T, HK), F32), pltpu.VMEM((T, HK), F32), pltpu.VMEM((T, HK), F32),
                        pltpu.VMEM((T, HV), F32), pltpu.VMEM((T, HV), F32), pltpu.VMEM((T, HV), F32),
                        pltpu.VMEM((n_heads, V, K), F32)],
        compiler_params=_cparams(("arbitrary", "arbitrary")),
        name=f"{kind}_prompt",
    )(x, mod3, mod3, *weights, nw2, w_proj, sel)


def _sample_kernel(*refs, kind, n_heads, K, V, TB, layer):
    if kind == "gla":
        (x_ref, sh_ref, sc_ref, wq_ref, wk_ref, wv_ref, wg_ref, wgk1_ref, wgk2_ref, bgk_ref,
         nw_ref, wp_ref, s_ref, y_ref, sout_ref, qT_s, kT_s, aT_s, v_s, g_s, o_s) = refs
    else:
        (x_ref, sh_ref, sc_ref, wq_ref, wk_ref, wv_ref, wg_ref, lb_ref,
         nw_ref, wp_ref, s_ref, y_ref, sout_ref, qT_s, kT_s, aT_s, v_s, g_s, o_s) = refs
    step = pl.program_id(0)
    NT = x_ref.shape[0]

    @pl.when(step == 0)
    def _():
        h = (x_ref[...] * (1.0 + sc_ref[...]) + sh_ref[...]).astype(BF16)
        scale = K ** -0.5
        if kind == "gla":
            q = _dot(h, wq_ref[...]) * scale
            k = _dot(h, wk_ref[...])
            lr = _dot(h, wgk1_ref[...]).astype(BF16)
            a = jnp.exp(_log_sigmoid(_dot(lr, wgk2_ref[...]) + bgk_ref[...]) * (1.0 / GLA_GATE_NORMALIZER))
        else:
            q = _silu(_dot(h, wq_ref[...])) * scale
            lbp = lb_ref[...]
            e = jnp.exp(lbp - jnp.max(lbp, axis=0, keepdims=True))
            lb = jnp.sum(e[:layer + 1], axis=0, keepdims=True) / jnp.sum(e, axis=0, keepdims=True)
            a = lb + (1.0 - lb) * jax.nn.sigmoid(_dot(h, wk_ref[...]))
            k = 1.0 - a
        for hd in range(n_heads):
            ks = slice(hd * K, (hd + 1) * K)
            qT_s[ks, :] = q[:, ks].T
            kT_s[ks, :] = k[:, ks].T
            aT_s[ks, :] = a[:, ks].T
        v_s[...] = _dot(h, wv_ref[...])
        g_s[...] = _dot(h, wg_ref[...])

    lane = lax.broadcasted_iota(jnp.int32, (1, NT), 1)
    sub = lax.broadcasted_iota(jnp.int32, (TB, 1), 0)
    t0 = pl.multiple_of(step * TB, TB)
    for hd in range(n_heads):
        ks = slice(hd * K, (hd + 1) * K)
        vs = slice(hd * V, (hd + 1) * V)
        v_rows = v_s[pl.ds(t0, TB), vs]
        o_rows = jnp.zeros((TB, V), F32)
        for j in range(TB):
            pick = lane == t0 + j
            acol = jnp.sum(jnp.where(pick, aT_s[ks, :], 0.0), axis=1, keepdims=True)
            kcol = jnp.sum(jnp.where(pick, kT_s[ks, :], 0.0), axis=1, keepdims=True)
            qcol = jnp.sum(jnp.where(pick, qT_s[ks, :], 0.0), axis=1, keepdims=True)
            s1 = acol * s_ref[j, hd] + kcol * v_rows[j:j + 1, :]
            sout_ref[j, hd] = s1
            o_rows = jnp.where(sub == j, jnp.sum(qcol * s1, axis=0, keepdims=True), o_rows)
        o_s[pl.ds(t0, TB), vs] = o_rows

    @pl.when(step == pl.num_programs(0) - 1)
    def _():
        nw = nw_ref[...]
        outs = []
        for hd in range(n_heads):
            vs = slice(hd * V, (hd + 1) * V)
            o = o_s[:, vs]
            g = g_s[:, vs]
            gate = _silu(g) if kind == "gla" else jax.nn.sigmoid(g)
            o = o * lax.rsqrt(jnp.mean(o * o, axis=-1, keepdims=True) + EPS) * nw * gate
            outs.append(o.astype(BF16))
        y_ref[...] = _dot(jnp.concatenate(outs, axis=1), wp_ref[...])


def _branch_sample(kind, x, mod, weights, norm_w, w_proj, state, layer):
    NT, D = x.shape
    _, n_heads, K, V = state.shape
    HK, HV = n_heads * K, n_heads * V
    TB = SUBLANES
    c1 = lambda s: pl.BlockSpec(s, lambda i: (0,) * len(s))
    nw2 = norm_w.reshape(1, V)
    in_specs = ([c1((NT, D)), pl.BlockSpec((NT, D), lambda i: (0, 0)), pl.BlockSpec((NT, D), lambda i: (0, 1))]
                + [c1(w.shape) for w in weights] + [c1(nw2.shape), c1(w_proj.shape),
                                                    pl.BlockSpec((TB, n_heads, K, V), lambda i: (i, 0, 0, 0))])
    kern = functools.partial(_sample_kernel, kind=kind, n_heads=n_heads, K=K, V=V, TB=TB, layer=layer)
    return pl.pallas_call(
        kern,
        grid=(NT // TB,),
        in_specs=in_specs,
        out_specs=[c1((NT, D)), pl.BlockSpec((TB, n_heads, K, V), lambda i: (i, 0, 0, 0))],
        out_shape=[jax.ShapeDtypeStruct((NT, D), F32), jax.ShapeDtypeStruct(state.shape, F32)],
        scratch_shapes=[pltpu.VMEM((HK, NT), F32), pltpu.VMEM((HK, NT), F32), pltpu.VMEM((HK, NT), F32),
                        pltpu.VMEM((NT, HV), F32), pltpu.VMEM((NT, HV), F32), pltpu.VMEM((NT, HV), F32)],
        compiler_params=_cparams(("arbitrary",)),
        name=f"{kind}_sample",
    )(x, mod, mod, *weights, nw2, w_proj, state)


def _merge_kernel(x_ref, ya_ref, yb_ref, sh_ref, sc_ref, g_ref, wu_ref, wo_ref, lg_ref, lb_ref, o_ref, *, alpha):
    x = x_ref[0]
    D = x.shape[-1]
    h = (x * (1.0 + sc_ref[0]) + sh_ref[0]).astype(BF16)
    u = _dot(h, wu_ref[...])
    merged = jax.nn.sigmoid(u[:, :D]) * ya_ref[0] + jax.nn.sigmoid(u[:, D:]) * yb_ref[0]
    mix = _dot(merged.astype(BF16), wo_ref[...])
    o_ref[0] = _layernorm(alpha * x + g_ref[0] * mix, lg_ref[...], lb_ref[...])


def _mod_specs(mod3, cols, T):
    D = mod3.shape[-1] // 6
    if mod3.shape[1] == 1:
        return [pl.BlockSpec((1, 1, D), functools.partial(lambda b, l, *_, c: (b, 0, c), c=c)) for c in cols]
    return [pl.BlockSpec((1, T, D), functools.partial(lambda b, l, *_, c: (b, l, c), c=c)) for c in cols]


def _merge(x, ya, yb, mod3, wu, w_out, ln_g, ln_b, alpha):
    B, L, D = x.shape
    T = min(TOKEN_TILE, L)
    tok = pl.BlockSpec((1, T, D), lambda b, l: (b, l, 0))
    return pl.pallas_call(
        functools.partial(_merge_kernel, alpha=alpha),
        grid=(B, L // T),
        in_specs=[tok, tok, tok] + _mod_specs(mod3, (0, 1, 2), T)
        + [_const_spec(wu.shape), _const_spec(w_out.shape), _const_spec((1, D)), _const_spec((1, D))],
        out_specs=tok,
        out_shape=jax.ShapeDtypeStruct((B, L, D), F32),
        compiler_params=_cparams(("arbitrary", "arbitrary")),
        name="merge",
    )(x, ya, yb, mod3, mod3, mod3, wu, w_out, ln_g.reshape(1, D), ln_b.reshape(1, D))


def _first_argmax(vals, iota, n, axis):
    m = jnp.max(vals, axis=axis, keepdims=True)
    idx = jnp.min(jnp.where(vals == m, iota, n), axis=axis, keepdims=True)
    return m, idx


def _router_kernel(x_ref, sh_ref, sc_ref, wrT_ref, bias_ref, eidx_ref, egate_ref, xg_ref, *, n_experts):
    E = n_experts
    per = E // N_GROUPS
    hf = x_ref[0] * (1.0 + sc_ref[0]) + sh_ref[0]
    h = hf.astype(BF16)
    T = h.shape[0]
    for s in range(hf.shape[1] // LANES):
        xg_ref[pl.ds(s, T, stride=SUBLANES), :] = hf[:, s * LANES:(s + 1) * LANES]
    scores = jax.nn.sigmoid(_dot_nt(wrT_ref[...], h))
    biased = scores + bias_ref[...]
    b3 = biased.reshape(N_GROUPS, per, T)
    i3 = lax.broadcasted_iota(jnp.int32, (N_GROUPS, per, T), 1)
    m1, a1 = _first_argmax(b3, i3, per, 1)
    m2 = jnp.max(jnp.where(i3 == a1, -jnp.inf, b3), axis=1, keepdims=True)
    gscore = (m1 + m2).reshape(N_GROUPS, T)
    gi = lax.broadcasted_iota(jnp.int32, (N_GROUPS, T), 0)
    gsel = jnp.zeros((N_GROUPS, T), jnp.bool_)
    for _ in range(TOPK_GROUPS):
        _, a = _first_argmax(gscore, gi, N_GROUPS, 0)
        hit = gi == a
        gsel = jnp.logical_or(gsel, hit)
        gscore = jnp.where(hit, -jnp.inf, gscore)
    emask = jnp.broadcast_to(gsel.reshape(N_GROUPS, 1, T), (N_GROUPS, per, T)).reshape(E, T)
    cand = jnp.where(emask, biased, -jnp.inf)
    ei = lax.broadcasted_iota(jnp.int32, (E, T), 0)
    picks, weights = [], []
    for _ in range(TOP_K):
        _, a = _first_argmax(cand, ei, E, 0)
        hit = ei == a
        picks.append(a)
        weights.append(jnp.sum(jnp.where(hit, scores, 0.0), axis=0, keepdims=True))
        cand = jnp.where(hit, -jnp.inf, cand)
    w = jnp.concatenate(weights, axis=0)
    egate_ref[...] = w / jnp.sum(w, axis=0, keepdims=True) * ROUTED_SCALE
    eidx_ref[...] = jnp.concatenate(picks, axis=0)


def _router(x1, mod3, wrT, bias):
    B, L, D = x1.shape
    assert D == SUBLANES * LANES
    E = wrT.shape[0]
    T = min(TOKEN_TILE, L)
    nl = L // T
    N = B * L
    tok = pl.BlockSpec((1, T, D), lambda b, l: (b, l, 0))
    pick_spec = pl.BlockSpec((TOP_K, T), lambda b, l: (0, b * nl + l))
    return pl.pallas_call(
        functools.partial(_router_kernel, n_experts=E),
        grid=(B, nl),
        in_specs=[tok] + _mod_specs(mod3, (3, 4), T) + [_const_spec(wrT.shape), _const_spec((E, 1))],
        out_specs=[pick_spec, pick_spec, pl.BlockSpec((T * SUBLANES, LANES), lambda b, l: (b * nl + l, 0))],
        out_shape=[jax.ShapeDtypeStruct((TOP_K, N), jnp.int32), jax.ShapeDtypeStruct((TOP_K, N), F32),
                   jax.ShapeDtypeStruct((N * SUBLANES, LANES), F32)],
        compiler_params=_cparams(("arbitrary", "arbitrary")),
        name="router",
    )(x1, mod3, mod3, wrT, bias.reshape(E, 1))


def _tile_schedule(eidx, egate, NB, E, R):
    Kp, N = eidx.shape
    nb = N // NB
    A = Kp * NB
    S = SUBLANES
    assert A % R == 0
    NW = A // R
    tok = jnp.broadcast_to(jnp.arange(N, dtype=jnp.int32)[None, :], (Kp, N))
    assert nb * E * NB < 2 ** 31
    key = ((tok // NB) * E + eidx) * NB + tok % NB
    skey, sgate = lax.sort((key.reshape(-1), egate.reshape(-1)), num_keys=1)
    w_e = ((skey // NB) % E).reshape(nb, NW, R)
    w_t = ((skey % NB) * S).reshape(nb, NW, R)
    w_g = sgate.reshape(nb, NW, R)
    first = w_e[:, :, 0]
    npair = w_e[:, :, R - 1] - first + 1
    cum = jnp.cumsum(npair, axis=1)
    ntiles = cum[:, -1]
    SL = NW + E + 3
    q = jnp.arange(SL, dtype=jnp.int32)[None, :] - 2
    qc = jnp.clip(q, 0, ntiles[:, None] - 1)
    k_q = jnp.minimum(jnp.sum((cum[:, None, :] <= qc[:, :, None]).astype(jnp.int32), axis=2), NW - 1)
    onehot = (k_q[:, :, None] == jnp.arange(NW, dtype=jnp.int32)[None, None, :]).astype(F32)
    sel = lambda a: jnp.einsum('bsk,bkr->bsr', onehot, a.astype(F32), precision=lax.Precision.HIGHEST)
    selk = lambda a: jnp.sum(onehot * a.astype(F32)[:, None, :], axis=2).astype(jnp.int32)
    e_q = jnp.clip(selk(first) + qc - selk(cum - npair), 0, E - 1)
    real = jnp.logical_and(q >= 0, q < ntiles[:, None])
    match = jnp.logical_and(real[:, :, None], sel(w_e).astype(jnp.int32) == e_q[:, :, None])
    rows = jnp.where(match, sel(w_t).astype(jnp.int32), NB * S).reshape(nb * SL, 1, R)
    gate = jnp.where(match, sel(w_g), 0.0).reshape(nb * SL, 1, R)
    return e_q.reshape(-1), ntiles, rows, gate, SL


def _moe_step(src_ref, dst_ref, gate_ref, xg_s, acc_s, wg_ref, wu_ref, wd_ref, gbuf, cbuf, cy, sy, R):
    S = SUBLANES
    for r in range(R):
        t0 = pl.multiple_of(src_ref[0, 0, r], S)
        gbuf[r * S:(r + 1) * S, :] = xg_s[pl.ds(t0, S), :]

    x = jnp.concatenate([cbuf[pl.ds(s, R, stride=S), :] for s in range(S)], axis=1).astype(BF16)
    a = _silu(_dot(x, wg_ref[0].astype(BF16))) * _dot(x, wu_ref[0].astype(BF16))
    y = _dot(a.astype(BF16), wd_ref[0].astype(BF16))
    for s in range(S):
        cy[pl.ds(s, R, stride=S), :] = y[:, s * LANES:(s + 1) * LANES]

    for r0 in range(0, R, RMW_BATCH):
        upd = []
        for r in range(r0, r0 + RMW_BATCH):
            a0 = pl.multiple_of(dst_ref[0, 0, r], S)
            upd.append((a0, acc_s[pl.ds(a0, S), :] + gate_ref[0, 0, r] * sy[r * S:(r + 1) * S, :]))
        for a0, val in upd:
            acc_s[pl.ds(a0, S), :] = val


def _moe_kernel(te_ref, nt_ref, src_ref, dst_ref, gate_ref, xg_hbm, wg_ref, wu_ref, wd_ref, out_hbm,
                xg_s, acc_s, buf0, buf1, y0, y1, *, NB, R):
    del te_ref
    b = pl.program_id(0)
    q = pl.program_id(1)

    @pl.when(jnp.logical_and(b == 0, q == 0))
    def _():
        for ref in (buf0, buf1, y0, y1):
            ref[...] = jnp.zeros_like(ref)

    @pl.when(q == 0)
    def _():
        pltpu.sync_copy(xg_hbm.at[b], xg_s.at[pl.ds(0, NB * SUBLANES)])
        xg_s[pl.ds(NB * SUBLANES, SUBLANES), :] = jnp.zeros((SUBLANES, LANES), F32)
        acc_s[...] = jnp.zeros_like(acc_s)

    active = q < nt_ref[b] + 2
    args = (src_ref, dst_ref, gate_ref, xg_s, acc_s, wg_ref, wu_ref, wd_ref)

    @pl.when(jnp.logical_and(active, q % 2 == 0))
    def _():
        _moe_step(*args, buf0, buf1, y1, y0, R)

    @pl.when(jnp.logical_and(active, q % 2 == 1))
    def _():
        _moe_step(*args, buf1, buf0, y0, y1, R)

    @pl.when(q == pl.num_programs(1) - 1)
    def _():
        pltpu.sync_copy(acc_s.at[pl.ds(0, NB * SUBLANES)], out_hbm.at[b])


def _moe_routed(xg, eidx, egate, wg, wu, wd):
    N = eidx.shape[1]
    E, D, DE = wg.shape
    NB = min(MOE_BLOCK, N)
    nb = N // NB
    S = SUBLANES
    R = MOE_ROWS if NB >= MOE_BLOCK else MOE_ROWS_SMALL
    te, ntiles, rows, gate, SL = _tile_schedule(eidx, egate, NB, E, R)
    smem = lambda shift: pl.BlockSpec((1, 1, R), lambda b, q, *_: (b * SL + q + shift, 0, 0),
                                      memory_space=pltpu.SMEM)
    w_map = lambda b, q, te_ref, nt_ref: (te_ref[b * SL + q + 1], 0, 0)
    tile_rows = pltpu.VMEM((R * S, LANES), F32)
    grid_spec = pltpu.PrefetchScalarGridSpec(
        num_scalar_prefetch=2,
        grid=(nb, SL - 2),
        in_specs=[smem(2), smem(0), smem(0),
                  pl.BlockSpec(memory_space=pl.ANY),
                  pl.BlockSpec((1, D, DE), w_map), pl.BlockSpec((1, D, DE), w_map), pl.BlockSpec((1, DE, D), w_map)],
        out_specs=pl.BlockSpec(memory_space=pl.ANY),
        scratch_shapes=[pltpu.VMEM(((NB + 1) * S, LANES), F32), pltpu.VMEM(((NB + 1) * S, LANES), F32),
                        tile_rows, tile_rows, tile_rows, tile_rows],
    )
    return pl.pallas_call(
        functools.partial(_moe_kernel, NB=NB, R=R),
        grid_spec=grid_spec,
        out_shape=jax.ShapeDtypeStruct((nb, NB * S, LANES), F32),
        compiler_params=_cparams(("arbitrary", "arbitrary")),
        name="moe_routed",
    )(te, ntiles, rows, rows, gate, xg.reshape(nb, NB * S, LANES), wg, wu, wd)


def _combine_kernel(x_ref, sh_ref, sc_ref, g2_ref, r_ref, sg_ref, su_ref, sd_ref, lg_ref, lb_ref, o_ref, *, alpha):
    x = x_ref[0]
    T = x.shape[0]
    h = (x * (1.0 + sc_ref[0]) + sh_ref[0]).astype(BF16)
    a = _silu(_dot(h, sg_ref[...])) * _dot(h, su_ref[...])
    shared = _dot(a.astype(BF16), sd_ref[...])
    routed = jnp.concatenate([r_ref[0, pl.ds(s, T, stride=SUBLANES), :] for s in range(SUBLANES)], axis=1)
    o_ref[0] = _layernorm(alpha * x + g2_ref[0] * (routed + shared), lg_ref[...], lb_ref[...])


def _combine(x1, mod3, routed, sg, su, sd, ln_g, ln_b, alpha):
    B, L, D = x1.shape
    T = min(TOKEN_TILE, L)
    nl = L // T
    NB = routed.shape[1] // SUBLANES
    per = NB // T
    tok = pl.BlockSpec((1, T, D), lambda b, l: (b, l, 0))
    r_spec = pl.BlockSpec((1, T * SUBLANES, LANES), lambda b, l: ((b * nl + l) // per, (b * nl + l) % per, 0))
    return pl.pallas_call(
        functools.partial(_combine_kernel, alpha=alpha),
        grid=(B, nl),
        in_specs=[tok] + _mod_specs(mod3, (3, 4, 5), T) + [r_spec]
        + [_const_spec(sg.shape), _const_spec(su.shape), _const_spec(sd.shape), _const_spec((1, D)),
           _const_spec((1, D))],
        out_specs=tok,
        out_shape=jax.ShapeDtypeStruct((B, L, D), F32),
        compiler_params=_cparams(("arbitrary", "arbitrary")),
        name="combine",
    )(x1, mod3, mod3, mod3, routed, sg, su, sd, ln_g.reshape(1, D), ln_b.reshape(1, D))


def _split_w_in(w_in_l, gla_shape, hgrn_shape, D):
    Hg, Kg, Vg = gla_shape
    Hh, Kh, Vh = hgrn_shape
    rank = w_in_l.shape[1] - (2 * Hg * Kg + 2 * Hg * Vg + 2 * Hh * Kh + 2 * Hh * Vh + 2 * D)
    widths = (Hg * Kg, Hg * Kg, Hg * Vg, Hg * Vg, rank, Hh * Kh, Hh * Kh, Hh * Vh, Hh * Vh, 2 * D)
    out, start = [], 0
    for w in widths:
        out.append(w_in_l[:, start:start + w].astype(BF16))
        start += w
    return out


def kernel(x_prompt, x_sample, state_gla, state_hgrn, c_prompt, c_sample, w_ada, b_ada, w_in, w_gk2, b_gk,
           hgrn_lb, gla_norm_w, hgrn_norm_w, w_proj_a, w_proj_b, w_out, ln1_g, ln1_b, w_router, router_bias,
           w_exp_gate, w_exp_up, w_exp_down, w_sh_gate, w_sh_up, w_sh_down, ln2_g, ln2_b):
    depth = w_in.shape[0]
    BP, L, D = x_prompt.shape
    NS = x_sample.shape[0]
    assert x_sample.shape[1] == 1
    gla_shape = state_gla.shape[2:]
    hgrn_shape = state_hgrn.shape[2:]
    alpha = (2.0 * depth) ** 0.25

    xp = x_prompt
    xs = x_sample.reshape(NS, D)
    c_all = jnp.concatenate([c_prompt, c_sample], axis=0)
    new_gla_p, new_hgrn_p, new_gla_s, new_hgrn_s = [], [], [], []
    for l in range(depth):
        mod = _ada_mod(c_all, w_ada[l], b_ada[l])
        mod_p = mod[:BP].reshape(BP, 1, 6 * D)
        mod_s = mod[BP:]
        (wqa, wka, wva, wga, wgk1, wqb, wfb, wib, wgb, wuab) = _split_w_in(w_in[l], gla_shape, hgrn_shape, D)
        gla_w = [wqa, wka, wva, wga, wgk1, w_gk2[l].astype(BF16), b_gk[l].reshape(1, -1)]
        hgrn_w = [wqb, wfb, wib, wgb, hgrn_lb]
        wpa = w_proj_a[l].astype(BF16)
        wpb = w_proj_b[l].astype(BF16)
        wo = w_out[l].astype(BF16)
        wrT = w_router[l].T.astype(BF16)
        eg, eu, ed = w_exp_gate[l], w_exp_up[l], w_exp_down[l]
        sg, su, sd = w_sh_gate[l].astype(BF16), w_sh_up[l].astype(BF16), w_sh_down[l].astype(BF16)

        def tail(x3, ya, yb, mod3):
            x1 = _merge(x3, ya, yb, mod3, wuab, wo, ln1_g[l], ln1_b[l], alpha)
            eidx, egate, xg = _router(x1, mod3, wrT, router_bias[l])
            routed = _moe_routed(xg, eidx, egate, eg, eu, ed)
            return _combine(x1, mod3, routed, sg, su, sd, ln2_g[l], ln2_b[l], alpha)

        ya, sg_p = _branch_prompt("gla", xp, mod_p, gla_w, gla_norm_w[l], wpa, *gla_shape, layer=l)
        yb, sh_p = _branch_prompt("hgrn", xp, mod_p, hgrn_w, hgrn_norm_w[l], wpb, *hgrn_shape, layer=l)
        xp = tail(xp, ya, yb, mod_p)
        new_gla_p.append(sg_p)
        new_hgrn_p.append(sh_p)

        ya, sg_s = _branch_sample("gla", xs, mod_s, gla_w, gla_norm_w[l], wpa, state_gla[l], layer=l)
        yb, sh_s = _branch_sample("hgrn", xs, mod_s, hgrn_w, hgrn_norm_w[l], wpb, state_hgrn[l], layer=l)
        xs = tail(xs[None], ya[None], yb[None], mod_s[None])[0]
        new_gla_s.append(sg_s)
        new_hgrn_s.append(sh_s)

    return (xp, xs.reshape(NS, 1, D), jnp.stack(new_gla_p), jnp.stack(new_hgrn_p),
            jnp.stack(new_gla_s), jnp.stack(new_hgrn_s))
```

```python
import functools

import jax
import jax.numpy as jnp
from jax import lax
from jax.experimental import pallas as pl
from jax.experimental.pallas import tpu as pltpu

F32 = jnp.float32
BF16 = jnp.bfloat16

GLA_GATE_NORMALIZER = 16.0
N_GROUPS = 8
TOPK_GROUPS = 4
TOP_K = 8
ROUTED_SCALE = 2.5
EPS = 1e-5

SUBLANES = 8
LANES = 128
VMEM_LIMIT_BYTES = 56 * 1024 * 1024

TOKEN_TILE = 512
MOE_BLOCK = 4096
MOE_ROWS = 256
MOE_ROWS_SMALL = 32
RMW_BATCH = 8
CHUNK = 128
SUB = SUBLANES
NEG_BIG = -1e30


def _cparams(sem):
    return pltpu.CompilerParams(dimension_semantics=sem, vmem_limit_bytes=VMEM_LIMIT_BYTES)


def _dot(a, b):
    return jnp.dot(a, b, preferred_element_type=F32)


def _dot_nt(a, b):
    return lax.dot_general(a, b, (((1,), (1,)), ((), ())), preferred_element_type=F32)


def _silu(x):
    return x * jax.nn.sigmoid(x)


def _log_sigmoid(x):
    return jnp.minimum(x, 0.0) - jnp.log1p(jnp.exp(-jnp.abs(x)))


def _layernorm(r, g, b):
    mu = jnp.mean(r, axis=-1, keepdims=True)
    d = r - mu
    var = jnp.mean(d * d, axis=-1, keepdims=True)
    return d * lax.rsqrt(var + EPS) * g + b


def _ada_kernel(c_ref, w_ref, b_ref, o_ref):
    c = c_ref[...]
    o_ref[...] = _dot(_silu(c).astype(BF16), w_ref[...].astype(BF16)) + b_ref[...]


def _ada_mod(c, w_ada, b_ada):
    R, D = c.shape
    N = w_ada.shape[1]
    tn = D
    return pl.pallas_call(
        _ada_kernel,
        grid=(N // tn,),
        in_specs=[pl.BlockSpec((R, D), lambda j: (0, 0)),
                  pl.BlockSpec((D, tn), lambda j: (0, j)),
                  pl.BlockSpec((1, tn), lambda j: (0, j))],
        out_specs=pl.BlockSpec((R, tn), lambda j: (0, j)),
        out_shape=jax.ShapeDtypeStruct((R, N), F32),
        compiler_params=_cparams(("arbitrary",)),
        name="ada_mod",
    )(c, w_ada, b_ada.reshape(1, N))


def _chunk_masks(C):
    row = lax.broadcasted_iota(jnp.int32, (C, 1), 0)
    ri = lax.broadcasted_iota(jnp.int32, (C, C), 0)
    ci = lax.broadcasted_iota(jnp.int32, (C, C), 1)
    levels = []
    s = SUB
    while s < C:
        same_group = (ri // (2 * s)) == (ci // (2 * s))
        levels.append((s, same_group))
        s *= 2
    diag = (ri // SUB) == (ci // SUB)
    return row, levels, diag


def _bcast_rows(x, group, idx):
    C, K = x.shape
    G = C // group
    x3 = x.reshape(G, group, K)
    return jnp.broadcast_to(x3[:, idx:idx + 1, :], (G, group, K)).reshape(C, K)


def _chunk_head(q, k, la, v, st, sel, masks):
    C, K = q.shape
    row, levels, diag = masks
    rmod = row % SUB

    x3 = la.reshape(C // SUB, SUB, K)
    sub3 = lax.broadcasted_iota(jnp.int32, (1, SUB, 1), 1)
    sh = 1
    while sh < SUB:
        x3 = x3 + jnp.where(sub3 >= sh, pltpu.roll(x3, sh, 1), 0.0)
        sh *= 2
    x = x3.reshape(C, K)
    x_sub = x

    sc = jnp.zeros((C, C), F32)
    for s, same_group in levels:
        G = C // (2 * s)
        x4 = x.reshape(G, 2, s, K)
        xl, xr = x4[:, 0], x4[:, 1]
        yl = jnp.broadcast_to(xl[:, s - 1:s, :], (G, s, K))
        qr = q.reshape(G, 2, s, K)[:, 1] * jnp.exp(xr)
        kl = k.reshape(G, 2, s, K)[:, 0] * jnp.exp(yl - xl)
        zero = jnp.zeros((G, s, K), F32)
        qf = jnp.stack([zero, qr], axis=1).reshape(C, K).astype(BF16)
        kf = jnp.stack([kl, zero], axis=1).reshape(C, K).astype(BF16)
        sc = sc + jnp.where(same_group, _dot_nt(qf, kf), 0.0)
        x = jnp.stack([xl, xr + yl], axis=1).reshape(C, K)
    b = x

    terms = []
    for jj in range(SUB):
        kb = _bcast_rows(k, SUB, jj)
        xb = _bcast_rows(x_sub, SUB, jj)
        e = jnp.where(rmod >= jj, x_sub - xb, NEG_BIG)
        terms.append((q * kb * jnp.exp(e)).astype(BF16))
    d = _dot(jnp.concatenate(terms, axis=1), sel)
    sc = sc + jnp.where(diag, d, 0.0)

    vb = v.astype(BF16)
    o = _dot(sc.astype(BF16), vb) + _dot_nt((q * jnp.exp(b)).astype(BF16), st.astype(BF16))
    b_last = b[C - 1:C, :]
    kd = (k * jnp.exp(b_last - b)).astype(BF16)
    st_new = st * jnp.exp(b_last) + _dot(v.T.astype(BF16), kd)
    return o, st_new


def _recurrence_tile(q_ref, k_ref, la_ref, v_ref, o_ref, st_ref, sel_ref, n_heads, K, V, T):
    C = CHUNK
    masks = _chunk_masks(C)
    sel = sel_ref[...]

    def body(c, carry):
        r0 = pl.multiple_of(c * C, C)
        for h in range(n_heads):
            ks = slice(h * K, (h + 1) * K)
            vs = slice(h * V, (h + 1) * V)
            o, st_new = _chunk_head(q_ref[pl.ds(r0, C), ks], k_ref[pl.ds(r0, C), ks],
                                    la_ref[pl.ds(r0, C), ks], v_ref[pl.ds(r0, C), vs],
                                    st_ref[h], sel, masks)
            o_ref[pl.ds(r0, C), vs] = o
            st_ref[h] = st_new
        return carry

    lax.fori_loop(0, T // C, body, 0)


def _branch_kernel(*refs, kind, n_heads, K, V, T, layer):
    if kind == "gla":
        (x_ref, sh_ref, sc_ref, wq_ref, wk_ref, wv_ref, wg_ref, wgk1_ref, wgk2_ref, bgk_ref,
         nw_ref, wp_ref, sel_ref, y_ref, sout_ref,
         q_s, k_s, la_s, v_s, g_s, o_s, st_s) = refs
    else:
        (x_ref, sh_ref, sc_ref, wq_ref, wk_ref, wv_ref, wg_ref, lb_ref,
         nw_ref, wp_ref, sel_ref, y_ref, sout_ref,
         q_s, k_s, la_s, v_s, g_s, o_s, st_s) = refs
    lt = pl.program_id(1)

    @pl.when(lt == 0)
    def _():
        st_s[...] = jnp.zeros_like(st_s)

    h = (x_ref[0] * (1.0 + sc_ref[0]) + sh_ref[0]).astype(BF16)
    scale = K ** -0.5
    if kind == "gla":
        q_s[...] = _dot(h, wq_ref[...]) * scale
        k_s[...] = _dot(h, wk_ref[...])
        lr = _dot(h, wgk1_ref[...]).astype(BF16)
        la_s[...] = _log_sigmoid(_dot(lr, wgk2_ref[...]) + bgk_ref[...]) * (1.0 / GLA_GATE_NORMALIZER)
    else:
        q_s[...] = _silu(_dot(h, wq_ref[...])) * scale
        lbp = lb_ref[...]
        e = jnp.exp(lbp - jnp.max(lbp, axis=0, keepdims=True))
        lb = jnp.sum(e[:layer + 1], axis=0, keepdims=True) / jnp.sum(e, axis=0, keepdims=True)
        forget = lb + (1.0 - lb) * jax.nn.sigmoid(_dot(h, wk_ref[...]))
        k_s[...] = 1.0 - forget
        la_s[...] = jnp.log(forget)
    v_s[...] = _dot(h, wv_ref[...])
    g_s[...] = _dot(h, wg_ref[...])

    _recurrence_tile(q_s, k_s, la_s, v_s, o_s, st_s, sel_ref, n_heads, K, V, T)

    nw = nw_ref[...]
    outs = []
    for hd in range(n_heads):
        vs = slice(hd * V, (hd + 1) * V)
        o = o_s[:, vs]
        g = g_s[:, vs]
        gate = _silu(g) if kind == "gla" else jax.nn.sigmoid(g)
        o = o * lax.rsqrt(jnp.mean(o * o, axis=-1, keepdims=True) + EPS) * nw * gate
        outs.append(o.astype(BF16))
    y_ref[0] = _dot(jnp.concatenate(outs, axis=1), wp_ref[...])

    @pl.when(lt == pl.num_programs(1) - 1)
    def _():
        for hd in range(n_heads):
            sout_ref[0, hd] = st_s[hd].T


def _sel_matrix(K, C):
    r = jnp.arange(SUB * K, dtype=jnp.int32)[:, None] // K
    c = jnp.arange(C, dtype=jnp.int32)[None, :] % SUB
    return (r == c).astype(BF16)


def _const_spec(shape):
    nd = len(shape)
    return pl.BlockSpec(shape, lambda b, l: (0,) * nd)


class _Cols:
    def __init__(self, arr, start, width):
        assert start % width == 0
        self.arr, self.width, self.index = arr, width, start // width


def _w_array(w):
    return w.arr if isinstance(w, _Cols) else w


def _w_spec(w):
    if isinstance(w, _Cols):
        idx = w.index
        return pl.BlockSpec((w.arr.shape[0], w.width), lambda *g: (0, idx))
    nd = len(w.shape)
    return pl.BlockSpec(w.shape, lambda *g: (0,) * nd)


def _branch_prompt(kind, x, mod3, weights, norm_w, w_proj, n_heads, K, V, layer):
    B, L, D = x.shape
    T = min(TOKEN_TILE, L)
    HK, HV = n_heads * K, n_heads * V
    sel = _sel_matrix(K, CHUNK)
    x_spec = pl.BlockSpec((1, T, D), lambda b, l: (b, l, 0))
    sh_spec = pl.BlockSpec((1, 1, D), lambda b, l: (b, 0, 0))
    sc_spec = pl.BlockSpec((1, 1, D), lambda b, l: (b, 0, 1))
    w_specs = [_w_spec(w) for w in weights]
    nw2 = norm_w.reshape(1, V)
    in_specs = [x_spec, sh_spec, sc_spec] + w_specs + [_const_spec(nw2.shape), _const_spec(w_proj.shape),
                                                       _const_spec(sel.shape)]
    kern = functools.partial(_branch_kernel, kind=kind, n_heads=n_heads, K=K, V=V, T=T, layer=layer)
    return pl.pallas_call(
        kern,
        grid=(B, L // T),
        in_specs=in_specs,
        out_specs=[pl.BlockSpec((1, T, D), lambda b, l: (b, l, 0)),
                   pl.BlockSpec((1, n_heads, K, V), lambda b, l: (b, 0, 0, 0))],
        out_shape=[jax.ShapeDtypeStruct((B, L, D), F32),
                   jax.ShapeDtypeStruct((B, n_heads, K, V), F32)],
        scratch_shapes=[pltpu.VMEM((T, HK), F32), pltpu.VMEM((T, HK), F32), pltpu.VMEM((T, HK), F32),
                        pltpu.VMEM((T, HV), F32), pltpu.VMEM((T, HV), F32), pltpu.VMEM((T, HV), F32),
                        pltpu.VMEM((n_heads, V, K), F32)],
        compiler_params=_cparams(("arbitrary", "arbitrary")),
        name=f"{kind}_prompt",
    )(x, mod3, mod3, *[_w_array(w) for w in weights], nw2, w_proj, sel)


def _sample_kernel(*refs, kind, n_heads, K, V, TB, layer):
    if kind == "gla":
        (x_ref, sh_ref, sc_ref, wq_ref, wk_ref, wv_ref, wg_ref, wgk1_ref, wgk2_ref, bgk_ref,
         nw_ref, wp_ref, s_ref, y_ref, sout_ref, qT_s, kT_s, aT_s, v_s, g_s, o_s) = refs
    else:
        (x_ref, sh_ref, sc_ref, wq_ref, wk_ref, wv_ref, wg_ref, lb_ref,
         nw_ref, wp_ref, s_ref, y_ref, sout_ref, qT_s, kT_s, aT_s, v_s, g_s, o_s) = refs
    step = pl.program_id(0)
    NT = x_ref.shape[0]

    @pl.when(step == 0)
    def _():
        h = (x_ref[...] * (1.0 + sc_ref[...]) + sh_ref[...]).astype(BF16)
        scale = K ** -0.5
        if kind == "gla":
            q = _dot(h, wq_ref[...]) * scale
            k = _dot(h, wk_ref[...])
            lr = _dot(h, wgk1_ref[...]).astype(BF16)
            a = jnp.exp(_log_sigmoid(_dot(lr, wgk2_ref[...]) + bgk_ref[...]) * (1.0 / GLA_GATE_NORMALIZER))
        else:
            q = _silu(_dot(h, wq_ref[...])) * scale
            lbp = lb_ref[...]
            e = jnp.exp(lbp - jnp.max(lbp, axis=0, keepdims=True))
            lb = jnp.sum(e[:layer + 1], axis=0, keepdims=True) / jnp.sum(e, axis=0, keepdims=True)
            a = lb + (1.0 - lb) * jax.nn.sigmoid(_dot(h, wk_ref[...]))
            k = 1.0 - a
        for hd in range(n_heads):
            ks = slice(hd * K, (hd + 1) * K)
            qT_s[ks, :] = q[:, ks].T
            kT_s[ks, :] = k[:, ks].T
            aT_s[ks, :] = a[:, ks].T
        v_s[...] = _dot(h, wv_ref[...])
        g_s[...] = _dot(h, wg_ref[...])

    lane = lax.broadcasted_iota(jnp.int32, (1, NT), 1)
    sub = lax.broadcasted_iota(jnp.int32, (TB, 1), 0)
    t0 = pl.multiple_of(step * TB, TB)
    for hd in range(n_heads):
        ks = slice(hd * K, (hd + 1) * K)
        vs = slice(hd * V, (hd + 1) * V)
        v_rows = v_s[pl.ds(t0, TB), vs]
        o_rows = jnp.zeros((TB, V), F32)
        for j in range(TB):
            pick = lane == t0 + j
            acol = jnp.sum(jnp.where(pick, aT_s[ks, :], 0.0), axis=1, keepdims=True)
            kcol = jnp.sum(jnp.where(pick, kT_s[ks, :], 0.0), axis=1, keepdims=True)
            qcol = jnp.sum(jnp.where(pick, qT_s[ks, :], 0.0), axis=1, keepdims=True)
            s1 = acol * s_ref[j, hd] + kcol * v_rows[j:j + 1, :]
            sout_ref[j, hd] = s1
            o_rows = jnp.where(sub == j, jnp.sum(qcol * s1, axis=0, keepdims=True), o_rows)
        o_s[pl.ds(t0, TB), vs] = o_rows

    @pl.when(step == pl.num_programs(0) - 1)
    def _():
        nw = nw_ref[...]
        outs = []
        for hd in range(n_heads):
            vs = slice(hd * V, (hd + 1) * V)
            o = o_s[:, vs]
            g = g_s[:, vs]
            gate = _silu(g) if kind == "gla" else jax.nn.sigmoid(g)
            o = o * lax.rsqrt(jnp.mean(o * o, axis=-1, keepdims=True) + EPS) * nw * gate
            outs.append(o.astype(BF16))
        y_ref[...] = _dot(jnp.concatenate(outs, axis=1), wp_ref[...])


def _branch_sample(kind, x, mod, weights, norm_w, w_proj, state, layer):
    NT, D = x.shape
    _, n_heads, K, V = state.shape
    HK, HV = n_heads * K, n_heads * V
    TB = SUBLANES
    c1 = lambda s: pl.BlockSpec(s, lambda i: (0,) * len(s))
    nw2 = norm_w.reshape(1, V)
    in_specs = ([c1((NT, D)), pl.BlockSpec((NT, D), lambda i: (0, 0)), pl.BlockSpec((NT, D), lambda i: (0, 1))]
                + [_w_spec(w) for w in weights] + [c1(nw2.shape), c1(w_proj.shape),
                                                    pl.BlockSpec((TB, n_heads, K, V), lambda i: (i, 0, 0, 0))])
    kern = functools.partial(_sample_kernel, kind=kind, n_heads=n_heads, K=K, V=V, TB=TB, layer=layer)
    return pl.pallas_call(
        kern,
        grid=(NT // TB,),
        in_specs=in_specs,
        out_specs=[c1((NT, D)), pl.BlockSpec((TB, n_heads, K, V), lambda i: (i, 0, 0, 0))],
        out_shape=[jax.ShapeDtypeStruct((NT, D), F32), jax.ShapeDtypeStruct(state.shape, F32)],
        scratch_shapes=[pltpu.VMEM((HK, NT), F32), pltpu.VMEM((HK, NT), F32), pltpu.VMEM((HK, NT), F32),
                        pltpu.VMEM((NT, HV), F32), pltpu.VMEM((NT, HV), F32), pltpu.VMEM((NT, HV), F32)],
        compiler_params=_cparams(("arbitrary",)),
        name=f"{kind}_sample",
    )(x, mod, mod, *[_w_array(w) for w in weights], nw2, w_proj, state)


def _merge_kernel(x_ref, ya_ref, yb_ref, sh_ref, sc_ref, g_ref, wu_ref, wo_ref, lg_ref, lb_ref, o_ref, *, alpha):
    x = x_ref[0]
    D = x.shape[-1]
    h = (x * (1.0 + sc_ref[0]) + sh_ref[0]).astype(BF16)
    u = _dot(h, wu_ref[...])
    merged = jax.nn.sigmoid(u[:, :D]) * ya_ref[0] + jax.nn.sigmoid(u[:, D:]) * yb_ref[0]
    mix = _dot(merged.astype(BF16), wo_ref[...])
    o_ref[0] = _layernorm(alpha * x + g_ref[0] * mix, lg_ref[...], lb_ref[...])


def _mod_specs(mod3, cols, T):
    D = mod3.shape[-1] // 6
    if mod3.shape[1] == 1:
        return [pl.BlockSpec((1, 1, D), functools.partial(lambda b, l, *_, c: (b, 0, c), c=c)) for c in cols]
    return [pl.BlockSpec((1, T, D), functools.partial(lambda b, l, *_, c: (b, l, c), c=c)) for c in cols]


def _merge(x, ya, yb, mod3, wu, w_out, ln_g, ln_b, alpha):
    B, L, D = x.shape
    T = min(TOKEN_TILE, L)
    tok = pl.BlockSpec((1, T, D), lambda b, l: (b, l, 0))
    return pl.pallas_call(
        functools.partial(_merge_kernel, alpha=alpha),
        grid=(B, L // T),
        in_specs=[tok, tok, tok] + _mod_specs(mod3, (0, 1, 2), T)
        + [_w_spec(wu), _const_spec(w_out.shape), _const_spec((1, D)), _const_spec((1, D))],
        out_specs=tok,
        out_shape=jax.ShapeDtypeStruct((B, L, D), F32),
        compiler_params=_cparams(("arbitrary", "arbitrary")),
        name="merge",
    )(x, ya, yb, mod3, mod3, mod3, _w_array(wu), w_out, ln_g.reshape(1, D), ln_b.reshape(1, D))


def _first_argmax(vals, iota, n, axis):
    m = jnp.max(vals, axis=axis, keepdims=True)
    idx = jnp.min(jnp.where(vals == m, iota, n), axis=axis, keepdims=True)
    return m, idx


def _router_kernel(x_ref, sh_ref, sc_ref, wrT_ref, bias_ref, eidx_ref, egate_ref, xg_ref, *, n_experts):
    E = n_experts
    per = E // N_GROUPS
    hf = x_ref[0] * (1.0 + sc_ref[0]) + sh_ref[0]
    h = hf.astype(BF16)
    T = h.shape[0]
    for s in range(hf.shape[1] // LANES):
        xg_ref[pl.ds(s, T, stride=SUBLANES), :] = hf[:, s * LANES:(s + 1) * LANES]
    scores = jax.nn.sigmoid(_dot_nt(wrT_ref[...], h))
    biased = scores + bias_ref[...]
    b3 = biased.reshape(N_GROUPS, per, T)
    i3 = lax.broadcasted_iota(jnp.int32, (N_GROUPS, per, T), 1)
    m1, a1 = _first_argmax(b3, i3, per, 1)
    m2 = jnp.max(jnp.where(i3 == a1, -jnp.inf, b3), axis=1, keepdims=True)
    gscore = (m1 + m2).reshape(N_GROUPS, T)
    gi = lax.broadcasted_iota(jnp.int32, (N_GROUPS, T), 0)
    gsel = jnp.zeros((N_GROUPS, T), jnp.bool_)
    for _ in range(TOPK_GROUPS):
        _, a = _first_argmax(gscore, gi, N_GROUPS, 0)
        hit = gi == a
        gsel = jnp.logical_or(gsel, hit)
        gscore = jnp.where(hit, -jnp.inf, gscore)
    emask = jnp.broadcast_to(gsel.reshape(N_GROUPS, 1, T), (N_GROUPS, per, T)).reshape(E, T)
    cand = jnp.where(emask, biased, -jnp.inf)
    ei = lax.broadcasted_iota(jnp.int32, (E, T), 0)
    picks, weights = [], []
    for _ in range(TOP_K):
        _, a = _first_argmax(cand, ei, E, 0)
        hit = ei == a
        picks.append(a)
        weights.append(jnp.sum(jnp.where(hit, scores, 0.0), axis=0, keepdims=True))
        cand = jnp.where(hit, -jnp.inf, cand)
    w = jnp.concatenate(weights, axis=0)
    egate_ref[...] = w / jnp.sum(w, axis=0, keepdims=True) * ROUTED_SCALE
    eidx_ref[...] = jnp.concatenate(picks, axis=0)


def _router(x1, mod3, wrT, bias):
    B, L, D = x1.shape
    assert D == SUBLANES * LANES
    E = wrT.shape[0]
    T = min(TOKEN_TILE, L)
    nl = L // T
    N = B * L
    tok = pl.BlockSpec((1, T, D), lambda b, l: (b, l, 0))
    pick_spec = pl.BlockSpec((TOP_K, T), lambda b, l: (0, b * nl + l))
    return pl.pallas_call(
        functools.partial(_router_kernel, n_experts=E),
        grid=(B, nl),
        in_specs=[tok] + _mod_specs(mod3, (3, 4), T) + [_const_spec(wrT.shape), _const_spec((E, 1))],
        out_specs=[pick_spec, pick_spec, pl.BlockSpec((T * SUBLANES, LANES), lambda b, l: (b * nl + l, 0))],
        out_shape=[jax.ShapeDtypeStruct((TOP_K, N), jnp.int32), jax.ShapeDtypeStruct((TOP_K, N), F32),
                   jax.ShapeDtypeStruct((N * SUBLANES, LANES), F32)],
        compiler_params=_cparams(("arbitrary", "arbitrary")),
        name="router",
    )(x1, mod3, mod3, wrT, bias.reshape(E, 1))


def _tile_schedule(eidx, egate, NB, E, R):
    Kp, N = eidx.shape
    nb = N // NB
    A = Kp * NB
    S = SUBLANES
    assert A % R == 0
    NW = A // R
    tok = jnp.broadcast_to(jnp.arange(N, dtype=jnp.int32)[None, :], (Kp, N))
    assert nb * E * NB < 2 ** 31
    key = ((tok // NB) * E + eidx) * NB + tok % NB
    skey, sgate = lax.sort((key.reshape(-1), egate.reshape(-1)), num_keys=1)
    w_e = ((skey // NB) % E).reshape(nb, NW, R)
    w_t = ((skey % NB) * S).reshape(nb, NW, R)
    w_g = sgate.reshape(nb, NW, R)
    first = w_e[:, :, 0]
    npair = w_e[:, :, R - 1] - first + 1
    cum = jnp.cumsum(npair, axis=1)
    ntiles = cum[:, -1]
    SL = NW + E + 3
    q = jnp.arange(SL, dtype=jnp.int32)[None, :] - 2
    qc = jnp.clip(q, 0, ntiles[:, None] - 1)
    k_q = jnp.minimum(jnp.sum((cum[:, None, :] <= qc[:, :, None]).astype(jnp.int32), axis=2), NW - 1)
    onehot = (k_q[:, :, None] == jnp.arange(NW, dtype=jnp.int32)[None, None, :]).astype(F32)
    sel = lambda a: jnp.einsum('bsk,bkr->bsr', onehot, a.astype(F32), precision=lax.Precision.HIGHEST)
    selk = lambda a: jnp.sum(onehot * a.astype(F32)[:, None, :], axis=2).astype(jnp.int32)
    e_q = jnp.clip(selk(first) + qc - selk(cum - npair), 0, E - 1)
    real = jnp.logical_and(q >= 0, q < ntiles[:, None])
    match = jnp.logical_and(real[:, :, None], sel(w_e).astype(jnp.int32) == e_q[:, :, None])
    rows = jnp.where(match, sel(w_t).astype(jnp.int32), NB * S).reshape(nb * SL, 1, R)
    gate = jnp.where(match, sel(w_g), 0.0).reshape(nb * SL, 1, R)
    return e_q.reshape(-1), ntiles, rows, gate, SL


def _moe_step(src_ref, dst_ref, gate_ref, xg_s, acc_s, wg_ref, wu_ref, wd_ref, gbuf, cbuf, cy, sy, R):
    S = SUBLANES
    for r in range(R):
        t0 = pl.multiple_of(src_ref[0, 0, r], S)
        gbuf[r * S:(r + 1) * S, :] = xg_s[pl.ds(t0, S), :]

    x = jnp.concatenate([cbuf[pl.ds(s, R, stride=S), :] for s in range(S)], axis=1).astype(BF16)
    a = _silu(_dot(x, wg_ref[0])) * _dot(x, wu_ref[0])
    y = _dot(a.astype(BF16), wd_ref[0])
    for s in range(S):
        cy[pl.ds(s, R, stride=S), :] = y[:, s * LANES:(s + 1) * LANES]

    for r0 in range(0, R, RMW_BATCH):
        upd = []
        for r in range(r0, r0 + RMW_BATCH):
            a0 = pl.multiple_of(dst_ref[0, 0, r], S)
            upd.append((a0, acc_s[pl.ds(a0, S), :] + gate_ref[0, 0, r] * sy[r * S:(r + 1) * S, :]))
        for a0, val in upd:
            acc_s[pl.ds(a0, S), :] = val


def _moe_kernel(te_ref, nt_ref, src_ref, dst_ref, gate_ref, xg_hbm, wg_ref, wu_ref, wd_ref, out_hbm,
                xg_s, acc_s, buf0, buf1, y0, y1, *, NB, R):
    del te_ref
    b = pl.program_id(0)
    q = pl.program_id(1)

    @pl.when(jnp.logical_and(b == 0, q == 0))
    def _():
        for ref in (buf0, buf1, y0, y1):
            ref[...] = jnp.zeros_like(ref)

    @pl.when(q == 0)
    def _():
        pltpu.sync_copy(xg_hbm.at[b], xg_s.at[pl.ds(0, NB * SUBLANES)])
        xg_s[pl.ds(NB * SUBLANES, SUBLANES), :] = jnp.zeros((SUBLANES, LANES), F32)
        acc_s[...] = jnp.zeros_like(acc_s)

    active = q < nt_ref[b] + 2
    args = (src_ref, dst_ref, gate_ref, xg_s, acc_s, wg_ref, wu_ref, wd_ref)

    @pl.when(jnp.logical_and(active, q % 2 == 0))
    def _():
        _moe_step(*args, buf0, buf1, y1, y0, R)

    @pl.when(jnp.logical_and(active, q % 2 == 1))
    def _():
        _moe_step(*args, buf1, buf0, y0, y1, R)

    @pl.when(q == pl.num_programs(1) - 1)
    def _():
        pltpu.sync_copy(acc_s.at[pl.ds(0, NB * SUBLANES)], out_hbm.at[b])


def _moe_routed(xg, eidx, egate, wg, wu, wd):
    N = eidx.shape[1]
    E, D, DE = wg.shape
    NB = min(MOE_BLOCK, N)
    nb = N // NB
    S = SUBLANES
    R = MOE_ROWS if NB >= MOE_BLOCK else MOE_ROWS_SMALL
    te, ntiles, rows, gate, SL = _tile_schedule(eidx, egate, NB, E, R)
    smem = lambda shift: pl.BlockSpec((1, 1, R), lambda b, q, *_: (b * SL + q + shift, 0, 0),
                                      memory_space=pltpu.SMEM)
    w_map = lambda b, q, te_ref, nt_ref: (te_ref[b * SL + q + 1], 0, 0)
    tile_rows = pltpu.VMEM((R * S, LANES), F32)
    grid_spec = pltpu.PrefetchScalarGridSpec(
        num_scalar_prefetch=2,
        grid=(nb, SL - 2),
        in_specs=[smem(2), smem(0), smem(0),
                  pl.BlockSpec(memory_space=pl.ANY),
                  pl.BlockSpec((1, D, DE), w_map), pl.BlockSpec((1, D, DE), w_map), pl.BlockSpec((1, DE, D), w_map)],
        out_specs=pl.BlockSpec(memory_space=pl.ANY),
        scratch_shapes=[pltpu.VMEM(((NB + 1) * S, LANES), F32), pltpu.VMEM(((NB + 1) * S, LANES), F32),
                        tile_rows, tile_rows, tile_rows, tile_rows],
    )
    return pl.pallas_call(
        functools.partial(_moe_kernel, NB=NB, R=R),
        grid_spec=grid_spec,
        out_shape=jax.ShapeDtypeStruct((nb, NB * S, LANES), F32),
        compiler_params=_cparams(("arbitrary", "arbitrary")),
        name="moe_routed",
    )(te, ntiles, rows, rows, gate, xg.reshape(nb, NB * S, LANES), wg, wu, wd)


def _combine_kernel(x_ref, sh_ref, sc_ref, g2_ref, r_ref, sg_ref, su_ref, sd_ref, lg_ref, lb_ref, o_ref, *, alpha):
    x = x_ref[0]
    T = x.shape[0]
    h = (x * (1.0 + sc_ref[0]) + sh_ref[0]).astype(BF16)
    a = _silu(_dot(h, sg_ref[...])) * _dot(h, su_ref[...])
    shared = _dot(a.astype(BF16), sd_ref[...])
    routed = jnp.concatenate([r_ref[0, pl.ds(s, T, stride=SUBLANES), :] for s in range(SUBLANES)], axis=1)
    o_ref[0] = _layernorm(alpha * x + g2_ref[0] * (routed + shared), lg_ref[...], lb_ref[...])


def _combine(x1, mod3, routed, sg, su, sd, ln_g, ln_b, alpha):
    B, L, D = x1.shape
    T = min(TOKEN_TILE, L)
    nl = L // T
    NB = routed.shape[1] // SUBLANES
    per = NB // T
    tok = pl.BlockSpec((1, T, D), lambda b, l: (b, l, 0))
    r_spec = pl.BlockSpec((1, T * SUBLANES, LANES), lambda b, l: ((b * nl + l) // per, (b * nl + l) % per, 0))
    return pl.pallas_call(
        functools.partial(_combine_kernel, alpha=alpha),
        grid=(B, nl),
        in_specs=[tok] + _mod_specs(mod3, (3, 4, 5), T) + [r_spec]
        + [_const_spec(sg.shape), _const_spec(su.shape), _const_spec(sd.shape), _const_spec((1, D)),
           _const_spec((1, D))],
        out_specs=tok,
        out_shape=jax.ShapeDtypeStruct((B, L, D), F32),
        compiler_params=_cparams(("arbitrary", "arbitrary")),
        name="combine",
    )(x1, mod3, mod3, mod3, routed, sg, su, sd, ln_g.reshape(1, D), ln_b.reshape(1, D))


def _split_w_in(w_in_l, gla_shape, hgrn_shape, D):
    Hg, Kg, Vg = gla_shape
    Hh, Kh, Vh = hgrn_shape
    rank = w_in_l.shape[1] - (2 * Hg * Kg + 2 * Hg * Vg + 2 * Hh * Kh + 2 * Hh * Vh + 2 * D)
    gla_w = (Hg * Kg, Hg * Kg, Hg * Vg, Hg * Vg)
    hgrn_w = (Hh * Kh, Hh * Kh, Hh * Vh, Hh * Vh, 2 * D)
    n_lo = sum(gla_w)
    lo = w_in_l[:, :n_lo].astype(BF16)
    mid = w_in_l[:, n_lo:n_lo + rank].astype(BF16)
    hi = w_in_l[:, n_lo + rank:].astype(BF16)
    out = []
    for arr, widths in ((lo, gla_w), (hi, hgrn_w)):
        start = 0
        for w in widths:
            out.append(_Cols(arr, start, w))
            start += w
    return out[:4] + [mid] + out[4:]


def kernel(x_prompt, x_sample, state_gla, state_hgrn, c_prompt, c_sample, w_ada, b_ada, w_in, w_gk2, b_gk,
           hgrn_lb, gla_norm_w, hgrn_norm_w, w_proj_a, w_proj_b, w_out, ln1_g, ln1_b, w_router, router_bias,
           w_exp_gate, w_exp_up, w_exp_down, w_sh_gate, w_sh_up, w_sh_down, ln2_g, ln2_b):
    depth = w_in.shape[0]
    BP, L, D = x_prompt.shape
    NS = x_sample.shape[0]
    assert x_sample.shape[1] == 1
    gla_shape = state_gla.shape[2:]
    hgrn_shape = state_hgrn.shape[2:]
    alpha = (2.0 * depth) ** 0.25

    xp = x_prompt
    xs = x_sample.reshape(NS, D)
    c_all = jnp.concatenate([c_prompt, c_sample], axis=0)
    new_gla_p, new_hgrn_p, new_gla_s, new_hgrn_s = [], [], [], []
    for l in range(depth):
        mod = _ada_mod(c_all, w_ada[l], b_ada[l])
        mod_p = mod[:BP].reshape(BP, 1, 6 * D)
        mod_s = mod[BP:]
        (wqa, wka, wva, wga, wgk1, wqb, wfb, wib, wgb, wuab) = _split_w_in(w_in[l], gla_shape, hgrn_shape, D)
        gla_w = [wqa, wka, wva, wga, wgk1, w_gk2[l].astype(BF16), b_gk[l].reshape(1, -1)]
        hgrn_w = [wqb, wfb, wib, wgb, hgrn_lb]
        wpa = w_proj_a[l].astype(BF16)
        wpb = w_proj_b[l].astype(BF16)
        wo = w_out[l].astype(BF16)
        wrT = w_router[l].T.astype(BF16)
        eg, eu, ed = w_exp_gate[l].astype(BF16), w_exp_up[l].astype(BF16), w_exp_down[l].astype(BF16)
        sg, su, sd = w_sh_gate[l].astype(BF16), w_sh_up[l].astype(BF16), w_sh_down[l].astype(BF16)

        def tail(x3, ya, yb, mod3):
            x1 = _merge(x3, ya, yb, mod3, wuab, wo, ln1_g[l], ln1_b[l], alpha)
            eidx, egate, xg = _router(x1, mod3, wrT, router_bias[l])
            routed = _moe_routed(xg, eidx, egate, eg, eu, ed)
            return _combine(x1, mod3, routed, sg, su, sd, ln2_g[l], ln2_b[l], alpha)

        ya, sg_p = _branch_prompt("gla", xp, mod_p, gla_w, gla_norm_w[l], wpa, *gla_shape, layer=l)
        yb, sh_p = _branch_prompt("hgrn", xp, mod_p, hgrn_w, hgrn_norm_w[l], wpb, *hgrn_shape, layer=l)
        xp = tail(xp, ya, yb, mod_p)
        new_gla_p.append(sg_p)
        new_hgrn_p.append(sh_p)

        ya, sg_s = _branch_sample("gla", xs, mod_s, gla_w, gla_norm_w[l], wpa, state_gla[l], layer=l)
        yb, sh_s = _branch_sample("hgrn", xs, mod_s, hgrn_w, hgrn_norm_w[l], wpb, state_hgrn[l], layer=l)
        xs = tail(xs[None], ya[None], yb[None], mod_s[None])[0]
        new_gla_s.append(sg_s)
        new_hgrn_s.append(sh_s)

    return (xp, xs.reshape(NS, 1, D), jnp.stack(new_gla_p), jnp.stack(new_hgrn_p),
            jnp.stack(new_gla_s), jnp.stack(new_hgrn_s))
```

```python
import functools

import jax
import jax.numpy as jnp
from jax import lax
from jax.experimental import pallas as pl
from jax.experimental.pallas import tpu as pltpu

F32 = jnp.float32
BF16 = jnp.bfloat16

GLA_GATE_NORMALIZER = 16.0
N_GROUPS = 8
TOPK_GROUPS = 4
TOP_K = 8
ROUTED_SCALE = 2.5
EPS = 1e-5

SUBLANES = 8
LANES = 128
VMEM_LIMIT_BYTES = 56 * 1024 * 1024

TOKEN_TILE = 512
MOE_BLOCK = 4096
MOE_ROWS = 256
MOE_ROWS_SMALL = 32
RMW_BATCH = 8
CHUNK = 128
SUB = SUBLANES
NEG_BIG = -1e30


def _cparams(sem):
    return pltpu.CompilerParams(dimension_semantics=sem, vmem_limit_bytes=VMEM_LIMIT_BYTES)


def _dot(a, b):
    return jnp.dot(a, b, preferred_element_type=F32)


def _dot_nt(a, b):
    return lax.dot_general(a, b, (((1,), (1,)), ((), ())), preferred_element_type=F32)


def _silu(x):
    return x * jax.nn.sigmoid(x)


def _log_sigmoid(x):
    return jnp.minimum(x, 0.0) - jnp.log1p(jnp.exp(-jnp.abs(x)))


def _layernorm(r, g, b):
    mu = jnp.mean(r, axis=-1, keepdims=True)
    d = r - mu
    var = jnp.mean(d * d, axis=-1, keepdims=True)
    return d * lax.rsqrt(var + EPS) * g + b


def _ada_kernel(c_ref, w_ref, b_ref, o_ref):
    c = c_ref[...]
    o_ref[...] = _dot(_silu(c).astype(BF16), w_ref[...].astype(BF16)) + b_ref[...]


def _ada_mod(c, w_ada, b_ada):
    R, D = c.shape
    N = w_ada.shape[1]
    tn = D
    return pl.pallas_call(
        _ada_kernel,
        grid=(N // tn,),
        in_specs=[pl.BlockSpec((R, D), lambda j: (0, 0)),
                  pl.BlockSpec((D, tn), lambda j: (0, j)),
                  pl.BlockSpec((1, tn), lambda j: (0, j))],
        out_specs=pl.BlockSpec((R, tn), lambda j: (0, j)),
        out_shape=jax.ShapeDtypeStruct((R, N), F32),
        compiler_params=_cparams(("arbitrary",)),
        name="ada_mod",
    )(c, w_ada, b_ada.reshape(1, N))


def _chunk_masks(C):
    row = lax.broadcasted_iota(jnp.int32, (C, 1), 0)
    ri = lax.broadcasted_iota(jnp.int32, (C, C), 0)
    ci = lax.broadcasted_iota(jnp.int32, (C, C), 1)
    levels = []
    s = SUB
    while s < C:
        same_group = (ri // (2 * s)) == (ci // (2 * s))
        levels.append((s, same_group))
        s *= 2
    diag = (ri // SUB) == (ci // SUB)
    return row, levels, diag


def _bcast_rows(x, group, idx):
    C, K = x.shape
    G = C // group
    x3 = x.reshape(G, group, K)
    return jnp.broadcast_to(x3[:, idx:idx + 1, :], (G, group, K)).reshape(C, K)


def _chunk_head(q, k, la, v, st, sel, masks):
    C, K = q.shape
    row, levels, diag = masks
    rmod = row % SUB

    x3 = la.reshape(C // SUB, SUB, K)
    sub3 = lax.broadcasted_iota(jnp.int32, (1, SUB, 1), 1)
    sh = 1
    while sh < SUB:
        x3 = x3 + jnp.where(sub3 >= sh, pltpu.roll(x3, sh, 1), 0.0)
        sh *= 2
    x = x3.reshape(C, K)
    x_sub = x

    sc = jnp.zeros((C, C), F32)
    for s, same_group in levels:
        G = C // (2 * s)
        x4 = x.reshape(G, 2, s, K)
        xl, xr = x4[:, 0], x4[:, 1]
        yl = jnp.broadcast_to(xl[:, s - 1:s, :], (G, s, K))
        qr = q.reshape(G, 2, s, K)[:, 1] * jnp.exp(xr)
        kl = k.reshape(G, 2, s, K)[:, 0] * jnp.exp(yl - xl)
        zero = jnp.zeros((G, s, K), F32)
        qf = jnp.stack([zero, qr], axis=1).reshape(C, K).astype(BF16)
        kf = jnp.stack([kl, zero], axis=1).reshape(C, K).astype(BF16)
        sc = sc + jnp.where(same_group, _dot_nt(qf, kf), 0.0)
        x = jnp.stack([xl, xr + yl], axis=1).reshape(C, K)
    b = x

    terms = []
    for jj in range(SUB):
        kb = _bcast_rows(k, SUB, jj)
        xb = _bcast_rows(x_sub, SUB, jj)
        e = jnp.where(rmod >= jj, x_sub - xb, NEG_BIG)
        terms.append((q * kb * jnp.exp(e)).astype(BF16))
    d = _dot(jnp.concatenate(terms, axis=1), sel)
    sc = sc + jnp.where(diag, d, 0.0)

    vb = v.astype(BF16)
    o = _dot(sc.astype(BF16), vb) + _dot_nt((q * jnp.exp(b)).astype(BF16), st.astype(BF16))
    b_last = b[C - 1:C, :]
    kd = (k * jnp.exp(b_last - b)).astype(BF16)
    st_new = st * jnp.exp(b_last) + _dot(v.T.astype(BF16), kd)
    return o, st_new


def _recurrence_tile(q_ref, k_ref, la_ref, v_ref, o_ref, st_ref, sel_ref, n_heads, K, V, T):
    C = CHUNK
    masks = _chunk_masks(C)
    sel = sel_ref[...]

    def body(c, carry):
        r0 = pl.multiple_of(c * C, C)
        for h in range(n_heads):
            ks = slice(h * K, (h + 1) * K)
            vs = slice(h * V, (h + 1) * V)
            o, st_new = _chunk_head(q_ref[pl.ds(r0, C), ks], k_ref[pl.ds(r0, C), ks],
                                    la_ref[pl.ds(r0, C), ks], v_ref[pl.ds(r0, C), vs],
                                    st_ref[h], sel, masks)
            o_ref[pl.ds(r0, C), vs] = o
            st_ref[h] = st_new
        return carry

    lax.fori_loop(0, T // C, body, 0)


def _branch_kernel(*refs, kind, n_heads, K, V, T, layer):
    if kind == "gla":
        (x_ref, sh_ref, sc_ref, wq_ref, wk_ref, wv_ref, wg_ref, wgk1_ref, wgk2_ref, bgk_ref,
         nw_ref, wp_ref, sel_ref, y_ref, sout_ref,
         q_s, k_s, la_s, v_s, g_s, o_s, st_s) = refs
    else:
        (x_ref, sh_ref, sc_ref, wq_ref, wk_ref, wv_ref, wg_ref, lb_ref,
         nw_ref, wp_ref, sel_ref, y_ref, sout_ref,
         q_s, k_s, la_s, v_s, g_s, o_s, st_s) = refs
    lt = pl.program_id(1)

    @pl.when(lt == 0)
    def _():
        st_s[...] = jnp.zeros_like(st_s)

    h = (x_ref[0] * (1.0 + sc_ref[0]) + sh_ref[0]).astype(BF16)
    scale = K ** -0.5
    if kind == "gla":
        q_s[...] = _dot(h, wq_ref[...]) * scale
        k_s[...] = _dot(h, wk_ref[...])
        lr = _dot(h, wgk1_ref[...]).astype(BF16)
        la_s[...] = _log_sigmoid(_dot(lr, wgk2_ref[...]) + bgk_ref[...]) * (1.0 / GLA_GATE_NORMALIZER)
    else:
        q_s[...] = _silu(_dot(h, wq_ref[...])) * scale
        lbp = lb_ref[...]
        e = jnp.exp(lbp - jnp.max(lbp, axis=0, keepdims=True))
        lb = jnp.sum(e[:layer + 1], axis=0, keepdims=True) / jnp.sum(e, axis=0, keepdims=True)
        forget = lb + (1.0 - lb) * jax.nn.sigmoid(_dot(h, wk_ref[...]))
        k_s[...] = 1.0 - forget
        la_s[...] = jnp.log(forget)
    v_s[...] = _dot(h, wv_ref[...])
    g_s[...] = _dot(h, wg_ref[...])

    _recurrence_tile(q_s, k_s, la_s, v_s, o_s, st_s, sel_ref, n_heads, K, V, T)

    nw = nw_ref[...]
    outs = []
    for hd in range(n_heads):
        vs = slice(hd * V, (hd + 1) * V)
        o = o_s[:, vs]
        g = g_s[:, vs]
        gate = _silu(g) if kind == "gla" else jax.nn.sigmoid(g)
        o = o * lax.rsqrt(jnp.mean(o * o, axis=-1, keepdims=True) + EPS) * nw * gate
        outs.append(o.astype(BF16))
    y_ref[0] = _dot(jnp.concatenate(outs, axis=1), wp_ref[...])

    @pl.when(lt == pl.num_programs(1) - 1)
    def _():
        for hd in range(n_heads):
            sout_ref[0, hd] = st_s[hd].T


def _sel_matrix(K, C):
    r = jnp.arange(SUB * K, dtype=jnp.int32)[:, None] // K
    c = jnp.arange(C, dtype=jnp.int32)[None, :] % SUB
    return (r == c).astype(BF16)


def _const_spec(shape):
    nd = len(shape)
    return pl.BlockSpec(shape, lambda b, l: (0,) * nd)


class _Cols:
    def __init__(self, arr, start, width):
        assert start % width == 0
        self.arr, self.width, self.index = arr, width, start // width


def _w_array(w):
    return w.arr if isinstance(w, _Cols) else w


def _w_spec(w):
    if isinstance(w, _Cols):
        idx = w.index
        return pl.BlockSpec((w.arr.shape[0], w.width), lambda *g: (0, idx))
    nd = len(w.shape)
    return pl.BlockSpec(w.shape, lambda *g: (0,) * nd)


def _branch_prompt(kind, x, mod3, weights, norm_w, w_proj, n_heads, K, V, layer):
    B, L, D = x.shape
    T = min(TOKEN_TILE, L)
    HK, HV = n_heads * K, n_heads * V
    sel = _sel_matrix(K, CHUNK)
    x_spec = pl.BlockSpec((1, T, D), lambda b, l: (b, l, 0))
    sh_spec = pl.BlockSpec((1, 1, D), lambda b, l: (b, 0, 0))
    sc_spec = pl.BlockSpec((1, 1, D), lambda b, l: (b, 0, 1))
    w_specs = [_w_spec(w) for w in weights]
    nw2 = norm_w.reshape(1, V)
    in_specs = [x_spec, sh_spec, sc_spec] + w_specs + [_const_spec(nw2.shape), _const_spec(w_proj.shape),
                                                       _const_spec(sel.shape)]
    kern = functools.partial(_branch_kernel, kind=kind, n_heads=n_heads, K=K, V=V, T=T, layer=layer)
    return pl.pallas_call(
        kern,
        grid=(B, L // T),
        in_specs=in_specs,
        out_specs=[pl.BlockSpec((1, T, D), lambda b, l: (b, l, 0)),
                   pl.BlockSpec((1, n_heads, K, V), lambda b, l: (b, 0, 0, 0))],
        out_shape=[jax.ShapeDtypeStruct((B, L, D), F32),
                   jax.ShapeDtypeStruct((B, n_heads, K, V), F32)],
        scratch_shapes=[pltpu.VMEM((T, HK), F32), pltpu.VMEM((T, HK), F32), pltpu.VMEM((T, HK), F32),
                        pltpu.VMEM((T, HV), F32), pltpu.VMEM((T, HV), F32), pltpu.VMEM((T, HV), F32),
                        pltpu.VMEM((n_heads, V, K), F32)],
        compiler_params=_cparams(("arbitrary", "arbitrary")),
        name=f"{kind}_prompt",
    )(x, mod3, mod3, *[_w_array(w) for w in weights], nw2, w_proj, sel)


def _sample_kernel(*refs, kind, n_heads, K, V, TB, layer):
    if kind == "gla":
        (x_ref, sh_ref, sc_ref, wq_ref, wk_ref, wv_ref, wg_ref, wgk1_ref, wgk2_ref, bgk_ref,
         nw_ref, wp_ref, s_ref, y_ref, sout_ref, qT_s, kT_s, aT_s, v_s, g_s, o_s) = refs
    else:
        (x_ref, sh_ref, sc_ref, wq_ref, wk_ref, wv_ref, wg_ref, lb_ref,
         nw_ref, wp_ref, s_ref, y_ref, sout_ref, qT_s, kT_s, aT_s, v_s, g_s, o_s) = refs
    step = pl.program_id(0)
    NT = x_ref.shape[0]

    @pl.when(step == 0)
    def _():
        h = (x_ref[...] * (1.0 + sc_ref[...]) + sh_ref[...]).astype(BF16)
        scale = K ** -0.5
        if kind == "gla":
            q = _dot(h, wq_ref[...]) * scale
            k = _dot(h, wk_ref[...])
            lr = _dot(h, wgk1_ref[...]).astype(BF16)
            a = jnp.exp(_log_sigmoid(_dot(lr, wgk2_ref[...]) + bgk_ref[...]) * (1.0 / GLA_GATE_NORMALIZER))
        else:
            q = _silu(_dot(h, wq_ref[...])) * scale
            lbp = lb_ref[...]
            e = jnp.exp(lbp - jnp.max(lbp, axis=0, keepdims=True))
            lb = jnp.sum(e[:layer + 1], axis=0, keepdims=True) / jnp.sum(e, axis=0, keepdims=True)
            a = lb + (1.0 - lb) * jax.nn.sigmoid(_dot(h, wk_ref[...]))
            k = 1.0 - a
        for hd in range(n_heads):
            ks = slice(hd * K, (hd + 1) * K)
            qT_s[ks, :] = q[:, ks].T
            kT_s[ks, :] = k[:, ks].T
            aT_s[ks, :] = a[:, ks].T
        v_s[...] = _dot(h, wv_ref[...])
        g_s[...] = _dot(h, wg_ref[...])

    lane = lax.broadcasted_iota(jnp.int32, (1, NT), 1)
    sub = lax.broadcasted_iota(jnp.int32, (TB, 1), 0)
    t0 = pl.multiple_of(step * TB, TB)
    for hd in range(n_heads):
        ks = slice(hd * K, (hd + 1) * K)
        vs = slice(hd * V, (hd + 1) * V)
        v_rows = v_s[pl.ds(t0, TB), vs]
        o_rows = jnp.zeros((TB, V), F32)
        for j in range(TB):
            pick = lane == t0 + j
            acol = jnp.sum(jnp.where(pick, aT_s[ks, :], 0.0), axis=1, keepdims=True)
            kcol = jnp.sum(jnp.where(pick, kT_s[ks, :], 0.0), axis=1, keepdims=True)
            qcol = jnp.sum(jnp.where(pick, qT_s[ks, :], 0.0), axis=1, keepdims=True)
            s1 = acol * s_ref[j, hd] + kcol * v_rows[j:j + 1, :]
            sout_ref[j, hd] = s1
            o_rows = jnp.where(sub == j, jnp.sum(qcol * s1, axis=0, keepdims=True), o_rows)
        o_s[pl.ds(t0, TB), vs] = o_rows

    @pl.when(step == pl.num_programs(0) - 1)
    def _():
        nw = nw_ref[...]
        outs = []
        for hd in range(n_heads):
            vs = slice(hd * V, (hd + 1) * V)
            o = o_s[:, vs]
            g = g_s[:, vs]
            gate = _silu(g) if kind == "gla" else jax.nn.sigmoid(g)
            o = o * lax.rsqrt(jnp.mean(o * o, axis=-1, keepdims=True) + EPS) * nw * gate
            outs.append(o.astype(BF16))
        y_ref[...] = _dot(jnp.concatenate(outs, axis=1), wp_ref[...])


def _branch_sample(kind, x, mod, weights, norm_w, w_proj, state, layer):
    NT, D = x.shape
    _, n_heads, K, V = state.shape
    HK, HV = n_heads * K, n_heads * V
    TB = SUBLANES
    c1 = lambda s: pl.BlockSpec(s, lambda i: (0,) * len(s))
    nw2 = norm_w.reshape(1, V)
    in_specs = ([c1((NT, D)), pl.BlockSpec((NT, D), lambda i: (0, 0)), pl.BlockSpec((NT, D), lambda i: (0, 1))]
                + [_w_spec(w) for w in weights] + [c1(nw2.shape), c1(w_proj.shape),
                                                    pl.BlockSpec((TB, n_heads, K, V), lambda i: (i, 0, 0, 0))])
    kern = functools.partial(_sample_kernel, kind=kind, n_heads=n_heads, K=K, V=V, TB=TB, layer=layer)
    return pl.pallas_call(
        kern,
        grid=(NT // TB,),
        in_specs=in_specs,
        out_specs=[c1((NT, D)), pl.BlockSpec((TB, n_heads, K, V), lambda i: (i, 0, 0, 0))],
        out_shape=[jax.ShapeDtypeStruct((NT, D), F32), jax.ShapeDtypeStruct(state.shape, F32)],
        scratch_shapes=[pltpu.VMEM((HK, NT), F32), pltpu.VMEM((HK, NT), F32), pltpu.VMEM((HK, NT), F32),
                        pltpu.VMEM((NT, HV), F32), pltpu.VMEM((NT, HV), F32), pltpu.VMEM((NT, HV), F32)],
        compiler_params=_cparams(("arbitrary",)),
        name=f"{kind}_sample",
    )(x, mod, mod, *[_w_array(w) for w in weights], nw2, w_proj, state)


def _merge_kernel(x_ref, ya_ref, yb_ref, sh_ref, sc_ref, g_ref, wu_ref, wo_ref, lg_ref, lb_ref, o_ref, *, alpha):
    x = x_ref[0]
    D = x.shape[-1]
    h = (x * (1.0 + sc_ref[0]) + sh_ref[0]).astype(BF16)
    u = _dot(h, wu_ref[...])
    merged = jax.nn.sigmoid(u[:, :D]) * ya_ref[0] + jax.nn.sigmoid(u[:, D:]) * yb_ref[0]
    mix = _dot(merged.astype(BF16), wo_ref[...])
    o_ref[0] = _layernorm(alpha * x + g_ref[0] * mix, lg_ref[...], lb_ref[...])


def _mod_specs(mod3, cols, T):
    D = mod3.shape[-1] // 6
    if mod3.shape[1] == 1:
        return [pl.BlockSpec((1, 1, D), functools.partial(lambda b, l, *_, c: (b, 0, c), c=c)) for c in cols]
    return [pl.BlockSpec((1, T, D), functools.partial(lambda b, l, *_, c: (b, l, c), c=c)) for c in cols]


def _merge(x, ya, yb, mod3, wu, w_out, ln_g, ln_b, alpha):
    B, L, D = x.shape
    T = min(TOKEN_TILE, L)
    tok = pl.BlockSpec((1, T, D), lambda b, l: (b, l, 0))
    return pl.pallas_call(
        functools.partial(_merge_kernel, alpha=alpha),
        grid=(B, L // T),
        in_specs=[tok, tok, tok] + _mod_specs(mod3, (0, 1, 2), T)
        + [_w_spec(wu), _const_spec(w_out.shape), _const_spec((1, D)), _const_spec((1, D))],
        out_specs=tok,
        out_shape=jax.ShapeDtypeStruct((B, L, D), F32),
        compiler_params=_cparams(("arbitrary", "arbitrary")),
        name="merge",
    )(x, ya, yb, mod3, mod3, mod3, _w_array(wu), w_out, ln_g.reshape(1, D), ln_b.reshape(1, D))


def _first_argmax(vals, iota, n, axis):
    m = jnp.max(vals, axis=axis, keepdims=True)
    idx = jnp.min(jnp.where(vals == m, iota, n), axis=axis, keepdims=True)
    return m, idx


def _router_kernel(x_ref, sh_ref, sc_ref, wrT_ref, bias_ref, eidx_ref, egate_ref, xg_ref, *, n_experts):
    E = n_experts
    per = E // N_GROUPS
    hf = x_ref[0] * (1.0 + sc_ref[0]) + sh_ref[0]
    h = hf.astype(BF16)
    T = h.shape[0]
    for s in range(hf.shape[1] // LANES):
        xg_ref[pl.ds(s, T, stride=SUBLANES), :] = hf[:, s * LANES:(s + 1) * LANES]
    scores = jax.nn.sigmoid(_dot_nt(wrT_ref[...], h))
    biased = scores + bias_ref[...]
    b3 = biased.reshape(N_GROUPS, per, T)
    i3 = lax.broadcasted_iota(jnp.int32, (N_GROUPS, per, T), 1)
    m1, a1 = _first_argmax(b3, i3, per, 1)
    m2 = jnp.max(jnp.where(i3 == a1, -jnp.inf, b3), axis=1, keepdims=True)
    gscore = (m1 + m2).reshape(N_GROUPS, T)
    gi = lax.broadcasted_iota(jnp.int32, (N_GROUPS, T), 0)
    gsel = jnp.zeros((N_GROUPS, T), jnp.bool_)
    for _ in range(TOPK_GROUPS):
        _, a = _first_argmax(gscore, gi, N_GROUPS, 0)
        hit = gi == a
        gsel = jnp.logical_or(gsel, hit)
        gscore = jnp.where(hit, -jnp.inf, gscore)
    emask = jnp.broadcast_to(gsel.reshape(N_GROUPS, 1, T), (N_GROUPS, per, T)).reshape(E, T)
    cand = jnp.where(emask, biased, -jnp.inf)
    ei = lax.broadcasted_iota(jnp.int32, (E, T), 0)
    picks, weights = [], []
    for _ in range(TOP_K):
        _, a = _first_argmax(cand, ei, E, 0)
        hit = ei == a
        picks.append(a)
        weights.append(jnp.sum(jnp.where(hit, scores, 0.0), axis=0, keepdims=True))
        cand = jnp.where(hit, -jnp.inf, cand)
    w = jnp.concatenate(weights, axis=0)
    egate_ref[...] = w / jnp.sum(w, axis=0, keepdims=True) * ROUTED_SCALE
    eidx_ref[...] = jnp.concatenate(picks, axis=0)


def _router(x1, mod3, wrT, bias):
    B, L, D = x1.shape
    assert D == SUBLANES * LANES
    E = wrT.shape[0]
    T = min(TOKEN_TILE, L)
    nl = L // T
    N = B * L
    tok = pl.BlockSpec((1, T, D), lambda b, l: (b, l, 0))
    pick_spec = pl.BlockSpec((TOP_K, T), lambda b, l: (0, b * nl + l))
    return pl.pallas_call(
        functools.partial(_router_kernel, n_experts=E),
        grid=(B, nl),
        in_specs=[tok] + _mod_specs(mod3, (3, 4), T) + [_const_spec(wrT.shape), _const_spec((E, 1))],
        out_specs=[pick_spec, pick_spec, pl.BlockSpec((T * SUBLANES, LANES), lambda b, l: (b * nl + l, 0))],
        out_shape=[jax.ShapeDtypeStruct((TOP_K, N), jnp.int32), jax.ShapeDtypeStruct((TOP_K, N), F32),
                   jax.ShapeDtypeStruct((N * SUBLANES, LANES), F32)],
        compiler_params=_cparams(("arbitrary", "arbitrary")),
        name="router",
    )(x1, mod3, mod3, wrT, bias.reshape(E, 1))


def _tile_schedule(eidx, egate, NB, E, R):
    Kp, N = eidx.shape
    nb = N // NB
    A = Kp * NB
    S = SUBLANES
    assert A % R == 0
    NW = A // R
    tok = jnp.broadcast_to(jnp.arange(N, dtype=jnp.int32)[None, :], (Kp, N))
    assert nb * E * NB < 2 ** 31
    key = ((tok // NB) * E + eidx) * NB + tok % NB
    skey, sgate = lax.sort((key.reshape(-1), egate.reshape(-1)), num_keys=1)
    w_e = ((skey // NB) % E).reshape(nb, NW, R)
    w_t = ((skey % NB) * S).reshape(nb, NW, R)
    w_g = sgate.reshape(nb, NW, R)
    first = w_e[:, :, 0]
    npair = w_e[:, :, R - 1] - first + 1
    cum = jnp.cumsum(npair, axis=1)
    ntiles = cum[:, -1]
    SL = NW + E + 3
    q = jnp.arange(SL, dtype=jnp.int32)[None, :] - 2
    qc = jnp.clip(q, 0, ntiles[:, None] - 1)
    k_q = jnp.minimum(jnp.sum((cum[:, None, :] <= qc[:, :, None]).astype(jnp.int32), axis=2), NW - 1)
    onehot = (k_q[:, :, None] == jnp.arange(NW, dtype=jnp.int32)[None, None, :]).astype(F32)
    sel = lambda a: jnp.einsum('bsk,bkr->bsr', onehot, a.astype(F32), precision=lax.Precision.HIGHEST)
    selk = lambda a: jnp.sum(onehot * a.astype(F32)[:, None, :], axis=2).astype(jnp.int32)
    e_q = jnp.clip(selk(first) + qc - selk(cum - npair), 0, E - 1)
    real = jnp.logical_and(q >= 0, q < ntiles[:, None])
    match = jnp.logical_and(real[:, :, None], sel(w_e).astype(jnp.int32) == e_q[:, :, None])
    rows = jnp.where(match, sel(w_t).astype(jnp.int32), NB * S).reshape(nb * SL, 1, R)
    gate = jnp.where(match, sel(w_g), 0.0).reshape(nb * SL, 1, R)
    return e_q.reshape(-1), ntiles, rows, gate, SL


def _moe_step(src_ref, dst_ref, gate_ref, xg_s, acc_s, wg_ref, wu_ref, wd_ref, gbuf, cbuf, cy, sy, R):
    S = SUBLANES
    for r in range(R):
        t0 = pl.multiple_of(src_ref[0, 0, r], S)
        gbuf[r * S:(r + 1) * S, :] = xg_s[pl.ds(t0, S), :]

    x = jnp.concatenate([cbuf[pl.ds(s, R, stride=S), :] for s in range(S)], axis=1).astype(BF16)
    a = _silu(_dot(x, wg_ref[0])) * _dot(x, wu_ref[0])
    y = _dot(a.astype(BF16), wd_ref[0])
    for s in range(S):
        cy[pl.ds(s, R, stride=S), :] = y[:, s * LANES:(s + 1) * LANES]

    for r0 in range(0, R, RMW_BATCH):
        upd = []
        for r in range(r0, r0 + RMW_BATCH):
            a0 = pl.multiple_of(dst_ref[0, 0, r], S)
            upd.append((a0, acc_s[pl.ds(a0, S), :] + gate_ref[0, 0, r] * sy[r * S:(r + 1) * S, :]))
        for a0, val in upd:
            acc_s[pl.ds(a0, S), :] = val


def _moe_kernel(te_ref, nt_ref, src_ref, dst_ref, gate_ref, xg_hbm, wg_ref, wu_ref, wd_ref, out_hbm,
                xg_s, acc_s, buf0, buf1, y0, y1, *, NB, R):
    del te_ref
    b = pl.program_id(0)
    q = pl.program_id(1)

    @pl.when(jnp.logical_and(b == 0, q == 0))
    def _():
        for ref in (buf0, buf1, y0, y1):
            ref[...] = jnp.zeros_like(ref)

    @pl.when(q == 0)
    def _():
        pltpu.sync_copy(xg_hbm.at[b], xg_s.at[pl.ds(0, NB * SUBLANES)])
        xg_s[pl.ds(NB * SUBLANES, SUBLANES), :] = jnp.zeros((SUBLANES, LANES), F32)
        acc_s[...] = jnp.zeros_like(acc_s)

    active = q < nt_ref[b] + 2
    args = (src_ref, dst_ref, gate_ref, xg_s, acc_s, wg_ref, wu_ref, wd_ref)

    @pl.when(jnp.logical_and(active, q % 2 == 0))
    def _():
        _moe_step(*args, buf0, buf1, y1, y0, R)

    @pl.when(jnp.logical_and(active, q % 2 == 1))
    def _():
        _moe_step(*args, buf1, buf0, y0, y1, R)

    @pl.when(q == pl.num_programs(1) - 1)
    def _():
        pltpu.sync_copy(acc_s.at[pl.ds(0, NB * SUBLANES)], out_hbm.at[b])


def _moe_routed(xg, eidx, egate, wg, wu, wd):
    N = eidx.shape[1]
    E, D, DE = wg.shape
    NB = min(MOE_BLOCK, N)
    nb = N // NB
    S = SUBLANES
    R = MOE_ROWS if NB >= MOE_BLOCK else MOE_ROWS_SMALL
    te, ntiles, rows, gate, SL = _tile_schedule(eidx, egate, NB, E, R)
    smem = lambda shift: pl.BlockSpec((1, 1, R), lambda b, q, *_: (b * SL + q + shift, 0, 0),
                                      memory_space=pltpu.SMEM)
    w_map = lambda b, q, te_ref, nt_ref: (te_ref[b * SL + q + 1], 0, 0)
    tile_rows = pltpu.VMEM((R * S, LANES), F32)
    grid_spec = pltpu.PrefetchScalarGridSpec(
        num_scalar_prefetch=2,
        grid=(nb, SL - 2),
        in_specs=[smem(2), smem(0), smem(0),
                  pl.BlockSpec(memory_space=pl.ANY),
                  pl.BlockSpec((1, D, DE), w_map), pl.BlockSpec((1, D, DE), w_map), pl.BlockSpec((1, DE, D), w_map)],
        out_specs=pl.BlockSpec(memory_space=pl.ANY),
        scratch_shapes=[pltpu.VMEM(((NB + 1) * S, LANES), F32), pltpu.VMEM(((NB + 1) * S, LANES), F32),
                        tile_rows, tile_rows, tile_rows, tile_rows],
    )
    return pl.pallas_call(
        functools.partial(_moe_kernel, NB=NB, R=R),
        grid_spec=grid_spec,
        out_shape=jax.ShapeDtypeStruct((nb, NB * S, LANES), F32),
        compiler_params=_cparams(("arbitrary", "arbitrary")),
        name="moe_routed",
    )(te, ntiles, rows, rows, gate, xg.reshape(nb, NB * S, LANES), wg, wu, wd)


def _combine_kernel(x_ref, sh_ref, sc_ref, g2_ref, r_ref, sg_ref, su_ref, sd_ref, lg_ref, lb_ref, o_ref, *, alpha):
    x = x_ref[0]
    T = x.shape[0]
    h = (x * (1.0 + sc_ref[0]) + sh_ref[0]).astype(BF16)
    a = _silu(_dot(h, sg_ref[...])) * _dot(h, su_ref[...])
    shared = _dot(a.astype(BF16), sd_ref[...])
    routed = jnp.concatenate([r_ref[0, pl.ds(s, T, stride=SUBLANES), :] for s in range(SUBLANES)], axis=1)
    o_ref[0] = _layernorm(alpha * x + g2_ref[0] * (routed + shared), lg_ref[...], lb_ref[...])


def _combine(x1, mod3, routed, sg, su, sd, ln_g, ln_b, alpha):
    B, L, D = x1.shape
    T = min(TOKEN_TILE, L)
    nl = L // T
    NB = routed.shape[1] // SUBLANES
    per = NB // T
    tok = pl.BlockSpec((1, T, D), lambda b, l: (b, l, 0))
    r_spec = pl.BlockSpec((1, T * SUBLANES, LANES), lambda b, l: ((b * nl + l) // per, (b * nl + l) % per, 0))
    return pl.pallas_call(
        functools.partial(_combine_kernel, alpha=alpha),
        grid=(B, nl),
        in_specs=[tok] + _mod_specs(mod3, (3, 4, 5), T) + [r_spec]
        + [_const_spec(sg.shape), _const_spec(su.shape), _const_spec(sd.shape), _const_spec((1, D)),
           _const_spec((1, D))],
        out_specs=tok,
        out_shape=jax.ShapeDtypeStruct((B, L, D), F32),
        compiler_params=_cparams(("arbitrary", "arbitrary")),
        name="combine",
    )(x1, mod3, mod3, mod3, routed, sg, su, sd, ln_g.reshape(1, D), ln_b.reshape(1, D))


def _shift_cast_kernel(a_ref, b_ref, o_ref, *, shift):
    w = o_ref.shape[1]
    cat = jnp.concatenate([a_ref[...], b_ref[:, :shift]], axis=1)
    o_ref[...] = cat[:, shift:shift + w].astype(BF16)


def _shifted_bf16_columns(w, start, width):
    rows, cols = w.shape
    bw = rows
    shift = start % LANES
    base = start - shift
    assert width % bw == 0 and base % bw == 0 and bw % LANES == 0 and 0 < shift
    assert start + width <= cols
    return pl.pallas_call(
        functools.partial(_shift_cast_kernel, shift=shift),
        grid=(width // bw,),
        in_specs=[pl.BlockSpec((rows, bw), lambda j: (0, base // bw + j)),
                  pl.BlockSpec((rows, LANES), lambda j: (0, (base + (j + 1) * bw) // LANES))],
        out_specs=pl.BlockSpec((rows, bw), lambda j: (0, j)),
        out_shape=jax.ShapeDtypeStruct((rows, width), BF16),
        compiler_params=_cparams(("arbitrary",)),
        name="shift_cast",
    )(w, w)


def _split_w_in(w_in_l, gla_shape, hgrn_shape, D):
    Hg, Kg, Vg = gla_shape
    Hh, Kh, Vh = hgrn_shape
    rank = w_in_l.shape[1] - (2 * Hg * Kg + 2 * Hg * Vg + 2 * Hh * Kh + 2 * Hh * Vh + 2 * D)
    gla_w = (Hg * Kg, Hg * Kg, Hg * Vg, Hg * Vg)
    hgrn_w = (Hh * Kh, Hh * Kh, Hh * Vh, Hh * Vh, 2 * D)
    n_lo = sum(gla_w)
    lo = w_in_l[:, :n_lo].astype(BF16)
    mid = w_in_l[:, n_lo:n_lo + rank].astype(BF16)
    hi = _shifted_bf16_columns(w_in_l, n_lo + rank, sum(hgrn_w))
    out = []
    for arr, widths in ((lo, gla_w), (hi, hgrn_w)):
        start = 0
        for w in widths:
            out.append(_Cols(arr, start, w))
            start += w
    return out[:4] + [mid] + out[4:]


def kernel(x_prompt, x_sample, state_gla, state_hgrn, c_prompt, c_sample, w_ada, b_ada, w_in, w_gk2, b_gk,
           hgrn_lb, gla_norm_w, hgrn_norm_w, w_proj_a, w_proj_b, w_out, ln1_g, ln1_b, w_router, router_bias,
           w_exp_gate, w_exp_up, w_exp_down, w_sh_gate, w_sh_up, w_sh_down, ln2_g, ln2_b):
    depth = w_in.shape[0]
    BP, L, D = x_prompt.shape
    NS = x_sample.shape[0]
    assert x_sample.shape[1] == 1
    gla_shape = state_gla.shape[2:]
    hgrn_shape = state_hgrn.shape[2:]
    alpha = (2.0 * depth) ** 0.25

    xp = x_prompt
    xs = x_sample.reshape(NS, D)
    c_all = jnp.concatenate([c_prompt, c_sample], axis=0)
    new_gla_p, new_hgrn_p, new_gla_s, new_hgrn_s = [], [], [], []
    for l in range(depth):
        mod = _ada_mod(c_all, w_ada[l], b_ada[l])
        mod_p = mod[:BP].reshape(BP, 1, 6 * D)
        mod_s = mod[BP:]
        (wqa, wka, wva, wga, wgk1, wqb, wfb, wib, wgb, wuab) = _split_w_in(w_in[l], gla_shape, hgrn_shape, D)
        gla_w = [wqa, wka, wva, wga, wgk1, w_gk2[l].astype(BF16), b_gk[l].reshape(1, -1)]
        hgrn_w = [wqb, wfb, wib, wgb, hgrn_lb]
        wpa = w_proj_a[l].astype(BF16)
        wpb = w_proj_b[l].astype(BF16)
        wo = w_out[l].astype(BF16)
        wrT = w_router[l].T.astype(BF16)
        eg, eu, ed = w_exp_gate[l].astype(BF16), w_exp_up[l].astype(BF16), w_exp_down[l].astype(BF16)
        sg, su, sd = w_sh_gate[l].astype(BF16), w_sh_up[l].astype(BF16), w_sh_down[l].astype(BF16)

        def tail(x3, ya, yb, mod3):
            x1 = _merge(x3, ya, yb, mod3, wuab, wo, ln1_g[l], ln1_b[l], alpha)
            eidx, egate, xg = _router(x1, mod3, wrT, router_bias[l])
            routed = _moe_routed(xg, eidx, egate, eg, eu, ed)
            return _combine(x1, mod3, routed, sg, su, sd, ln2_g[l], ln2_b[l], alpha)

        ya, sg_p = _branch_prompt("gla", xp, mod_p, gla_w, gla_norm_w[l], wpa, *gla_shape, layer=l)
        yb, sh_p = _branch_prompt("hgrn", xp, mod_p, hgrn_w, hgrn_norm_w[l], wpb, *hgrn_shape, layer=l)
        xp = tail(xp, ya, yb, mod_p)
        new_gla_p.append(sg_p)
        new_hgrn_p.append(sh_p)

        ya, sg_s = _branch_sample("gla", xs, mod_s, gla_w, gla_norm_w[l], wpa, state_gla[l], layer=l)
        yb, sh_s = _branch_sample("hgrn", xs, mod_s, hgrn_w, hgrn_norm_w[l], wpb, state_hgrn[l], layer=l)
        xs = tail(xs[None], ya[None], yb[None], mod_s[None])[0]
        new_gla_s.append(sg_s)
        new_hgrn_s.append(sh_s)

    return (xp, xs.reshape(NS, 1, D), jnp.stack(new_gla_p), jnp.stack(new_hgrn_p),
            jnp.stack(new_gla_s), jnp.stack(new_hgrn_s))
```

```python
import functools

import jax
import jax.numpy as jnp
from jax import lax
from jax.experimental import pallas as pl
from jax.experimental.pallas import tpu as pltpu

F32 = jnp.float32
BF16 = jnp.bfloat16

GLA_GATE_NORMALIZER = 16.0
N_GROUPS = 8
TOPK_GROUPS = 4
TOP_K = 8
ROUTED_SCALE = 2.5
EPS = 1e-5

SUBLANES = 8
LANES = 128
VMEM_LIMIT_BYTES = 56 * 1024 * 1024

TOKEN_TILE = 512
MOE_BLOCK = 4096
MOE_ROWS = 256
MOE_ROWS_SMALL = 32
RMW_BATCH = 16
CHUNK = 128
SUB = SUBLANES
NEG_BIG = -1e30


def _cparams(sem):
    return pltpu.CompilerParams(dimension_semantics=sem, vmem_limit_bytes=VMEM_LIMIT_BYTES)


def _dot(a, b):
    return jnp.dot(a, b, preferred_element_type=F32)


def _dot_nt(a, b):
    return lax.dot_general(a, b, (((1,), (1,)), ((), ())), preferred_element_type=F32)


def _silu(x):
    return x * jax.nn.sigmoid(x)


def _log_sigmoid(x):
    return jnp.minimum(x, 0.0) - jnp.log1p(jnp.exp(-jnp.abs(x)))


def _layernorm(r, g, b):
    mu = jnp.mean(r, axis=-1, keepdims=True)
    d = r - mu
    var = jnp.mean(d * d, axis=-1, keepdims=True)
    return d * lax.rsqrt(var + EPS) * g + b


def _ada_kernel(c_ref, w_ref, b_ref, o_ref):
    c = c_ref[...]
    o_ref[...] = _dot(_silu(c).astype(BF16), w_ref[...].astype(BF16)) + b_ref[...]


def _ada_mod(c, w_ada, b_ada):
    R, D = c.shape
    N = w_ada.shape[1]
    tn = D
    return pl.pallas_call(
        _ada_kernel,
        grid=(N // tn,),
        in_specs=[pl.BlockSpec((R, D), lambda j: (0, 0)),
                  pl.BlockSpec((D, tn), lambda j: (0, j)),
                  pl.BlockSpec((1, tn), lambda j: (0, j))],
        out_specs=pl.BlockSpec((R, tn), lambda j: (0, j)),
        out_shape=jax.ShapeDtypeStruct((R, N), F32),
        compiler_params=_cparams(("arbitrary",)),
        name="ada_mod",
    )(c, w_ada, b_ada.reshape(1, N))


def _chunk_masks(C):
    row = lax.broadcasted_iota(jnp.int32, (C, 1), 0)
    ri = lax.broadcasted_iota(jnp.int32, (C, C), 0)
    ci = lax.broadcasted_iota(jnp.int32, (C, C), 1)
    levels = []
    s = SUB
    while s < C:
        same_group = (ri // (2 * s)) == (ci // (2 * s))
        levels.append((s, same_group))
        s *= 2
    diag = (ri // SUB) == (ci // SUB)
    return row, levels, diag


def _bcast_rows(x, group, idx):
    C, K = x.shape
    G = C // group
    x3 = x.reshape(G, group, K)
    return jnp.broadcast_to(x3[:, idx:idx + 1, :], (G, group, K)).reshape(C, K)


def _chunk_head(q, k, la, v, st, sel, masks):
    C, K = q.shape
    row, levels, diag = masks
    rmod = row % SUB

    x3 = la.reshape(C // SUB, SUB, K)
    sub3 = lax.broadcasted_iota(jnp.int32, (1, SUB, 1), 1)
    sh = 1
    while sh < SUB:
        x3 = x3 + jnp.where(sub3 >= sh, pltpu.roll(x3, sh, 1), 0.0)
        sh *= 2
    x = x3.reshape(C, K)
    x_sub = x

    sc = jnp.zeros((C, C), F32)
    for s, same_group in levels:
        G = C // (2 * s)
        x4 = x.reshape(G, 2, s, K)
        xl, xr = x4[:, 0], x4[:, 1]
        yl = jnp.broadcast_to(xl[:, s - 1:s, :], (G, s, K))
        qr = q.reshape(G, 2, s, K)[:, 1] * jnp.exp(xr)
        kl = k.reshape(G, 2, s, K)[:, 0] * jnp.exp(yl - xl)
        zero = jnp.zeros((G, s, K), F32)
        qf = jnp.stack([zero, qr], axis=1).reshape(C, K).astype(BF16)
        kf = jnp.stack([kl, zero], axis=1).reshape(C, K).astype(BF16)
        sc = sc + jnp.where(same_group, _dot_nt(qf, kf), 0.0)
        x = jnp.stack([xl, xr + yl], axis=1).reshape(C, K)
    b = x

    terms = []
    for jj in range(SUB):
        kb = _bcast_rows(k, SUB, jj)
        xb = _bcast_rows(x_sub, SUB, jj)
        e = jnp.where(rmod >= jj, x_sub - xb, NEG_BIG)
        terms.append((q * kb * jnp.exp(e)).astype(BF16))
    d = _dot(jnp.concatenate(terms, axis=1), sel)
    sc = sc + jnp.where(diag, d, 0.0)

    vb = v.astype(BF16)
    o = _dot(sc.astype(BF16), vb) + _dot_nt((q * jnp.exp(b)).astype(BF16), st.astype(BF16))
    b_last = b[C - 1:C, :]
    kd = (k * jnp.exp(b_last - b)).astype(BF16)
    st_new = st * jnp.exp(b_last) + _dot(v.T.astype(BF16), kd)
    return o, st_new


def _recurrence_tile(q_ref, k_ref, la_ref, v_ref, o_ref, st_ref, sel_ref, n_heads, K, V, T):
    C = CHUNK
    masks = _chunk_masks(C)
    sel = sel_ref[...]

    def body(c, carry):
        r0 = pl.multiple_of(c * C, C)
        for h in range(n_heads):
            ks = slice(h * K, (h + 1) * K)
            vs = slice(h * V, (h + 1) * V)
            o, st_new = _chunk_head(q_ref[pl.ds(r0, C), ks], k_ref[pl.ds(r0, C), ks],
                                    la_ref[pl.ds(r0, C), ks], v_ref[pl.ds(r0, C), vs],
                                    st_ref[h], sel, masks)
            o_ref[pl.ds(r0, C), vs] = o
            st_ref[h] = st_new
        return carry

    lax.fori_loop(0, T // C, body, 0)


def _branch_kernel(*refs, kind, n_heads, K, V, T, layer):
    if kind == "gla":
        (x_ref, sh_ref, sc_ref, wq_ref, wk_ref, wv_ref, wg_ref, wgk1_ref, wgk2_ref, bgk_ref,
         nw_ref, wp_ref, sel_ref, y_ref, sout_ref,
         q_s, k_s, la_s, v_s, g_s, o_s, st_s) = refs
    else:
        (x_ref, sh_ref, sc_ref, wq_ref, wk_ref, wv_ref, wg_ref, lb_ref,
         nw_ref, wp_ref, sel_ref, y_ref, sout_ref,
         q_s, k_s, la_s, v_s, g_s, o_s, st_s) = refs
    lt = pl.program_id(1)

    @pl.when(lt == 0)
    def _():
        st_s[...] = jnp.zeros_like(st_s)

    h = (x_ref[0] * (1.0 + sc_ref[0]) + sh_ref[0]).astype(BF16)
    scale = K ** -0.5
    if kind == "gla":
        q_s[...] = _dot(h, wq_ref[...]) * scale
        k_s[...] = _dot(h, wk_ref[...])
        lr = _dot(h, wgk1_ref[...]).astype(BF16)
        la_s[...] = _log_sigmoid(_dot(lr, wgk2_ref[...]) + bgk_ref[...]) * (1.0 / GLA_GATE_NORMALIZER)
    else:
        q_s[...] = _silu(_dot(h, wq_ref[...])) * scale
        lbp = lb_ref[...]
        e = jnp.exp(lbp - jnp.max(lbp, axis=0, keepdims=True))
        lb = jnp.sum(e[:layer + 1], axis=0, keepdims=True) / jnp.sum(e, axis=0, keepdims=True)
        forget = lb + (1.0 - lb) * jax.nn.sigmoid(_dot(h, wk_ref[...]))
        k_s[...] = 1.0 - forget
        la_s[...] = jnp.log(forget)
    v_s[...] = _dot(h, wv_ref[...])
    g_s[...] = _dot(h, wg_ref[...])

    _recurrence_tile(q_s, k_s, la_s, v_s, o_s, st_s, sel_ref, n_heads, K, V, T)

    nw = nw_ref[...]
    outs = []
    for hd in range(n_heads):
        vs = slice(hd * V, (hd + 1) * V)
        o = o_s[:, vs]
        g = g_s[:, vs]
        gate = _silu(g) if kind == "gla" else jax.nn.sigmoid(g)
        o = o * lax.rsqrt(jnp.mean(o * o, axis=-1, keepdims=True) + EPS) * nw * gate
        outs.append(o.astype(BF16))
    y_ref[0] = _dot(jnp.concatenate(outs, axis=1), wp_ref[...])

    @pl.when(lt == pl.num_programs(1) - 1)
    def _():
        for hd in range(n_heads):
            sout_ref[0, hd] = st_s[hd].T


def _sel_matrix(K, C):
    r = jnp.arange(SUB * K, dtype=jnp.int32)[:, None] // K
    c = jnp.arange(C, dtype=jnp.int32)[None, :] % SUB
    return (r == c).astype(BF16)


def _const_spec(shape):
    nd = len(shape)
    return pl.BlockSpec(shape, lambda b, l: (0,) * nd)


class _Cols:
    def __init__(self, arr, start, width):
        assert start % width == 0
        self.arr, self.width, self.index = arr, width, start // width


def _w_array(w):
    return w.arr if isinstance(w, _Cols) else w


def _w_spec(w):
    if isinstance(w, _Cols):
        idx = w.index
        return pl.BlockSpec((w.arr.shape[0], w.width), lambda *g: (0, idx))
    nd = len(w.shape)
    return pl.BlockSpec(w.shape, lambda *g: (0,) * nd)


def _branch_prompt(kind, x, mod3, weights, norm_w, w_proj, n_heads, K, V, layer):
    B, L, D = x.shape
    T = min(TOKEN_TILE, L)
    HK, HV = n_heads * K, n_heads * V
    sel = _sel_matrix(K, CHUNK)
    x_spec = pl.BlockSpec((1, T, D), lambda b, l: (b, l, 0))
    sh_spec = pl.BlockSpec((1, 1, D), lambda b, l: (b, 0, 0))
    sc_spec = pl.BlockSpec((1, 1, D), lambda b, l: (b, 0, 1))
    w_specs = [_w_spec(w) for w in weights]
    nw2 = norm_w.reshape(1, V)
    in_specs = [x_spec, sh_spec, sc_spec] + w_specs + [_const_spec(nw2.shape), _const_spec(w_proj.shape),
                                                       _const_spec(sel.shape)]
    kern = functools.partial(_branch_kernel, kind=kind, n_heads=n_heads, K=K, V=V, T=T, layer=layer)
    return pl.pallas_call(
        kern,
        grid=(B, L // T),
        in_specs=in_specs,
        out_specs=[pl.BlockSpec((1, T, D), lambda b, l: (b, l, 0)),
                   pl.BlockSpec((1, n_heads, K, V), lambda b, l: (b, 0, 0, 0))],
        out_shape=[jax.ShapeDtypeStruct((B, L, D), F32),
                   jax.ShapeDtypeStruct((B, n_heads, K, V), F32)],
        scratch_shapes=[pltpu.VMEM((T, HK), F32), pltpu.VMEM((T, HK), F32), pltpu.VMEM((T, HK), F32),
                        pltpu.VMEM((T, HV), F32), pltpu.VMEM((T, HV), F32), pltpu.VMEM((T, HV), F32),
                        pltpu.VMEM((n_heads, V, K), F32)],
        compiler_params=_cparams(("arbitrary", "arbitrary")),
        name=f"{kind}_prompt",
    )(x, mod3, mod3, *[_w_array(w) for w in weights], nw2, w_proj, sel)


def _sample_kernel(*refs, kind, n_heads, K, V, TB, layer):
    if kind == "gla":
        (x_ref, sh_ref, sc_ref, wq_ref, wk_ref, wv_ref, wg_ref, wgk1_ref, wgk2_ref, bgk_ref,
         nw_ref, wp_ref, s_ref, y_ref, sout_ref, qT_s, kT_s, aT_s, v_s, g_s, o_s) = refs
    else:
        (x_ref, sh_ref, sc_ref, wq_ref, wk_ref, wv_ref, wg_ref, lb_ref,
         nw_ref, wp_ref, s_ref, y_ref, sout_ref, qT_s, kT_s, aT_s, v_s, g_s, o_s) = refs
    step = pl.program_id(0)
    NT = x_ref.shape[0]

    @pl.when(step == 0)
    def _():
        h = (x_ref[...] * (1.0 + sc_ref[...]) + sh_ref[...]).astype(BF16)
        scale = K ** -0.5
        if kind == "gla":
            q = _dot(h, wq_ref[...]) * scale
            k = _dot(h, wk_ref[...])
            lr = _dot(h, wgk1_ref[...]).astype(BF16)
            a = jnp.exp(_log_sigmoid(_dot(lr, wgk2_ref[...]) + bgk_ref[...]) * (1.0 / GLA_GATE_NORMALIZER))
        else:
            q = _silu(_dot(h, wq_ref[...])) * scale
            lbp = lb_ref[...]
            e = jnp.exp(lbp - jnp.max(lbp, axis=0, keepdims=True))
            lb = jnp.sum(e[:layer + 1], axis=0, keepdims=True) / jnp.sum(e, axis=0, keepdims=True)
            a = lb + (1.0 - lb) * jax.nn.sigmoid(_dot(h, wk_ref[...]))
            k = 1.0 - a
        for hd in range(n_heads):
            ks = slice(hd * K, (hd + 1) * K)
            qT_s[ks, :] = q[:, ks].T
            kT_s[ks, :] = k[:, ks].T
            aT_s[ks, :] = a[:, ks].T
        v_s[...] = _dot(h, wv_ref[...])
        g_s[...] = _dot(h, wg_ref[...])

    lane = lax.broadcasted_iota(jnp.int32, (1, NT), 1)
    sub = lax.broadcasted_iota(jnp.int32, (TB, 1), 0)
    t0 = pl.multiple_of(step * TB, TB)
    for hd in range(n_heads):
        ks = slice(hd * K, (hd + 1) * K)
        vs = slice(hd * V, (hd + 1) * V)
        v_rows = v_s[pl.ds(t0, TB), vs]
        o_rows = jnp.zeros((TB, V), F32)
        for j in range(TB):
            pick = lane == t0 + j
            acol = jnp.sum(jnp.where(pick, aT_s[ks, :], 0.0), axis=1, keepdims=True)
            kcol = jnp.sum(jnp.where(pick, kT_s[ks, :], 0.0), axis=1, keepdims=True)
            qcol = jnp.sum(jnp.where(pick, qT_s[ks, :], 0.0), axis=1, keepdims=True)
            s1 = acol * s_ref[j, hd] + kcol * v_rows[j:j + 1, :]
            sout_ref[j, hd] = s1
            o_rows = jnp.where(sub == j, jnp.sum(qcol * s1, axis=0, keepdims=True), o_rows)
        o_s[pl.ds(t0, TB), vs] = o_rows

    @pl.when(step == pl.num_programs(0) - 1)
    def _():
        nw = nw_ref[...]
        outs = []
        for hd in range(n_heads):
            vs = slice(hd * V, (hd + 1) * V)
            o = o_s[:, vs]
            g = g_s[:, vs]
            gate = _silu(g) if kind == "gla" else jax.nn.sigmoid(g)
            o = o * lax.rsqrt(jnp.mean(o * o, axis=-1, keepdims=True) + EPS) * nw * gate
            outs.append(o.astype(BF16))
        y_ref[...] = _dot(jnp.concatenate(outs, axis=1), wp_ref[...])


def _branch_sample(kind, x, mod, weights, norm_w, w_proj, state, layer):
    NT, D = x.shape
    _, n_heads, K, V = state.shape
    HK, HV = n_heads * K, n_heads * V
    TB = SUBLANES
    c1 = lambda s: pl.BlockSpec(s, lambda i: (0,) * len(s))
    nw2 = norm_w.reshape(1, V)
    in_specs = ([c1((NT, D)), pl.BlockSpec((NT, D), lambda i: (0, 0)), pl.BlockSpec((NT, D), lambda i: (0, 1))]
                + [_w_spec(w) for w in weights] + [c1(nw2.shape), c1(w_proj.shape),
                                                    pl.BlockSpec((TB, n_heads, K, V), lambda i: (i, 0, 0, 0))])
    kern = functools.partial(_sample_kernel, kind=kind, n_heads=n_heads, K=K, V=V, TB=TB, layer=layer)
    return pl.pallas_call(
        kern,
        grid=(NT // TB,),
        in_specs=in_specs,
        out_specs=[c1((NT, D)), pl.BlockSpec((TB, n_heads, K, V), lambda i: (i, 0, 0, 0))],
        out_shape=[jax.ShapeDtypeStruct((NT, D), F32), jax.ShapeDtypeStruct(state.shape, F32)],
        scratch_shapes=[pltpu.VMEM((HK, NT), F32), pltpu.VMEM((HK, NT), F32), pltpu.VMEM((HK, NT), F32),
                        pltpu.VMEM((NT, HV), F32), pltpu.VMEM((NT, HV), F32), pltpu.VMEM((NT, HV), F32)],
        compiler_params=_cparams(("arbitrary",)),
        name=f"{kind}_sample",
    )(x, mod, mod, *[_w_array(w) for w in weights], nw2, w_proj, state)


def _merge_kernel(x_ref, ya_ref, yb_ref, sh_ref, sc_ref, g_ref, wu_ref, wo_ref, lg_ref, lb_ref, o_ref, *, alpha):
    x = x_ref[0]
    D = x.shape[-1]
    h = (x * (1.0 + sc_ref[0]) + sh_ref[0]).astype(BF16)
    u = _dot(h, wu_ref[...])
    merged = jax.nn.sigmoid(u[:, :D]) * ya_ref[0] + jax.nn.sigmoid(u[:, D:]) * yb_ref[0]
    mix = _dot(merged.astype(BF16), wo_ref[...])
    o_ref[0] = _layernorm(alpha * x + g_ref[0] * mix, lg_ref[...], lb_ref[...])


def _mod_specs(mod3, cols, T):
    D = mod3.shape[-1] // 6
    if mod3.shape[1] == 1:
        return [pl.BlockSpec((1, 1, D), functools.partial(lambda b, l, *_, c: (b, 0, c), c=c)) for c in cols]
    return [pl.BlockSpec((1, T, D), functools.partial(lambda b, l, *_, c: (b, l, c), c=c)) for c in cols]


def _merge(x, ya, yb, mod3, wu, w_out, ln_g, ln_b, alpha):
    B, L, D = x.shape
    T = min(TOKEN_TILE, L)
    tok = pl.BlockSpec((1, T, D), lambda b, l: (b, l, 0))
    return pl.pallas_call(
        functools.partial(_merge_kernel, alpha=alpha),
        grid=(B, L // T),
        in_specs=[tok, tok, tok] + _mod_specs(mod3, (0, 1, 2), T)
        + [_w_spec(wu), _const_spec(w_out.shape), _const_spec((1, D)), _const_spec((1, D))],
        out_specs=tok,
        out_shape=jax.ShapeDtypeStruct((B, L, D), F32),
        compiler_params=_cparams(("arbitrary", "arbitrary")),
        name="merge",
    )(x, ya, yb, mod3, mod3, mod3, _w_array(wu), w_out, ln_g.reshape(1, D), ln_b.reshape(1, D))


def _first_argmax(vals, iota, n, axis):
    m = jnp.max(vals, axis=axis, keepdims=True)
    idx = jnp.min(jnp.where(vals == m, iota, n), axis=axis, keepdims=True)
    return m, idx


def _router_kernel(x_ref, sh_ref, sc_ref, wrT_ref, bias_ref, eidx_ref, egate_ref, xg_ref, *, n_experts):
    E = n_experts
    per = E // N_GROUPS
    hf = x_ref[0] * (1.0 + sc_ref[0]) + sh_ref[0]
    h = hf.astype(BF16)
    T = h.shape[0]
    for s in range(hf.shape[1] // LANES):
        xg_ref[pl.ds(s, T, stride=SUBLANES), :] = hf[:, s * LANES:(s + 1) * LANES]
    scores = jax.nn.sigmoid(_dot_nt(wrT_ref[...], h))
    biased = scores + bias_ref[...]
    b3 = biased.reshape(N_GROUPS, per, T)
    i3 = lax.broadcasted_iota(jnp.int32, (N_GROUPS, per, T), 1)
    m1, a1 = _first_argmax(b3, i3, per, 1)
    m2 = jnp.max(jnp.where(i3 == a1, -jnp.inf, b3), axis=1, keepdims=True)
    gscore = (m1 + m2).reshape(N_GROUPS, T)
    gi = lax.broadcasted_iota(jnp.int32, (N_GROUPS, T), 0)
    gsel = jnp.zeros((N_GROUPS, T), jnp.bool_)
    for _ in range(TOPK_GROUPS):
        _, a = _first_argmax(gscore, gi, N_GROUPS, 0)
        hit = gi == a
        gsel = jnp.logical_or(gsel, hit)
        gscore = jnp.where(hit, -jnp.inf, gscore)
    emask = jnp.broadcast_to(gsel.reshape(N_GROUPS, 1, T), (N_GROUPS, per, T)).reshape(E, T)
    cand = jnp.where(emask, biased, -jnp.inf)
    ei = lax.broadcasted_iota(jnp.int32, (E, T), 0)
    picks, weights = [], []
    for _ in range(TOP_K):
        _, a = _first_argmax(cand, ei, E, 0)
        hit = ei == a
        picks.append(a)
        weights.append(jnp.sum(jnp.where(hit, scores, 0.0), axis=0, keepdims=True))
        cand = jnp.where(hit, -jnp.inf, cand)
    w = jnp.concatenate(weights, axis=0)
    egate_ref[...] = w / jnp.sum(w, axis=0, keepdims=True) * ROUTED_SCALE
    eidx_ref[...] = jnp.concatenate(picks, axis=0)


def _router(x1, mod3, wrT, bias):
    B, L, D = x1.shape
    assert D == SUBLANES * LANES
    E = wrT.shape[0]
    T = min(TOKEN_TILE, L)
    nl = L // T
    N = B * L
    tok = pl.BlockSpec((1, T, D), lambda b, l: (b, l, 0))
    pick_spec = pl.BlockSpec((TOP_K, T), lambda b, l: (0, b * nl + l))
    return pl.pallas_call(
        functools.partial(_router_kernel, n_experts=E),
        grid=(B, nl),
        in_specs=[tok] + _mod_specs(mod3, (3, 4), T) + [_const_spec(wrT.shape), _const_spec((E, 1))],
        out_specs=[pick_spec, pick_spec, pl.BlockSpec((T * SUBLANES, LANES), lambda b, l: (b * nl + l, 0))],
        out_shape=[jax.ShapeDtypeStruct((TOP_K, N), jnp.int32), jax.ShapeDtypeStruct((TOP_K, N), F32),
                   jax.ShapeDtypeStruct((N * SUBLANES, LANES), F32)],
        compiler_params=_cparams(("arbitrary", "arbitrary")),
        name="router",
    )(x1, mod3, mod3, wrT, bias.reshape(E, 1))


def _tile_schedule(eidx, egate, NB, E, R):
    Kp, N = eidx.shape
    nb = N // NB
    A = Kp * NB
    S = SUBLANES
    assert A % R == 0
    NW = A // R
    tok = jnp.broadcast_to(jnp.arange(N, dtype=jnp.int32)[None, :], (Kp, N))
    assert nb * E * NB < 2 ** 31
    key = ((tok // NB) * E + eidx) * NB + tok % NB
    skey, sgate = lax.sort((key.reshape(-1), egate.reshape(-1)), num_keys=1)
    w_e = ((skey // NB) % E).reshape(nb, NW, R)
    w_t = ((skey % NB) * S).reshape(nb, NW, R)
    w_g = sgate.reshape(nb, NW, R)
    first = w_e[:, :, 0]
    npair = w_e[:, :, R - 1] - first + 1
    cum = jnp.cumsum(npair, axis=1)
    ntiles = cum[:, -1]
    SL = NW + E + 3
    q = jnp.arange(SL, dtype=jnp.int32)[None, :] - 2
    qc = jnp.clip(q, 0, ntiles[:, None] - 1)
    k_q = jnp.minimum(jnp.sum((cum[:, None, :] <= qc[:, :, None]).astype(jnp.int32), axis=2), NW - 1)
    onehot = (k_q[:, :, None] == jnp.arange(NW, dtype=jnp.int32)[None, None, :]).astype(F32)
    sel = lambda a: jnp.einsum('bsk,bkr->bsr', onehot, a.astype(F32), precision=lax.Precision.HIGHEST)
    selk = lambda a: jnp.sum(onehot * a.astype(F32)[:, None, :], axis=2).astype(jnp.int32)
    e_q = jnp.clip(selk(first) + qc - selk(cum - npair), 0, E - 1)
    real = jnp.logical_and(q >= 0, q < ntiles[:, None])
    match = jnp.logical_and(real[:, :, None], sel(w_e).astype(jnp.int32) == e_q[:, :, None])
    rows = jnp.where(match, sel(w_t).astype(jnp.int32), NB * S).reshape(nb * SL, 1, R)
    gate = jnp.where(match, sel(w_g), 0.0).reshape(nb * SL, 1, R)
    return e_q.reshape(-1), ntiles, rows, gate, SL


def _moe_step(src_ref, dst_ref, gate_ref, xg_s, acc_s, wg_ref, wu_ref, wd_ref, gbuf, cbuf, cy, sy, R):
    S = SUBLANES
    for r in range(R):
        t0 = pl.multiple_of(src_ref[0, 0, r], S)
        gbuf[r * S:(r + 1) * S, :] = xg_s[pl.ds(t0, S), :]

    x = jnp.concatenate([cbuf[pl.ds(s, R, stride=S), :] for s in range(S)], axis=1).astype(BF16)
    a = _silu(_dot(x, wg_ref[0])) * _dot(x, wu_ref[0])
    y = _dot(a.astype(BF16), wd_ref[0])
    for s in range(S):
        cy[pl.ds(s, R, stride=S), :] = y[:, s * LANES:(s + 1) * LANES]

    for r0 in range(0, R, RMW_BATCH):
        upd = []
        for r in range(r0, r0 + RMW_BATCH):
            a0 = pl.multiple_of(dst_ref[0, 0, r], S)
            upd.append((a0, acc_s[pl.ds(a0, S), :] + gate_ref[0, 0, r] * sy[r * S:(r + 1) * S, :]))
        for a0, val in upd:
            acc_s[pl.ds(a0, S), :] = val


def _moe_kernel(te_ref, nt_ref, src_ref, dst_ref, gate_ref, xg_hbm, wg_ref, wu_ref, wd_ref, out_hbm,
                xg_s, acc_s, buf0, buf1, y0, y1, *, NB, R):
    del te_ref
    b = pl.program_id(0)
    q = pl.program_id(1)

    @pl.when(jnp.logical_and(b == 0, q == 0))
    def _():
        for ref in (buf0, buf1, y0, y1):
            ref[...] = jnp.zeros_like(ref)

    @pl.when(q == 0)
    def _():
        pltpu.sync_copy(xg_hbm.at[b], xg_s.at[pl.ds(0, NB * SUBLANES)])
        xg_s[pl.ds(NB * SUBLANES, SUBLANES), :] = jnp.zeros((SUBLANES, LANES), F32)
        acc_s[...] = jnp.zeros_like(acc_s)

    active = q < nt_ref[b] + 2
    args = (src_ref, dst_ref, gate_ref, xg_s, acc_s, wg_ref, wu_ref, wd_ref)

    @pl.when(jnp.logical_and(active, q % 2 == 0))
    def _():
        _moe_step(*args, buf0, buf1, y1, y0, R)

    @pl.when(jnp.logical_and(active, q % 2 == 1))
    def _():
        _moe_step(*args, buf1, buf0, y0, y1, R)

    @pl.when(q == pl.num_programs(1) - 1)
    def _():
        pltpu.sync_copy(acc_s.at[pl.ds(0, NB * SUBLANES)], out_hbm.at[b])


def _moe_routed(xg, eidx, egate, wg, wu, wd):
    N = eidx.shape[1]
    E, D, DE = wg.shape
    NB = min(MOE_BLOCK, N)
    nb = N // NB
    S = SUBLANES
    R = MOE_ROWS if NB >= MOE_BLOCK else MOE_ROWS_SMALL
    te, ntiles, rows, gate, SL = _tile_schedule(eidx, egate, NB, E, R)
    smem = lambda shift: pl.BlockSpec((1, 1, R), lambda b, q, *_: (b * SL + q + shift, 0, 0),
                                      memory_space=pltpu.SMEM)
    w_map = lambda b, q, te_ref, nt_ref: (te_ref[b * SL + q + 1], 0, 0)
    tile_rows = pltpu.VMEM((R * S, LANES), F32)
    grid_spec = pltpu.PrefetchScalarGridSpec(
        num_scalar_prefetch=2,
        grid=(nb, SL - 2),
        in_specs=[smem(2), smem(0), smem(0),
                  pl.BlockSpec(memory_space=pl.ANY),
                  pl.BlockSpec((1, D, DE), w_map), pl.BlockSpec((1, D, DE), w_map), pl.BlockSpec((1, DE, D), w_map)],
        out_specs=pl.BlockSpec(memory_space=pl.ANY),
        scratch_shapes=[pltpu.VMEM(((NB + 1) * S, LANES), F32), pltpu.VMEM(((NB + 1) * S, LANES), F32),
                        tile_rows, tile_rows, tile_rows, tile_rows],
    )
    return pl.pallas_call(
        functools.partial(_moe_kernel, NB=NB, R=R),
        grid_spec=grid_spec,
        out_shape=jax.ShapeDtypeStruct((nb, NB * S, LANES), F32),
        compiler_params=_cparams(("arbitrary", "arbitrary")),
        name="moe_routed",
    )(te, ntiles, rows, rows, gate, xg.reshape(nb, NB * S, LANES), wg, wu, wd)


def _combine_kernel(x_ref, sh_ref, sc_ref, g2_ref, r_ref, sg_ref, su_ref, sd_ref, lg_ref, lb_ref, o_ref, *, alpha):
    x = x_ref[0]
    T = x.shape[0]
    h = (x * (1.0 + sc_ref[0]) + sh_ref[0]).astype(BF16)
    a = _silu(_dot(h, sg_ref[...])) * _dot(h, su_ref[...])
    shared = _dot(a.astype(BF16), sd_ref[...])
    routed = jnp.concatenate([r_ref[0, pl.ds(s, T, stride=SUBLANES), :] for s in range(SUBLANES)], axis=1)
    o_ref[0] = _layernorm(alpha * x + g2_ref[0] * (routed + shared), lg_ref[...], lb_ref[...])


def _combine(x1, mod3, routed, sg, su, sd, ln_g, ln_b, alpha):
    B, L, D = x1.shape
    T = min(TOKEN_TILE, L)
    nl = L // T
    NB = routed.shape[1] // SUBLANES
    per = NB // T
    tok = pl.BlockSpec((1, T, D), lambda b, l: (b, l, 0))
    r_spec = pl.BlockSpec((1, T * SUBLANES, LANES), lambda b, l: ((b * nl + l) // per, (b * nl + l) % per, 0))
    return pl.pallas_call(
        functools.partial(_combine_kernel, alpha=alpha),
        grid=(B, nl),
        in_specs=[tok] + _mod_specs(mod3, (3, 4, 5), T) + [r_spec]
        + [_const_spec(sg.shape), _const_spec(su.shape), _const_spec(sd.shape), _const_spec((1, D)),
           _const_spec((1, D))],
        out_specs=tok,
        out_shape=jax.ShapeDtypeStruct((B, L, D), F32),
        compiler_params=_cparams(("arbitrary", "arbitrary")),
        name="combine",
    )(x1, mod3, mod3, mod3, routed, sg, su, sd, ln_g.reshape(1, D), ln_b.reshape(1, D))


def _split_w_in(w_in_l, gla_shape, hgrn_shape, D):
    Hg, Kg, Vg = gla_shape
    Hh, Kh, Vh = hgrn_shape
    rank = w_in_l.shape[1] - (2 * Hg * Kg + 2 * Hg * Vg + 2 * Hh * Kh + 2 * Hh * Vh + 2 * D)
    gla_w = (Hg * Kg, Hg * Kg, Hg * Vg, Hg * Vg)
    hgrn_w = (Hh * Kh, Hh * Kh, Hh * Vh, Hh * Vh, 2 * D)
    n_lo = sum(gla_w)
    lo = w_in_l[:, :n_lo].astype(BF16)
    mid = w_in_l[:, n_lo:n_lo + rank].astype(BF16)
    hi = w_in_l[:, n_lo + rank:].astype(BF16)
    out = []
    for arr, widths in ((lo, gla_w), (hi, hgrn_w)):
        start = 0
        for w in widths:
            out.append(_Cols(arr, start, w))
            start += w
    return out[:4] + [mid] + out[4:]


def kernel(x_prompt, x_sample, state_gla, state_hgrn, c_prompt, c_sample, w_ada, b_ada, w_in, w_gk2, b_gk,
           hgrn_lb, gla_norm_w, hgrn_norm_w, w_proj_a, w_proj_b, w_out, ln1_g, ln1_b, w_router, router_bias,
           w_exp_gate, w_exp_up, w_exp_down, w_sh_gate, w_sh_up, w_sh_down, ln2_g, ln2_b):
    depth = w_in.shape[0]
    BP, L, D = x_prompt.shape
    NS = x_sample.shape[0]
    assert x_sample.shape[1] == 1
    gla_shape = state_gla.shape[2:]
    hgrn_shape = state_hgrn.shape[2:]
    alpha = (2.0 * depth) ** 0.25

    xp = x_prompt
    xs = x_sample.reshape(NS, D)
    c_all = jnp.concatenate([c_prompt, c_sample], axis=0)
    new_gla_p, new_hgrn_p, new_gla_s, new_hgrn_s = [], [], [], []
    for l in range(depth):
        mod = _ada_mod(c_all, w_ada[l], b_ada[l])
        mod_p = mod[:BP].reshape(BP, 1, 6 * D)
        mod_s = mod[BP:]
        (wqa, wka, wva, wga, wgk1, wqb, wfb, wib, wgb, wuab) = _split_w_in(w_in[l], gla_shape, hgrn_shape, D)
        gla_w = [wqa, wka, wva, wga, wgk1, w_gk2[l].astype(BF16), b_gk[l].reshape(1, -1)]
        hgrn_w = [wqb, wfb, wib, wgb, hgrn_lb]
        wpa = w_proj_a[l].astype(BF16)
        wpb = w_proj_b[l].astype(BF16)
        wo = w_out[l].astype(BF16)
        wrT = w_router[l].T.astype(BF16)
        eg, eu, ed = w_exp_gate[l].astype(BF16), w_exp_up[l].astype(BF16), w_exp_down[l].astype(BF16)
        sg, su, sd = w_sh_gate[l].astype(BF16), w_sh_up[l].astype(BF16), w_sh_down[l].astype(BF16)

        def tail(x3, ya, yb, mod3):
            x1 = _merge(x3, ya, yb, mod3, wuab, wo, ln1_g[l], ln1_b[l], alpha)
            eidx, egate, xg = _router(x1, mod3, wrT, router_bias[l])
            routed = _moe_routed(xg, eidx, egate, eg, eu, ed)
            return _combine(x1, mod3, routed, sg, su, sd, ln2_g[l], ln2_b[l], alpha)

        ya, sg_p = _branch_prompt("gla", xp, mod_p, gla_w, gla_norm_w[l], wpa, *gla_shape, layer=l)
        yb, sh_p = _branch_prompt("hgrn", xp, mod_p, hgrn_w, hgrn_norm_w[l], wpb, *hgrn_shape, layer=l)
        xp = tail(xp, ya, yb, mod_p)
        new_gla_p.append(sg_p)
        new_hgrn_p.append(sh_p)

        ya, sg_s = _branch_sample("gla", xs, mod_s, gla_w, gla_norm_w[l], wpa, state_gla[l], layer=l)
        yb, sh_s = _branch_sample("hgrn", xs, mod_s, hgrn_w, hgrn_norm_w[l], wpb, state_hgrn[l], layer=l)
        xs = tail(xs[None], ya[None], yb[None], mod_s[None])[0]
        new_gla_s.append(sg_s)
        new_hgrn_s.append(sh_s)

    return (xp, xs.reshape(NS, 1, D), jnp.stack(new_gla_p), jnp.stack(new_hgrn_p),
            jnp.stack(new_gla_s), jnp.stack(new_hgrn_s))
```

```python
import functools

import jax
import jax.numpy as jnp
from jax import lax
from jax.experimental import pallas as pl
from jax.experimental.pallas import tpu as pltpu

F32 = jnp.float32
BF16 = jnp.bfloat16

GLA_GATE_NORMALIZER = 16.0
N_GROUPS = 8
TOPK_GROUPS = 4
TOP_K = 8
ROUTED_SCALE = 2.5
EPS = 1e-5

SUBLANES = 8
LANES = 128
VMEM_LIMIT_BYTES = 56 * 1024 * 1024

TOKEN_TILE = 512
MOE_BLOCK = 4096
MOE_ROWS = 256
MOE_ROWS_SMALL = 64
RMW_BATCH = 8
CHUNK = 128
SUB = SUBLANES
NEG_BIG = -1e30


def _cparams(sem):
    return pltpu.CompilerParams(dimension_semantics=sem, vmem_limit_bytes=VMEM_LIMIT_BYTES)


def _dot(a, b):
    return jnp.dot(a, b, preferred_element_type=F32)


def _dot_nt(a, b):
    return lax.dot_general(a, b, (((1,), (1,)), ((), ())), preferred_element_type=F32)


def _silu(x):
    return x * jax.nn.sigmoid(x)


def _log_sigmoid(x):
    return jnp.minimum(x, 0.0) - jnp.log1p(jnp.exp(-jnp.abs(x)))


def _layernorm(r, g, b):
    mu = jnp.mean(r, axis=-1, keepdims=True)
    d = r - mu
    var = jnp.mean(d * d, axis=-1, keepdims=True)
    return d * lax.rsqrt(var + EPS) * g + b


def _ada_kernel(c_ref, w_ref, b_ref, o_ref):
    c = c_ref[...]
    o_ref[...] = _dot(_silu(c).astype(BF16), w_ref[...].astype(BF16)) + b_ref[...]


def _ada_mod(c, w_ada, b_ada):
    R, D = c.shape
    N = w_ada.shape[1]
    tn = D
    return pl.pallas_call(
        _ada_kernel,
        grid=(N // tn,),
        in_specs=[pl.BlockSpec((R, D), lambda j: (0, 0)),
                  pl.BlockSpec((D, tn), lambda j: (0, j)),
                  pl.BlockSpec((1, tn), lambda j: (0, j))],
        out_specs=pl.BlockSpec((R, tn), lambda j: (0, j)),
        out_shape=jax.ShapeDtypeStruct((R, N), F32),
        compiler_params=_cparams(("arbitrary",)),
        name="ada_mod",
    )(c, w_ada, b_ada.reshape(1, N))


def _chunk_masks(C):
    row = lax.broadcasted_iota(jnp.int32, (C, 1), 0)
    ri = lax.broadcasted_iota(jnp.int32, (C, C), 0)
    ci = lax.broadcasted_iota(jnp.int32, (C, C), 1)
    levels = []
    s = SUB
    while s < C:
        same_group = (ri // (2 * s)) == (ci // (2 * s))
        levels.append((s, same_group))
        s *= 2
    diag = (ri // SUB) == (ci // SUB)
    return row, levels, diag


def _bcast_rows(x, group, idx):
    C, K = x.shape
    G = C // group
    x3 = x.reshape(G, group, K)
    return jnp.broadcast_to(x3[:, idx:idx + 1, :], (G, group, K)).reshape(C, K)


def _chunk_head(q, k, la, v, st, sel, masks):
    C, K = q.shape
    row, levels, diag = masks
    rmod = row % SUB

    x3 = la.reshape(C // SUB, SUB, K)
    sub3 = lax.broadcasted_iota(jnp.int32, (1, SUB, 1), 1)
    sh = 1
    while sh < SUB:
        x3 = x3 + jnp.where(sub3 >= sh, pltpu.roll(x3, sh, 1), 0.0)
        sh *= 2
    x = x3.reshape(C, K)
    x_sub = x

    sc = jnp.zeros((C, C), F32)
    for s, same_group in levels:
        G = C // (2 * s)
        x4 = x.reshape(G, 2, s, K)
        xl, xr = x4[:, 0], x4[:, 1]
        yl = jnp.broadcast_to(xl[:, s - 1:s, :], (G, s, K))
        qr = q.reshape(G, 2, s, K)[:, 1] * jnp.exp(xr)
        kl = k.reshape(G, 2, s, K)[:, 0] * jnp.exp(yl - xl)
        zero = jnp.zeros((G, s, K), F32)
        qf = jnp.stack([zero, qr], axis=1).reshape(C, K).astype(BF16)
        kf = jnp.stack([kl, zero], axis=1).reshape(C, K).astype(BF16)
        sc = sc + jnp.where(same_group, _dot_nt(qf, kf), 0.0)
        x = jnp.stack([xl, xr + yl], axis=1).reshape(C, K)
    b = x

    terms = []
    for jj in range(SUB):
        kb = _bcast_rows(k, SUB, jj)
        xb = _bcast_rows(x_sub, SUB, jj)
        e = jnp.where(rmod >= jj, x_sub - xb, NEG_BIG)
        terms.append((q * kb * jnp.exp(e)).astype(BF16))
    d = _dot(jnp.concatenate(terms, axis=1), sel)
    sc = sc + jnp.where(diag, d, 0.0)

    vb = v.astype(BF16)
    o = _dot(sc.astype(BF16), vb) + _dot_nt((q * jnp.exp(b)).astype(BF16), st.astype(BF16))
    b_last = b[C - 1:C, :]
    kd = (k * jnp.exp(b_last - b)).astype(BF16)
    st_new = st * jnp.exp(b_last) + _dot(v.T.astype(BF16), kd)
    return o, st_new


def _recurrence_tile(q_ref, k_ref, la_ref, v_ref, o_ref, st_ref, sel_ref, n_heads, K, V, T):
    C = CHUNK
    masks = _chunk_masks(C)
    sel = sel_ref[...]

    def body(c, carry):
        r0 = pl.multiple_of(c * C, C)
        for h in range(n_heads):
            ks = slice(h * K, (h + 1) * K)
            vs = slice(h * V, (h + 1) * V)
            o, st_new = _chunk_head(q_ref[pl.ds(r0, C), ks], k_ref[pl.ds(r0, C), ks],
                                    la_ref[pl.ds(r0, C), ks], v_ref[pl.ds(r0, C), vs],
                                    st_ref[h], sel, masks)
            o_ref[pl.ds(r0, C), vs] = o
            st_ref[h] = st_new
        return carry

    lax.fori_loop(0, T // C, body, 0)


def _branch_kernel(*refs, kind, n_heads, K, V, T, layer):
    if kind == "gla":
        (x_ref, sh_ref, sc_ref, wq_ref, wk_ref, wv_ref, wg_ref, wgk1_ref, wgk2_ref, bgk_ref,
         nw_ref, wp_ref, sel_ref, y_ref, sout_ref,
         q_s, k_s, la_s, v_s, g_s, o_s, st_s) = refs
    else:
        (x_ref, sh_ref, sc_ref, wq_ref, wk_ref, wv_ref, wg_ref, lb_ref,
         nw_ref, wp_ref, sel_ref, y_ref, sout_ref,
         q_s, k_s, la_s, v_s, g_s, o_s, st_s) = refs
    lt = pl.program_id(1)

    @pl.when(lt == 0)
    def _():
        st_s[...] = jnp.zeros_like(st_s)

    h = (x_ref[0] * (1.0 + sc_ref[0]) + sh_ref[0]).astype(BF16)
    scale = K ** -0.5
    if kind == "gla":
        q_s[...] = _dot(h, wq_ref[...]) * scale
        k_s[...] = _dot(h, wk_ref[...])
        lr = _dot(h, wgk1_ref[...]).astype(BF16)
        la_s[...] = _log_sigmoid(_dot(lr, wgk2_ref[...]) + bgk_ref[...]) * (1.0 / GLA_GATE_NORMALIZER)
    else:
        q_s[...] = _silu(_dot(h, wq_ref[...])) * scale
        lbp = lb_ref[...]
        e = jnp.exp(lbp - jnp.max(lbp, axis=0, keepdims=True))
        lb = jnp.sum(e[:layer + 1], axis=0, keepdims=True) / jnp.sum(e, axis=0, keepdims=True)
        forget = lb + (1.0 - lb) * jax.nn.sigmoid(_dot(h, wk_ref[...]))
        k_s[...] = 1.0 - forget
        la_s[...] = jnp.log(forget)
    v_s[...] = _dot(h, wv_ref[...])
    g_s[...] = _dot(h, wg_ref[...])

    _recurrence_tile(q_s, k_s, la_s, v_s, o_s, st_s, sel_ref, n_heads, K, V, T)

    nw = nw_ref[...]
    outs = []
    for hd in range(n_heads):
        vs = slice(hd * V, (hd + 1) * V)
        o = o_s[:, vs]
        g = g_s[:, vs]
        gate = _silu(g) if kind == "gla" else jax.nn.sigmoid(g)
        o = o * lax.rsqrt(jnp.mean(o * o, axis=-1, keepdims=True) + EPS) * nw * gate
        outs.append(o.astype(BF16))
    y_ref[0] = _dot(jnp.concatenate(outs, axis=1), wp_ref[...])

    @pl.when(lt == pl.num_programs(1) - 1)
    def _():
        for hd in range(n_heads):
            sout_ref[0, hd] = st_s[hd].T


def _sel_matrix(K, C):
    r = jnp.arange(SUB * K, dtype=jnp.int32)[:, None] // K
    c = jnp.arange(C, dtype=jnp.int32)[None, :] % SUB
    return (r == c).astype(BF16)


def _const_spec(shape):
    nd = len(shape)
    return pl.BlockSpec(shape, lambda b, l: (0,) * nd)


class _Cols:
    def __init__(self, arr, start, width):
        assert start % width == 0
        self.arr, self.width, self.index = arr, width, start // width


def _w_array(w):
    return w.arr if isinstance(w, _Cols) else w


def _w_spec(w):
    if isinstance(w, _Cols):
        idx = w.index
        return pl.BlockSpec((w.arr.shape[0], w.width), lambda *g: (0, idx))
    nd = len(w.shape)
    return pl.BlockSpec(w.shape, lambda *g: (0,) * nd)


def _branch_prompt(kind, x, mod3, weights, norm_w, w_proj, n_heads, K, V, layer):
    B, L, D = x.shape
    T = min(TOKEN_TILE, L)
    HK, HV = n_heads * K, n_heads * V
    sel = _sel_matrix(K, CHUNK)
    x_spec = pl.BlockSpec((1, T, D), lambda b, l: (b, l, 0))
    sh_spec = pl.BlockSpec((1, 1, D), lambda b, l: (b, 0, 0))
    sc_spec = pl.BlockSpec((1, 1, D), lambda b, l: (b, 0, 1))
    w_specs = [_w_spec(w) for w in weights]
    nw2 = norm_w.reshape(1, V)
    in_specs = [x_spec, sh_spec, sc_spec] + w_specs + [_const_spec(nw2.shape), _const_spec(w_proj.shape),
                                                       _const_spec(sel.shape)]
    kern = functools.partial(_branch_kernel, kind=kind, n_heads=n_heads, K=K, V=V, T=T, layer=layer)
    return pl.pallas_call(
        kern,
        grid=(B, L // T),
        in_specs=in_specs,
        out_specs=[pl.BlockSpec((1, T, D), lambda b, l: (b, l, 0)),
                   pl.BlockSpec((1, n_heads, K, V), lambda b, l: (b, 0, 0, 0))],
        out_shape=[jax.ShapeDtypeStruct((B, L, D), F32),
                   jax.ShapeDtypeStruct((B, n_heads, K, V), F32)],
        scratch_shapes=[pltpu.VMEM((T, HK), F32), pltpu.VMEM((T, HK), F32), pltpu.VMEM((T, HK), F32),
                        pltpu.VMEM((T, HV), F32), pltpu.VMEM((T, HV), F32), pltpu.VMEM((T, HV), F32),
                        pltpu.VMEM((n_heads, V, K), F32)],
        compiler_params=_cparams(("arbitrary", "arbitrary")),
        name=f"{kind}_prompt",
    )(x, mod3, mod3, *[_w_array(w) for w in weights], nw2, w_proj, sel)


def _sample_kernel(*refs, kind, n_heads, K, V, TB, layer):
    if kind == "gla":
        (x_ref, sh_ref, sc_ref, wq_ref, wk_ref, wv_ref, wg_ref, wgk1_ref, wgk2_ref, bgk_ref,
         nw_ref, wp_ref, s_ref, y_ref, sout_ref, qT_s, kT_s, aT_s, v_s, g_s, o_s) = refs
    else:
        (x_ref, sh_ref, sc_ref, wq_ref, wk_ref, wv_ref, wg_ref, lb_ref,
         nw_ref, wp_ref, s_ref, y_ref, sout_ref, qT_s, kT_s, aT_s, v_s, g_s, o_s) = refs
    step = pl.program_id(0)
    NT = x_ref.shape[0]

    @pl.when(step == 0)
    def _():
        h = (x_ref[...] * (1.0 + sc_ref[...]) + sh_ref[...]).astype(BF16)
        scale = K ** -0.5
        if kind == "gla":
            q = _dot(h, wq_ref[...]) * scale
            k = _dot(h, wk_ref[...])
            lr = _dot(h, wgk1_ref[...]).astype(BF16)
            a = jnp.exp(_log_sigmoid(_dot(lr, wgk2_ref[...]) + bgk_ref[...]) * (1.0 / GLA_GATE_NORMALIZER))
        else:
            q = _silu(_dot(h, wq_ref[...])) * scale
            lbp = lb_ref[...]
            e = jnp.exp(lbp - jnp.max(lbp, axis=0, keepdims=True))
            lb = jnp.sum(e[:layer + 1], axis=0, keepdims=True) / jnp.sum(e, axis=0, keepdims=True)
            a = lb + (1.0 - lb) * jax.nn.sigmoid(_dot(h, wk_ref[...]))
            k = 1.0 - a
        for hd in range(n_heads):
            ks = slice(hd * K, (hd + 1) * K)
            qT_s[ks, :] = q[:, ks].T
            kT_s[ks, :] = k[:, ks].T
            aT_s[ks, :] = a[:, ks].T
        v_s[...] = _dot(h, wv_ref[...])
        g_s[...] = _dot(h, wg_ref[...])

    lane = lax.broadcasted_iota(jnp.int32, (1, NT), 1)
    sub = lax.broadcasted_iota(jnp.int32, (TB, 1), 0)
    t0 = pl.multiple_of(step * TB, TB)
    for hd in range(n_heads):
        ks = slice(hd * K, (hd + 1) * K)
        vs = slice(hd * V, (hd + 1) * V)
        v_rows = v_s[pl.ds(t0, TB), vs]
        o_rows = jnp.zeros((TB, V), F32)
        for j in range(TB):
            pick = lane == t0 + j
            acol = jnp.sum(jnp.where(pick, aT_s[ks, :], 0.0), axis=1, keepdims=True)
            kcol = jnp.sum(jnp.where(pick, kT_s[ks, :], 0.0), axis=1, keepdims=True)
            qcol = jnp.sum(jnp.where(pick, qT_s[ks, :], 0.0), axis=1, keepdims=True)
            s1 = acol * s_ref[j, hd] + kcol * v_rows[j:j + 1, :]
            sout_ref[j, hd] = s1
            o_rows = jnp.where(sub == j, jnp.sum(qcol * s1, axis=0, keepdims=True), o_rows)
        o_s[pl.ds(t0, TB), vs] = o_rows

    @pl.when(step == pl.num_programs(0) - 1)
    def _():
        nw = nw_ref[...]
        outs = []
        for hd in range(n_heads):
            vs = slice(hd * V, (hd + 1) * V)
            o = o_s[:, vs]
            g = g_s[:, vs]
            gate = _silu(g) if kind == "gla" else jax.nn.sigmoid(g)
            o = o * lax.rsqrt(jnp.mean(o * o, axis=-1, keepdims=True) + EPS) * nw * gate
            outs.append(o.astype(BF16))
        y_ref[...] = _dot(jnp.concatenate(outs, axis=1), wp_ref[...])


def _branch_sample(kind, x, mod, weights, norm_w, w_proj, state, layer):
    NT, D = x.shape
    _, n_heads, K, V = state.shape
    HK, HV = n_heads * K, n_heads * V
    TB = SUBLANES
    c1 = lambda s: pl.BlockSpec(s, lambda i: (0,) * len(s))
    nw2 = norm_w.reshape(1, V)
    in_specs = ([c1((NT, D)), pl.BlockSpec((NT, D), lambda i: (0, 0)), pl.BlockSpec((NT, D), lambda i: (0, 1))]
                + [_w_spec(w) for w in weights] + [c1(nw2.shape), c1(w_proj.shape),
                                                    pl.BlockSpec((TB, n_heads, K, V), lambda i: (i, 0, 0, 0))])
    kern = functools.partial(_sample_kernel, kind=kind, n_heads=n_heads, K=K, V=V, TB=TB, layer=layer)
    return pl.pallas_call(
        kern,
        grid=(NT // TB,),
        in_specs=in_specs,
        out_specs=[c1((NT, D)), pl.BlockSpec((TB, n_heads, K, V), lambda i: (i, 0, 0, 0))],
        out_shape=[jax.ShapeDtypeStruct((NT, D), F32), jax.ShapeDtypeStruct(state.shape, F32)],
        scratch_shapes=[pltpu.VMEM((HK, NT), F32), pltpu.VMEM((HK, NT), F32), pltpu.VMEM((HK, NT), F32),
                        pltpu.VMEM((NT, HV), F32), pltpu.VMEM((NT, HV), F32), pltpu.VMEM((NT, HV), F32)],
        compiler_params=_cparams(("arbitrary",)),
        name=f"{kind}_sample",
    )(x, mod, mod, *[_w_array(w) for w in weights], nw2, w_proj, state)


def _merge_kernel(x_ref, ya_ref, yb_ref, sh_ref, sc_ref, g_ref, wu_ref, wo_ref, lg_ref, lb_ref, o_ref, *, alpha):
    x = x_ref[0]
    D = x.shape[-1]
    h = (x * (1.0 + sc_ref[0]) + sh_ref[0]).astype(BF16)
    u = _dot(h, wu_ref[...])
    merged = jax.nn.sigmoid(u[:, :D]) * ya_ref[0] + jax.nn.sigmoid(u[:, D:]) * yb_ref[0]
    mix = _dot(merged.astype(BF16), wo_ref[...])
    o_ref[0] = _layernorm(alpha * x + g_ref[0] * mix, lg_ref[...], lb_ref[...])


def _mod_specs(mod3, cols, T):
    D = mod3.shape[-1] // 6
    if mod3.shape[1] == 1:
        return [pl.BlockSpec((1, 1, D), functools.partial(lambda b, l, *_, c: (b, 0, c), c=c)) for c in cols]
    return [pl.BlockSpec((1, T, D), functools.partial(lambda b, l, *_, c: (b, l, c), c=c)) for c in cols]


def _merge(x, ya, yb, mod3, wu, w_out, ln_g, ln_b, alpha):
    B, L, D = x.shape
    T = min(TOKEN_TILE, L)
    tok = pl.BlockSpec((1, T, D), lambda b, l: (b, l, 0))
    return pl.pallas_call(
        functools.partial(_merge_kernel, alpha=alpha),
        grid=(B, L // T),
        in_specs=[tok, tok, tok] + _mod_specs(mod3, (0, 1, 2), T)
        + [_w_spec(wu), _const_spec(w_out.shape), _const_spec((1, D)), _const_spec((1, D))],
        out_specs=tok,
        out_shape=jax.ShapeDtypeStruct((B, L, D), F32),
        compiler_params=_cparams(("arbitrary", "arbitrary")),
        name="merge",
    )(x, ya, yb, mod3, mod3, mod3, _w_array(wu), w_out, ln_g.reshape(1, D), ln_b.reshape(1, D))


def _first_argmax(vals, iota, n, axis):
    m = jnp.max(vals, axis=axis, keepdims=True)
    idx = jnp.min(jnp.where(vals == m, iota, n), axis=axis, keepdims=True)
    return m, idx


def _router_kernel(x_ref, sh_ref, sc_ref, wrT_ref, bias_ref, eidx_ref, egate_ref, xg_ref, *, n_experts):
    E = n_experts
    per = E // N_GROUPS
    hf = x_ref[0] * (1.0 + sc_ref[0]) + sh_ref[0]
    h = hf.astype(BF16)
    T = h.shape[0]
    for s in range(hf.shape[1] // LANES):
        xg_ref[pl.ds(s, T, stride=SUBLANES), :] = hf[:, s * LANES:(s + 1) * LANES]
    scores = jax.nn.sigmoid(_dot_nt(wrT_ref[...], h))
    biased = scores + bias_ref[...]
    b3 = biased.reshape(N_GROUPS, per, T)
    i3 = lax.broadcasted_iota(jnp.int32, (N_GROUPS, per, T), 1)
    m1, a1 = _first_argmax(b3, i3, per, 1)
    m2 = jnp.max(jnp.where(i3 == a1, -jnp.inf, b3), axis=1, keepdims=True)
    gscore = (m1 + m2).reshape(N_GROUPS, T)
    gi = lax.broadcasted_iota(jnp.int32, (N_GROUPS, T), 0)
    gsel = jnp.zeros((N_GROUPS, T), jnp.bool_)
    for _ in range(TOPK_GROUPS):
        _, a = _first_argmax(gscore, gi, N_GROUPS, 0)
        hit = gi == a
        gsel = jnp.logical_or(gsel, hit)
        gscore = jnp.where(hit, -jnp.inf, gscore)
    emask = jnp.broadcast_to(gsel.reshape(N_GROUPS, 1, T), (N_GROUPS, per, T)).reshape(E, T)
    cand = jnp.where(emask, biased, -jnp.inf)
    ei = lax.broadcasted_iota(jnp.int32, (E, T), 0)
    picks, weights = [], []
    for _ in range(TOP_K):
        _, a = _first_argmax(cand, ei, E, 0)
        hit = ei == a
        picks.append(a)
        weights.append(jnp.sum(jnp.where(hit, scores, 0.0), axis=0, keepdims=True))
        cand = jnp.where(hit, -jnp.inf, cand)
    w = jnp.concatenate(weights, axis=0)
    egate_ref[...] = w / jnp.sum(w, axis=0, keepdims=True) * ROUTED_SCALE
    eidx_ref[...] = jnp.concatenate(picks, axis=0)


def _router(x1, mod3, wrT, bias):
    B, L, D = x1.shape
    assert D == SUBLANES * LANES
    E = wrT.shape[0]
    T = min(TOKEN_TILE, L)
    nl = L // T
    N = B * L
    tok = pl.BlockSpec((1, T, D), lambda b, l: (b, l, 0))
    pick_spec = pl.BlockSpec((TOP_K, T), lambda b, l: (0, b * nl + l))
    return pl.pallas_call(
        functools.partial(_router_kernel, n_experts=E),
        grid=(B, nl),
        in_specs=[tok] + _mod_specs(mod3, (3, 4), T) + [_const_spec(wrT.shape), _const_spec((E, 1))],
        out_specs=[pick_spec, pick_spec, pl.BlockSpec((T * SUBLANES, LANES), lambda b, l: (b * nl + l, 0))],
        out_shape=[jax.ShapeDtypeStruct((TOP_K, N), jnp.int32), jax.ShapeDtypeStruct((TOP_K, N), F32),
                   jax.ShapeDtypeStruct((N * SUBLANES, LANES), F32)],
        compiler_params=_cparams(("arbitrary", "arbitrary")),
        name="router",
    )(x1, mod3, mod3, wrT, bias.reshape(E, 1))


def _tile_schedule(eidx, egate, NB, E, R):
    Kp, N = eidx.shape
    nb = N // NB
    A = Kp * NB
    S = SUBLANES
    assert A % R == 0
    NW = A // R
    tok = jnp.broadcast_to(jnp.arange(N, dtype=jnp.int32)[None, :], (Kp, N))
    assert nb * E * NB < 2 ** 31
    key = ((tok // NB) * E + eidx) * NB + tok % NB
    skey, sgate = lax.sort((key.reshape(-1), egate.reshape(-1)), num_keys=1)
    w_e = ((skey // NB) % E).reshape(nb, NW, R)
    w_t = ((skey % NB) * S).reshape(nb, NW, R)
    w_g = sgate.reshape(nb, NW, R)
    first = w_e[:, :, 0]
    npair = w_e[:, :, R - 1] - first + 1
    cum = jnp.cumsum(npair, axis=1)
    ntiles = cum[:, -1]
    SL = NW + E + 3
    q = jnp.arange(SL, dtype=jnp.int32)[None, :] - 2
    qc = jnp.clip(q, 0, ntiles[:, None] - 1)
    k_q = jnp.minimum(jnp.sum((cum[:, None, :] <= qc[:, :, None]).astype(jnp.int32), axis=2), NW - 1)
    onehot = (k_q[:, :, None] == jnp.arange(NW, dtype=jnp.int32)[None, None, :]).astype(F32)
    sel = lambda a: jnp.einsum('bsk,bkr->bsr', onehot, a.astype(F32), precision=lax.Precision.HIGHEST)
    selk = lambda a: jnp.sum(onehot * a.astype(F32)[:, None, :], axis=2).astype(jnp.int32)
    e_q = jnp.clip(selk(first) + qc - selk(cum - npair), 0, E - 1)
    real = jnp.logical_and(q >= 0, q < ntiles[:, None])
    match = jnp.logical_and(real[:, :, None], sel(w_e).astype(jnp.int32) == e_q[:, :, None])
    rows = jnp.where(match, sel(w_t).astype(jnp.int32), NB * S).reshape(nb * SL, 1, R)
    gate = jnp.where(match, sel(w_g), 0.0).reshape(nb * SL, 1, R)
    return e_q.reshape(-1), ntiles, rows, gate, SL


def _moe_step(src_ref, dst_ref, gate_ref, xg_s, acc_s, wg_ref, wu_ref, wd_ref, gbuf, cbuf, cy, sy, R):
    S = SUBLANES
    for r in range(R):
        t0 = pl.multiple_of(src_ref[0, 0, r], S)
        gbuf[r * S:(r + 1) * S, :] = xg_s[pl.ds(t0, S), :]

    x = jnp.concatenate([cbuf[pl.ds(s, R, stride=S), :] for s in range(S)], axis=1).astype(BF16)
    a = _silu(_dot(x, wg_ref[0])) * _dot(x, wu_ref[0])
    y = _dot(a.astype(BF16), wd_ref[0])
    for s in range(S):
        cy[pl.ds(s, R, stride=S), :] = y[:, s * LANES:(s + 1) * LANES]

    for r0 in range(0, R, RMW_BATCH):
        upd = []
        for r in range(r0, r0 + RMW_BATCH):
            a0 = pl.multiple_of(dst_ref[0, 0, r], S)
            upd.append((a0, acc_s[pl.ds(a0, S), :] + gate_ref[0, 0, r] * sy[r * S:(r + 1) * S, :]))
        for a0, val in upd:
            acc_s[pl.ds(a0, S), :] = val


def _moe_kernel(te_ref, nt_ref, src_ref, dst_ref, gate_ref, xg_hbm, wg_ref, wu_ref, wd_ref, out_hbm,
                xg_s, acc_s, buf0, buf1, y0, y1, *, NB, R):
    del te_ref
    b = pl.program_id(0)
    q = pl.program_id(1)

    @pl.when(jnp.logical_and(b == 0, q == 0))
    def _():
        for ref in (buf0, buf1, y0, y1):
            ref[...] = jnp.zeros_like(ref)

    @pl.when(q == 0)
    def _():
        pltpu.sync_copy(xg_hbm.at[b], xg_s.at[pl.ds(0, NB * SUBLANES)])
        xg_s[pl.ds(NB * SUBLANES, SUBLANES), :] = jnp.zeros((SUBLANES, LANES), F32)
        acc_s[...] = jnp.zeros_like(acc_s)

    active = q < nt_ref[b] + 2
    args = (src_ref, dst_ref, gate_ref, xg_s, acc_s, wg_ref, wu_ref, wd_ref)

    @pl.when(jnp.logical_and(active, q % 2 == 0))
    def _():
        _moe_step(*args, buf0, buf1, y1, y0, R)

    @pl.when(jnp.logical_and(active, q % 2 == 1))
    def _():
        _moe_step(*args, buf1, buf0, y0, y1, R)

    @pl.when(q == pl.num_programs(1) - 1)
    def _():
        pltpu.sync_copy(acc_s.at[pl.ds(0, NB * SUBLANES)], out_hbm.at[b])


def _moe_routed(xg, eidx, egate, wg, wu, wd):
    N = eidx.shape[1]
    E, D, DE = wg.shape
    NB = min(MOE_BLOCK, N)
    nb = N // NB
    S = SUBLANES
    R = MOE_ROWS if NB >= MOE_BLOCK else MOE_ROWS_SMALL
    te, ntiles, rows, gate, SL = _tile_schedule(eidx, egate, NB, E, R)
    smem = lambda shift: pl.BlockSpec((1, 1, R), lambda b, q, *_: (b * SL + q + shift, 0, 0),
                                      memory_space=pltpu.SMEM)
    w_map = lambda b, q, te_ref, nt_ref: (te_ref[b * SL + q + 1], 0, 0)
    tile_rows = pltpu.VMEM((R * S, LANES), F32)
    grid_spec = pltpu.PrefetchScalarGridSpec(
        num_scalar_prefetch=2,
        grid=(nb, SL - 2),
        in_specs=[smem(2), smem(0), smem(0),
                  pl.BlockSpec(memory_space=pl.ANY),
                  pl.BlockSpec((1, D, DE), w_map), pl.BlockSpec((1, D, DE), w_map), pl.BlockSpec((1, DE, D), w_map)],
        out_specs=pl.BlockSpec(memory_space=pl.ANY),
        scratch_shapes=[pltpu.VMEM(((NB + 1) * S, LANES), F32), pltpu.VMEM(((NB + 1) * S, LANES), F32),
                        tile_rows, tile_rows, tile_rows, tile_rows],
    )
    return pl.pallas_call(
        functools.partial(_moe_kernel, NB=NB, R=R),
        grid_spec=grid_spec,
        out_shape=jax.ShapeDtypeStruct((nb, NB * S, LANES), F32),
        compiler_params=_cparams(("arbitrary", "arbitrary")),
        name="moe_routed",
    )(te, ntiles, rows, rows, gate, xg.reshape(nb, NB * S, LANES), wg, wu, wd)


def _combine_kernel(x_ref, sh_ref, sc_ref, g2_ref, r_ref, sg_ref, su_ref, sd_ref, lg_ref, lb_ref, o_ref, *, alpha):
    x = x_ref[0]
    T = x.shape[0]
    h = (x * (1.0 + sc_ref[0]) + sh_ref[0]).astype(BF16)
    a = _silu(_dot(h, sg_ref[...])) * _dot(h, su_ref[...])
    shared = _dot(a.astype(BF16), sd_ref[...])
    routed = jnp.concatenate([r_ref[0, pl.ds(s, T, stride=SUBLANES), :] for s in range(SUBLANES)], axis=1)
    o_ref[0] = _layernorm(alpha * x + g2_ref[0] * (routed + shared), lg_ref[...], lb_ref[...])


def _combine(x1, mod3, routed, sg, su, sd, ln_g, ln_b, alpha):
    B, L, D = x1.shape
    T = min(TOKEN_TILE, L)
    nl = L // T
    NB = routed.shape[1] // SUBLANES
    per = NB // T
    tok = pl.BlockSpec((1, T, D), lambda b, l: (b, l, 0))
    r_spec = pl.BlockSpec((1, T * SUBLANES, LANES), lambda b, l: ((b * nl + l) // per, (b * nl + l) % per, 0))
    return pl.pallas_call(
        functools.partial(_combine_kernel, alpha=alpha),
        grid=(B, nl),
        in_specs=[tok] + _mod_specs(mod3, (3, 4, 5), T) + [r_spec]
        + [_const_spec(sg.shape), _const_spec(su.shape), _const_spec(sd.shape), _const_spec((1, D)),
           _const_spec((1, D))],
        out_specs=tok,
        out_shape=jax.ShapeDtypeStruct((B, L, D), F32),
        compiler_params=_cparams(("arbitrary", "arbitrary")),
        name="combine",
    )(x1, mod3, mod3, mod3, routed, sg, su, sd, ln_g.reshape(1, D), ln_b.reshape(1, D))


def _split_w_in(w_in_l, gla_shape, hgrn_shape, D):
    Hg, Kg, Vg = gla_shape
    Hh, Kh, Vh = hgrn_shape
    rank = w_in_l.shape[1] - (2 * Hg * Kg + 2 * Hg * Vg + 2 * Hh * Kh + 2 * Hh * Vh + 2 * D)
    gla_w = (Hg * Kg, Hg * Kg, Hg * Vg, Hg * Vg)
    hgrn_w = (Hh * Kh, Hh * Kh, Hh * Vh, Hh * Vh, 2 * D)
    n_lo = sum(gla_w)
    lo = w_in_l[:, :n_lo].astype(BF16)
    mid = w_in_l[:, n_lo:n_lo + rank].astype(BF16)
    hi = w_in_l[:, n_lo + rank:].astype(BF16)
    out = []
    for arr, widths in ((lo, gla_w), (hi, hgrn_w)):
        start = 0
        for w in widths:
            out.append(_Cols(arr, start, w))
            start += w
    return out[:4] + [mid] + out[4:]


def kernel(x_prompt, x_sample, state_gla, state_hgrn, c_prompt, c_sample, w_ada, b_ada, w_in, w_gk2, b_gk,
           hgrn_lb, gla_norm_w, hgrn_norm_w, w_proj_a, w_proj_b, w_out, ln1_g, ln1_b, w_router, router_bias,
           w_exp_gate, w_exp_up, w_exp_down, w_sh_gate, w_sh_up, w_sh_down, ln2_g, ln2_b):
    depth = w_in.shape[0]
    BP, L, D = x_prompt.shape
    NS = x_sample.shape[0]
    assert x_sample.shape[1] == 1
    gla_shape = state_gla.shape[2:]
    hgrn_shape = state_hgrn.shape[2:]
    alpha = (2.0 * depth) ** 0.25

    xp = x_prompt
    xs = x_sample.reshape(NS, D)
    c_all = jnp.concatenate([c_prompt, c_sample], axis=0)
    new_gla_p, new_hgrn_p, new_gla_s, new_hgrn_s = [], [], [], []
    for l in range(depth):
        mod = _ada_mod(c_all, w_ada[l], b_ada[l])
        mod_p = mod[:BP].reshape(BP, 1, 6 * D)
        mod_s = mod[BP:]
        (wqa, wka, wva, wga, wgk1, wqb, wfb, wib, wgb, wuab) = _split_w_in(w_in[l], gla_shape, hgrn_shape, D)
        gla_w = [wqa, wka, wva, wga, wgk1, w_gk2[l].astype(BF16), b_gk[l].reshape(1, -1)]
        hgrn_w = [wqb, wfb, wib, wgb, hgrn_lb]
        wpa = w_proj_a[l].astype(BF16)
        wpb = w_proj_b[l].astype(BF16)
        wo = w_out[l].astype(BF16)
        wrT = w_router[l].T.astype(BF16)
        eg, eu, ed = w_exp_gate[l].astype(BF16), w_exp_up[l].astype(BF16), w_exp_down[l].astype(BF16)
        sg, su, sd = w_sh_gate[l].astype(BF16), w_sh_up[l].astype(BF16), w_sh_down[l].astype(BF16)

        def tail(x3, ya, yb, mod3):
            x1 = _merge(x3, ya, yb, mod3, wuab, wo, ln1_g[l], ln1_b[l], alpha)
            eidx, egate, xg = _router(x1, mod3, wrT, router_bias[l])
            routed = _moe_routed(xg, eidx, egate, eg, eu, ed)
            return _combine(x1, mod3, routed, sg, su, sd, ln2_g[l], ln2_b[l], alpha)

        ya, sg_p = _branch_prompt("gla", xp, mod_p, gla_w, gla_norm_w[l], wpa, *gla_shape, layer=l)
        yb, sh_p = _branch_prompt("hgrn", xp, mod_p, hgrn_w, hgrn_norm_w[l], wpb, *hgrn_shape, layer=l)
        xp = tail(xp, ya, yb, mod_p)
        new_gla_p.append(sg_p)
        new_hgrn_p.append(sh_p)

        ya, sg_s = _branch_sample("gla", xs, mod_s, gla_w, gla_norm_w[l], wpa, state_gla[l], layer=l)
        yb, sh_s = _branch_sample("hgrn", xs, mod_s, hgrn_w, hgrn_norm_w[l], wpb, state_hgrn[l], layer=l)
        xs = tail(xs[None], ya[None], yb[None], mod_s[None])[0]
        new_gla_s.append(sg_s)
        new_hgrn_s.append(sh_s)

    return (xp, xs.reshape(NS, 1, D), jnp.stack(new_gla_p), jnp.stack(new_hgrn_p),
            jnp.stack(new_gla_s), jnp.stack(new_hgrn_s))
```

```python
import functools

import jax
import jax.numpy as jnp
from jax import lax
from jax.experimental import pallas as pl
from jax.experimental.pallas import tpu as pltpu

F32 = jnp.float32
BF16 = jnp.bfloat16

GLA_GATE_NORMALIZER = 16.0
N_GROUPS = 8
TOPK_GROUPS = 4
TOP_K = 8
ROUTED_SCALE = 2.5
EPS = 1e-5

SUBLANES = 8
LANES = 128
VMEM_LIMIT_BYTES = 56 * 1024 * 1024

TOKEN_TILE = 512
MOE_BLOCK = 4096
MOE_ROWS = 256
MOE_ROWS_SMALL = 64
RMW_BATCH = 8
CHUNK = 128
SUB = SUBLANES
NEG_BIG = -1e30


def _cparams(sem):
    return pltpu.CompilerParams(dimension_semantics=sem, vmem_limit_bytes=VMEM_LIMIT_BYTES)


def _dot(a, b):
    return jnp.dot(a, b, preferred_element_type=F32)


def _dot_nt(a, b):
    return lax.dot_general(a, b, (((1,), (1,)), ((), ())), preferred_element_type=F32)


def _silu(x):
    return x * jax.nn.sigmoid(x)


def _log_sigmoid(x):
    return jnp.minimum(x, 0.0) - jnp.log1p(jnp.exp(-jnp.abs(x)))


def _layernorm(r, g, b):
    mu = jnp.mean(r, axis=-1, keepdims=True)
    d = r - mu
    var = jnp.mean(d * d, axis=-1, keepdims=True)
    return d * lax.rsqrt(var + EPS) * g + b


def _ada_kernel(c_ref, w_ref, b_ref, o_ref):
    c = c_ref[...]
    o_ref[...] = _dot(_silu(c).astype(BF16), w_ref[...].astype(BF16)) + b_ref[...]


def _ada_mod(c, w_ada, b_ada):
    R, D = c.shape
    N = w_ada.shape[1]
    tn = D
    return pl.pallas_call(
        _ada_kernel,
        grid=(N // tn,),
        in_specs=[pl.BlockSpec((R, D), lambda j: (0, 0)),
                  pl.BlockSpec((D, tn), lambda j: (0, j)),
                  pl.BlockSpec((1, tn), lambda j: (0, j))],
        out_specs=pl.BlockSpec((R, tn), lambda j: (0, j)),
        out_shape=jax.ShapeDtypeStruct((R, N), F32),
        compiler_params=_cparams(("arbitrary",)),
        name="ada_mod",
    )(c, w_ada, b_ada.reshape(1, N))


def _chunk_masks(C):
    row = lax.broadcasted_iota(jnp.int32, (C, 1), 0)
    ri = lax.broadcasted_iota(jnp.int32, (C, C), 0)
    ci = lax.broadcasted_iota(jnp.int32, (C, C), 1)
    levels = []
    s = SUB
    while s < C:
        same_group = (ri // (2 * s)) == (ci // (2 * s))
        levels.append((s, same_group))
        s *= 2
    diag = (ri // SUB) == (ci // SUB)
    return row, levels, diag


def _bcast_rows(x, group, idx):
    C, K = x.shape
    G = C // group
    x3 = x.reshape(G, group, K)
    return jnp.broadcast_to(x3[:, idx:idx + 1, :], (G, group, K)).reshape(C, K)


def _chunk_head(q, k, la, v, st, sel, masks):
    C, K = q.shape
    row, levels, diag = masks
    rmod = row % SUB

    x3 = la.reshape(C // SUB, SUB, K)
    sub3 = lax.broadcasted_iota(jnp.int32, (1, SUB, 1), 1)
    sh = 1
    while sh < SUB:
        x3 = x3 + jnp.where(sub3 >= sh, pltpu.roll(x3, sh, 1), 0.0)
        sh *= 2
    x = x3.reshape(C, K)
    x_sub = x

    sc = jnp.zeros((C, C), F32)
    for s, same_group in levels:
        G = C // (2 * s)
        x4 = x.reshape(G, 2, s, K)
        xl, xr = x4[:, 0], x4[:, 1]
        yl = jnp.broadcast_to(xl[:, s - 1:s, :], (G, s, K))
        qr = q.reshape(G, 2, s, K)[:, 1] * jnp.exp(xr)
        kl = k.reshape(G, 2, s, K)[:, 0] * jnp.exp(yl - xl)
        zero = jnp.zeros((G, s, K), F32)
        qf = jnp.stack([zero, qr], axis=1).reshape(C, K).astype(BF16)
        kf = jnp.stack([kl, zero], axis=1).reshape(C, K).astype(BF16)
        sc = sc + jnp.where(same_group, _dot_nt(qf, kf), 0.0)
        x = jnp.stack([xl, xr + yl], axis=1).reshape(C, K)
    b = x

    terms = []
    for jj in range(SUB):
        kb = _bcast_rows(k, SUB, jj)
        xb = _bcast_rows(x_sub, SUB, jj)
        e = jnp.where(rmod >= jj, x_sub - xb, NEG_BIG)
        terms.append((q * kb * jnp.exp(e)).astype(BF16))
    d = _dot(jnp.concatenate(terms, axis=1), sel)
    sc = sc + jnp.where(diag, d, 0.0)

    vb = v.astype(BF16)
    o = _dot(sc.astype(BF16), vb) + _dot_nt((q * jnp.exp(b)).astype(BF16), st.astype(BF16))
    b_last = b[C - 1:C, :]
    kd = (k * jnp.exp(b_last - b)).astype(BF16)
    st_new = st * jnp.exp(b_last) + _dot(v.T.astype(BF16), kd)
    return o, st_new


def _recurrence_tile(q_ref, k_ref, la_ref, v_ref, o_ref, st_ref, sel_ref, n_heads, K, V, T):
    C = CHUNK
    masks = _chunk_masks(C)
    sel = sel_ref[...]

    def body(c, carry):
        r0 = pl.multiple_of(c * C, C)
        for h in range(n_heads):
            ks = slice(h * K, (h + 1) * K)
            vs = slice(h * V, (h + 1) * V)
            o, st_new = _chunk_head(q_ref[pl.ds(r0, C), ks], k_ref[pl.ds(r0, C), ks],
                                    la_ref[pl.ds(r0, C), ks], v_ref[pl.ds(r0, C), vs],
                                    st_ref[h], sel, masks)
            o_ref[pl.ds(r0, C), vs] = o
            st_ref[h] = st_new
        return carry

    lax.fori_loop(0, T // C, body, 0)


def _branch_kernel(*refs, kind, n_heads, K, V, T, layer):
    if kind == "gla":
        (x_ref, sh_ref, sc_ref, wq_ref, wk_ref, wv_ref, wg_ref, wgk1_ref, wgk2_ref, bgk_ref,
         nw_ref, wp_ref, sel_ref, y_ref, sout_ref,
         q_s, k_s, la_s, v_s, g_s, o_s, st_s) = refs
    else:
        (x_ref, sh_ref, sc_ref, wq_ref, wk_ref, wv_ref, wg_ref, lb_ref,
         nw_ref, wp_ref, sel_ref, y_ref, sout_ref,
         q_s, k_s, la_s, v_s, g_s, o_s, st_s) = refs
    lt = pl.program_id(1)

    @pl.when(lt == 0)
    def _():
        st_s[...] = jnp.zeros_like(st_s)

    h = (x_ref[0] * (1.0 + sc_ref[0]) + sh_ref[0]).astype(BF16)
    scale = K ** -0.5
    if kind == "gla":
        q_s[...] = _dot(h, wq_ref[...]) * scale
        k_s[...] = _dot(h, wk_ref[...])
        lr = _dot(h, wgk1_ref[...]).astype(BF16)
        la_s[...] = _log_sigmoid(_dot(lr, wgk2_ref[...]) + bgk_ref[...]) * (1.0 / GLA_GATE_NORMALIZER)
    else:
        q_s[...] = _silu(_dot(h, wq_ref[...])) * scale
        lbp = lb_ref[...]
        e = jnp.exp(lbp - jnp.max(lbp, axis=0, keepdims=True))
        lb = jnp.sum(e[:layer + 1], axis=0, keepdims=True) / jnp.sum(e, axis=0, keepdims=True)
        forget = lb + (1.0 - lb) * jax.nn.sigmoid(_dot(h, wk_ref[...]))
        k_s[...] = 1.0 - forget
        la_s[...] = jnp.log(forget)
    v_s[...] = _dot(h, wv_ref[...])
    g_s[...] = _dot(h, wg_ref[...])

    _recurrence_tile(q_s, k_s, la_s, v_s, o_s, st_s, sel_ref, n_heads, K, V, T)

    nw = nw_ref[...]
    outs = []
    for hd in range(n_heads):
        vs = slice(hd * V, (hd + 1) * V)
        o = o_s[:, vs]
        g = g_s[:, vs]
        gate = _silu(g) if kind == "gla" else jax.nn.sigmoid(g)
        o = o * lax.rsqrt(jnp.mean(o * o, axis=-1, keepdims=True) + EPS) * nw * gate
        outs.append(o.astype(BF16))
    y_ref[0] = _dot(jnp.concatenate(outs, axis=1), wp_ref[...])

    @pl.when(lt == pl.num_programs(1) - 1)
    def _():
        for hd in range(n_heads):
            sout_ref[0, hd] = st_s[hd].T


def _sel_matrix(K, C):
    r = jnp.arange(SUB * K, dtype=jnp.int32)[:, None] // K
    c = jnp.arange(C, dtype=jnp.int32)[None, :] % SUB
    return (r == c).astype(BF16)


def _const_spec(shape):
    nd = len(shape)
    return pl.BlockSpec(shape, lambda b, l: (0,) * nd)


class _Cols:
    def __init__(self, arr, start, width):
        assert start % width == 0
        self.arr, self.width, self.index = arr, width, start // width


def _w_array(w):
    return w.arr if isinstance(w, _Cols) else w


def _w_spec(w):
    if isinstance(w, _Cols):
        idx = w.index
        return pl.BlockSpec((w.arr.shape[0], w.width), lambda *g: (0, idx))
    nd = len(w.shape)
    return pl.BlockSpec(w.shape, lambda *g: (0,) * nd)


def _branch_prompt(kind, x, mod3, weights, norm_w, w_proj, n_heads, K, V, layer):
    B, L, D = x.shape
    T = min(TOKEN_TILE, L)
    HK, HV = n_heads * K, n_heads * V
    sel = _sel_matrix(K, CHUNK)
    x_spec = pl.BlockSpec((1, T, D), lambda b, l: (b, l, 0))
    sh_spec = pl.BlockSpec((1, 1, D), lambda b, l: (b, 0, 0))
    sc_spec = pl.BlockSpec((1, 1, D), lambda b, l: (b, 0, 1))
    w_specs = [_w_spec(w) for w in weights]
    nw2 = norm_w.reshape(1, V)
    in_specs = [x_spec, sh_spec, sc_spec] + w_specs + [_const_spec(nw2.shape), _const_spec(w_proj.shape),
                                                       _const_spec(sel.shape)]
    kern = functools.partial(_branch_kernel, kind=kind, n_heads=n_heads, K=K, V=V, T=T, layer=layer)
    return pl.pallas_call(
        kern,
        grid=(B, L // T),
        in_specs=in_specs,
        out_specs=[pl.BlockSpec((1, T, D), lambda b, l: (b, l, 0)),
                   pl.BlockSpec((1, n_heads, K, V), lambda b, l: (b, 0, 0, 0))],
        out_shape=[jax.ShapeDtypeStruct((B, L, D), F32),
                   jax.ShapeDtypeStruct((B, n_heads, K, V), F32)],
        scratch_shapes=[pltpu.VMEM((T, HK), F32), pltpu.VMEM((T, HK), F32), pltpu.VMEM((T, HK), F32),
                        pltpu.VMEM((T, HV), F32), pltpu.VMEM((T, HV), F32), pltpu.VMEM((T, HV), F32),
                        pltpu.VMEM((n_heads, V, K), F32)],
        compiler_params=_cparams(("arbitrary", "arbitrary")),
        name=f"{kind}_prompt",
    )(x, mod3, mod3, *[_w_array(w) for w in weights], nw2, w_proj, sel)


def _sample_kernel(*refs, kind, n_heads, K, V, TB, layer):
    if kind == "gla":
        (x_ref, sh_ref, sc_ref, wq_ref, wk_ref, wv_ref, wg_ref, wgk1_ref, wgk2_ref, bgk_ref,
         nw_ref, wp_ref, s_ref, y_ref, sout_ref, qT_s, kT_s, aT_s, v_s, g_s, o_s) = refs
    else:
        (x_ref, sh_ref, sc_ref, wq_ref, wk_ref, wv_ref, wg_ref, lb_ref,
         nw_ref, wp_ref, s_ref, y_ref, sout_ref, qT_s, kT_s, aT_s, v_s, g_s, o_s) = refs
    step = pl.program_id(0)
    NT = x_ref.shape[0]

    @pl.when(step == 0)
    def _():
        h = (x_ref[...] * (1.0 + sc_ref[...]) + sh_ref[...]).astype(BF16)
        scale = K ** -0.5
        if kind == "gla":
            q = _dot(h, wq_ref[...]) * scale
            k = _dot(h, wk_ref[...])
            lr = _dot(h, wgk1_ref[...]).astype(BF16)
            a = jnp.exp(_log_sigmoid(_dot(lr, wgk2_ref[...]) + bgk_ref[...]) * (1.0 / GLA_GATE_NORMALIZER))
        else:
            q = _silu(_dot(h, wq_ref[...])) * scale
            lbp = lb_ref[...]
            e = jnp.exp(lbp - jnp.max(lbp, axis=0, keepdims=True))
            lb = jnp.sum(e[:layer + 1], axis=0, keepdims=True) / jnp.sum(e, axis=0, keepdims=True)
            a = lb + (1.0 - lb) * jax.nn.sigmoid(_dot(h, wk_ref[...]))
            k = 1.0 - a
        for hd in range(n_heads):
            ks = slice(hd * K, (hd + 1) * K)
            qT_s[ks, :] = q[:, ks].T
            kT_s[ks, :] = k[:, ks].T
            aT_s[ks, :] = a[:, ks].T
        v_s[...] = _dot(h, wv_ref[...])
        g_s[...] = _dot(h, wg_ref[...])

    lane = lax.broadcasted_iota(jnp.int32, (1, NT), 1)
    sub = lax.broadcasted_iota(jnp.int32, (TB, 1), 0)
    t0 = pl.multiple_of(step * TB, TB)
    for hd in range(n_heads):
        ks = slice(hd * K, (hd + 1) * K)
        vs = slice(hd * V, (hd + 1) * V)
        v_rows = v_s[pl.ds(t0, TB), vs]
        o_rows = jnp.zeros((TB, V), F32)
        for j in range(TB):
            pick = lane == t0 + j
            acol = jnp.sum(jnp.where(pick, aT_s[ks, :], 0.0), axis=1, keepdims=True)
            kcol = jnp.sum(jnp.where(pick, kT_s[ks, :], 0.0), axis=1, keepdims=True)
            qcol = jnp.sum(jnp.where(pick, qT_s[ks, :], 0.0), axis=1, keepdims=True)
            s1 = acol * s_ref[j, hd] + kcol * v_rows[j:j + 1, :]
            sout_ref[j, hd] = s1
            o_rows = jnp.where(sub == j, jnp.sum(qcol * s1, axis=0, keepdims=True), o_rows)
        o_s[pl.ds(t0, TB), vs] = o_rows

    @pl.when(step == pl.num_programs(0) - 1)
    def _():
        nw = nw_ref[...]
        outs = []
        for hd in range(n_heads):
            vs = slice(hd * V, (hd + 1) * V)
            o = o_s[:, vs]
            g = g_s[:, vs]
            gate = _silu(g) if kind == "gla" else jax.nn.sigmoid(g)
            o = o * lax.rsqrt(jnp.mean(o * o, axis=-1, keepdims=True) + EPS) * nw * gate
            outs.append(o.astype(BF16))
        y_ref[...] = _dot(jnp.concatenate(outs, axis=1), wp_ref[...])


def _branch_sample(kind, x, mod, weights, norm_w, w_proj, state, layer):
    NT, D = x.shape
    _, n_heads, K, V = state.shape
    HK, HV = n_heads * K, n_heads * V
    TB = SUBLANES
    c1 = lambda s: pl.BlockSpec(s, lambda i: (0,) * len(s))
    nw2 = norm_w.reshape(1, V)
    in_specs = ([c1((NT, D)), pl.BlockSpec((NT, D), lambda i: (0, 0)), pl.BlockSpec((NT, D), lambda i: (0, 1))]
                + [_w_spec(w) for w in weights] + [c1(nw2.shape), c1(w_proj.shape),
                                                    pl.BlockSpec((TB, n_heads, K, V), lambda i: (i, 0, 0, 0))])
    kern = functools.partial(_sample_kernel, kind=kind, n_heads=n_heads, K=K, V=V, TB=TB, layer=layer)
    return pl.pallas_call(
        kern,
        grid=(NT // TB,),
        in_specs=in_specs,
        out_specs=[c1((NT, D)), pl.BlockSpec((TB, n_heads, K, V), lambda i: (i, 0, 0, 0))],
        out_shape=[jax.ShapeDtypeStruct((NT, D), F32), jax.ShapeDtypeStruct(state.shape, F32)],
        scratch_shapes=[pltpu.VMEM((HK, NT), F32), pltpu.VMEM((HK, NT), F32), pltpu.VMEM((HK, NT), F32),
                        pltpu.VMEM((NT, HV), F32), pltpu.VMEM((NT, HV), F32), pltpu.VMEM((NT, HV), F32)],
        compiler_params=_cparams(("arbitrary",)),
        name=f"{kind}_sample",
    )(x, mod, mod, *[_w_array(w) for w in weights], nw2, w_proj, state)


def _merge_kernel(x_ref, ya_ref, yb_ref, sh_ref, sc_ref, g_ref, wu_ref, wo_ref, lg_ref, lb_ref, o_ref, *, alpha):
    x = x_ref[0]
    D = x.shape[-1]
    h = (x * (1.0 + sc_ref[0]) + sh_ref[0]).astype(BF16)
    u = _dot(h, wu_ref[...])
    merged = jax.nn.sigmoid(u[:, :D]) * ya_ref[0] + jax.nn.sigmoid(u[:, D:]) * yb_ref[0]
    mix = _dot(merged.astype(BF16), wo_ref[...])
    o_ref[0] = _layernorm(alpha * x + g_ref[0] * mix, lg_ref[...], lb_ref[...])


def _mod_specs(mod3, cols, T):
    D = mod3.shape[-1] // 6
    if mod3.shape[1] == 1:
        return [pl.BlockSpec((1, 1, D), functools.partial(lambda b, l, *_, c: (b, 0, c), c=c)) for c in cols]
    return [pl.BlockSpec((1, T, D), functools.partial(lambda b, l, *_, c: (b, l, c), c=c)) for c in cols]


def _merge(x, ya, yb, mod3, wu, w_out, ln_g, ln_b, alpha):
    B, L, D = x.shape
    T = min(TOKEN_TILE, L)
    tok = pl.BlockSpec((1, T, D), lambda b, l: (b, l, 0))
    return pl.pallas_call(
        functools.partial(_merge_kernel, alpha=alpha),
        grid=(B, L // T),
        in_specs=[tok, tok, tok] + _mod_specs(mod3, (0, 1, 2), T)
        + [_w_spec(wu), _const_spec(w_out.shape), _const_spec((1, D)), _const_spec((1, D))],
        out_specs=tok,
        out_shape=jax.ShapeDtypeStruct((B, L, D), F32),
        compiler_params=_cparams(("arbitrary", "arbitrary")),
        name="merge",
    )(x, ya, yb, mod3, mod3, mod3, _w_array(wu), w_out, ln_g.reshape(1, D), ln_b.reshape(1, D))


def _first_argmax(vals, iota, n, axis):
    m = jnp.max(vals, axis=axis, keepdims=True)
    idx = jnp.min(jnp.where(vals == m, iota, n), axis=axis, keepdims=True)
    return m, idx


def _router_kernel(x_ref, sh_ref, sc_ref, wrT_ref, bias_ref, eidx_ref, egate_ref, xg_ref, *, n_experts):
    E = n_experts
    per = E // N_GROUPS
    hf = x_ref[0] * (1.0 + sc_ref[0]) + sh_ref[0]
    h = hf.astype(BF16)
    T = h.shape[0]
    for s in range(hf.shape[1] // LANES):
        xg_ref[pl.ds(s, T, stride=SUBLANES), :] = hf[:, s * LANES:(s + 1) * LANES]
    scores = jax.nn.sigmoid(_dot_nt(wrT_ref[...], h))
    biased = scores + bias_ref[...]
    b3 = biased.reshape(N_GROUPS, per, T)
    i3 = lax.broadcasted_iota(jnp.int32, (N_GROUPS, per, T), 1)
    m1, a1 = _first_argmax(b3, i3, per, 1)
    m2 = jnp.max(jnp.where(i3 == a1, -jnp.inf, b3), axis=1, keepdims=True)
    gscore = (m1 + m2).reshape(N_GROUPS, T)
    gi = lax.broadcasted_iota(jnp.int32, (N_GROUPS, T), 0)
    gsel = jnp.zeros((N_GROUPS, T), jnp.bool_)
    for _ in range(TOPK_GROUPS):
        _, a = _first_argmax(gscore, gi, N_GROUPS, 0)
        hit = gi == a
        gsel = jnp.logical_or(gsel, hit)
        gscore = jnp.where(hit, -jnp.inf, gscore)
    emask = jnp.broadcast_to(gsel.reshape(N_GROUPS, 1, T), (N_GROUPS, per, T)).reshape(E, T)
    cand = jnp.where(emask, biased, -jnp.inf)
    ei = lax.broadcasted_iota(jnp.int32, (E, T), 0)
    picks, weights = [], []
    for _ in range(TOP_K):
        _, a = _first_argmax(cand, ei, E, 0)
        hit = ei == a
        picks.append(a)
        weights.append(jnp.sum(jnp.where(hit, scores, 0.0), axis=0, keepdims=True))
        cand = jnp.where(hit, -jnp.inf, cand)
    w = jnp.concatenate(weights, axis=0)
    egate_ref[...] = w / jnp.sum(w, axis=0, keepdims=True) * ROUTED_SCALE
    eidx_ref[...] = jnp.concatenate(picks, axis=0)


def _merge_router_kernel(x_ref, ya_ref, yb_ref, sh_ref, sc_ref, g_ref, wu_ref, wo_ref, lg_ref, lb_ref,
                         sh2_ref, sc2_ref, wrT_ref, bias_ref, o_ref, eidx_ref, egate_ref, xg_ref,
                         *, alpha, n_experts):
    _merge_kernel(x_ref, ya_ref, yb_ref, sh_ref, sc_ref, g_ref, wu_ref, wo_ref, lg_ref, lb_ref, o_ref, alpha=alpha)
    _router_kernel(o_ref, sh2_ref, sc2_ref, wrT_ref, bias_ref, eidx_ref, egate_ref, xg_ref, n_experts=n_experts)


def _merge_route(x, ya, yb, mod3, wu, w_out, ln_g, ln_b, alpha, wrT, bias):
    B, L, D = x.shape
    assert D == SUBLANES * LANES
    E = wrT.shape[0]
    T = min(TOKEN_TILE, L)
    nl = L // T
    N = B * L
    tok = pl.BlockSpec((1, T, D), lambda b, l: (b, l, 0))
    pick_spec = pl.BlockSpec((TOP_K, T), lambda b, l: (0, b * nl + l))
    return pl.pallas_call(
        functools.partial(_merge_router_kernel, alpha=alpha, n_experts=E),
        grid=(B, nl),
        in_specs=[tok, tok, tok] + _mod_specs(mod3, (0, 1, 2), T)
        + [_w_spec(wu), _const_spec(w_out.shape), _const_spec((1, D)), _const_spec((1, D))]
        + _mod_specs(mod3, (3, 4), T) + [_const_spec(wrT.shape), _const_spec((E, 1))],
        out_specs=[tok, pick_spec, pick_spec, pl.BlockSpec((T * SUBLANES, LANES), lambda b, l: (b * nl + l, 0))],
        out_shape=[jax.ShapeDtypeStruct((B, L, D), F32), jax.ShapeDtypeStruct((TOP_K, N), jnp.int32),
                   jax.ShapeDtypeStruct((TOP_K, N), F32), jax.ShapeDtypeStruct((N * SUBLANES, LANES), F32)],
        compiler_params=_cparams(("arbitrary", "arbitrary")),
        name="merge_route",
    )(x, ya, yb, mod3, mod3, mod3, _w_array(wu), w_out, ln_g.reshape(1, D), ln_b.reshape(1, D),
      mod3, mod3, wrT, bias.reshape(E, 1))


def _router(x1, mod3, wrT, bias):
    B, L, D = x1.shape
    assert D == SUBLANES * LANES
    E = wrT.shape[0]
    T = min(TOKEN_TILE, L)
    nl = L // T
    N = B * L
    tok = pl.BlockSpec((1, T, D), lambda b, l: (b, l, 0))
    pick_spec = pl.BlockSpec((TOP_K, T), lambda b, l: (0, b * nl + l))
    return pl.pallas_call(
        functools.partial(_router_kernel, n_experts=E),
        grid=(B, nl),
        in_specs=[tok] + _mod_specs(mod3, (3, 4), T) + [_const_spec(wrT.shape), _const_spec((E, 1))],
        out_specs=[pick_spec, pick_spec, pl.BlockSpec((T * SUBLANES, LANES), lambda b, l: (b * nl + l, 0))],
        out_shape=[jax.ShapeDtypeStruct((TOP_K, N), jnp.int32), jax.ShapeDtypeStruct((TOP_K, N), F32),
                   jax.ShapeDtypeStruct((N * SUBLANES, LANES), F32)],
        compiler_params=_cparams(("arbitrary", "arbitrary")),
        name="router",
    )(x1, mod3, mod3, wrT, bias.reshape(E, 1))


def _tile_schedule(eidx, egate, NB, E, R):
    Kp, N = eidx.shape
    nb = N // NB
    A = Kp * NB
    S = SUBLANES
    assert A % R == 0
    NW = A // R
    tok = jnp.broadcast_to(jnp.arange(N, dtype=jnp.int32)[None, :], (Kp, N))
    assert nb * E * NB < 2 ** 31
    key = ((tok // NB) * E + eidx) * NB + tok % NB
    skey, sgate = lax.sort((key.reshape(-1), egate.reshape(-1)), num_keys=1)
    w_e = ((skey // NB) % E).reshape(nb, NW, R)
    w_t = ((skey % NB) * S).reshape(nb, NW, R)
    w_g = sgate.reshape(nb, NW, R)
    first = w_e[:, :, 0]
    npair = w_e[:, :, R - 1] - first + 1
    cum = jnp.cumsum(npair, axis=1)
    ntiles = cum[:, -1]
    SL = NW + E + 3
    q = jnp.arange(SL, dtype=jnp.int32)[None, :] - 2
    qc = jnp.clip(q, 0, ntiles[:, None] - 1)
    k_q = jnp.minimum(jnp.sum((cum[:, None, :] <= qc[:, :, None]).astype(jnp.int32), axis=2), NW - 1)
    onehot = (k_q[:, :, None] == jnp.arange(NW, dtype=jnp.int32)[None, None, :]).astype(F32)
    sel = lambda a: jnp.einsum('bsk,bkr->bsr', onehot, a.astype(F32), precision=lax.Precision.HIGHEST)
    selk = lambda a: jnp.sum(onehot * a.astype(F32)[:, None, :], axis=2).astype(jnp.int32)
    e_q = jnp.clip(selk(first) + qc - selk(cum - npair), 0, E - 1)
    real = jnp.logical_and(q >= 0, q < ntiles[:, None])
    match = jnp.logical_and(real[:, :, None], sel(w_e).astype(jnp.int32) == e_q[:, :, None])
    rows = jnp.where(match, sel(w_t).astype(jnp.int32), NB * S).reshape(nb * SL, 1, R)
    gate = jnp.where(match, sel(w_g), 0.0).reshape(nb * SL, 1, R)
    return e_q.reshape(-1), ntiles, rows, gate, SL


def _moe_step(src_ref, dst_ref, gate_ref, xg_s, acc_s, wg_ref, wu_ref, wd_ref, gbuf, cbuf, cy, sy, R):
    S = SUBLANES
    for r in range(R):
        t0 = pl.multiple_of(src_ref[0, 0, r], S)
        gbuf[r * S:(r + 1) * S, :] = xg_s[pl.ds(t0, S), :]

    x = jnp.concatenate([cbuf[pl.ds(s, R, stride=S), :] for s in range(S)], axis=1).astype(BF16)
    a = _silu(_dot(x, wg_ref[0])) * _dot(x, wu_ref[0])
    y = _dot(a.astype(BF16), wd_ref[0])
    for s in range(S):
        cy[pl.ds(s, R, stride=S), :] = y[:, s * LANES:(s + 1) * LANES]

    for r0 in range(0, R, RMW_BATCH):
        upd = []
        for r in range(r0, r0 + RMW_BATCH):
            a0 = pl.multiple_of(dst_ref[0, 0, r], S)
            upd.append((a0, acc_s[pl.ds(a0, S), :] + gate_ref[0, 0, r] * sy[r * S:(r + 1) * S, :]))
        for a0, val in upd:
            acc_s[pl.ds(a0, S), :] = val


def _moe_kernel(te_ref, nt_ref, src_ref, dst_ref, gate_ref, xg_hbm, wg_ref, wu_ref, wd_ref, out_hbm,
                xg_s, acc_s, buf0, buf1, y0, y1, *, NB, R):
    del te_ref
    b = pl.program_id(0)
    q = pl.program_id(1)

    @pl.when(jnp.logical_and(b == 0, q == 0))
    def _():
        for ref in (buf0, buf1, y0, y1):
            ref[...] = jnp.zeros_like(ref)

    @pl.when(q == 0)
    def _():
        pltpu.sync_copy(xg_hbm.at[b], xg_s.at[pl.ds(0, NB * SUBLANES)])
        xg_s[pl.ds(NB * SUBLANES, SUBLANES), :] = jnp.zeros((SUBLANES, LANES), F32)
        acc_s[...] = jnp.zeros_like(acc_s)

    active = q < nt_ref[b] + 2
    args = (src_ref, dst_ref, gate_ref, xg_s, acc_s, wg_ref, wu_ref, wd_ref)

    @pl.when(jnp.logical_and(active, q % 2 == 0))
    def _():
        _moe_step(*args, buf0, buf1, y1, y0, R)

    @pl.when(jnp.logical_and(active, q % 2 == 1))
    def _():
        _moe_step(*args, buf1, buf0, y0, y1, R)

    @pl.when(q == pl.num_programs(1) - 1)
    def _():
        pltpu.sync_copy(acc_s.at[pl.ds(0, NB * SUBLANES)], out_hbm.at[b])


def _moe_routed(xg, eidx, egate, wg, wu, wd):
    N = eidx.shape[1]
    E, D, DE = wg.shape
    NB = min(MOE_BLOCK, N)
    nb = N // NB
    S = SUBLANES
    R = MOE_ROWS if NB >= MOE_BLOCK else MOE_ROWS_SMALL
    te, ntiles, rows, gate, SL = _tile_schedule(eidx, egate, NB, E, R)
    smem = lambda shift: pl.BlockSpec((1, 1, R), lambda b, q, *_: (b * SL + q + shift, 0, 0),
                                      memory_space=pltpu.SMEM)
    w_map = lambda b, q, te_ref, nt_ref: (te_ref[b * SL + q + 1], 0, 0)
    tile_rows = pltpu.VMEM((R * S, LANES), F32)
    grid_spec = pltpu.PrefetchScalarGridSpec(
        num_scalar_prefetch=2,
        grid=(nb, SL - 2),
        in_specs=[smem(2), smem(0), smem(0),
                  pl.BlockSpec(memory_space=pl.ANY),
                  pl.BlockSpec((1, D, DE), w_map), pl.BlockSpec((1, D, DE), w_map), pl.BlockSpec((1, DE, D), w_map)],
        out_specs=pl.BlockSpec(memory_space=pl.ANY),
        scratch_shapes=[pltpu.VMEM(((NB + 1) * S, LANES), F32), pltpu.VMEM(((NB + 1) * S, LANES), F32),
                        tile_rows, tile_rows, tile_rows, tile_rows],
    )
    return pl.pallas_call(
        functools.partial(_moe_kernel, NB=NB, R=R),
        grid_spec=grid_spec,
        out_shape=jax.ShapeDtypeStruct((nb, NB * S, LANES), F32),
        compiler_params=_cparams(("arbitrary", "arbitrary")),
        name="moe_routed",
    )(te, ntiles, rows, rows, gate, xg.reshape(nb, NB * S, LANES), wg, wu, wd)


def _combine_kernel(x_ref, sh_ref, sc_ref, g2_ref, r_ref, sg_ref, su_ref, sd_ref, lg_ref, lb_ref, o_ref, *, alpha):
    x = x_ref[0]
    T = x.shape[0]
    h = (x * (1.0 + sc_ref[0]) + sh_ref[0]).astype(BF16)
    a = _silu(_dot(h, sg_ref[...])) * _dot(h, su_ref[...])
    shared = _dot(a.astype(BF16), sd_ref[...])
    routed = jnp.concatenate([r_ref[0, pl.ds(s, T, stride=SUBLANES), :] for s in range(SUBLANES)], axis=1)
    o_ref[0] = _layernorm(alpha * x + g2_ref[0] * (routed + shared), lg_ref[...], lb_ref[...])


def _combine(x1, mod3, routed, sg, su, sd, ln_g, ln_b, alpha):
    B, L, D = x1.shape
    T = min(TOKEN_TILE, L)
    nl = L // T
    NB = routed.shape[1] // SUBLANES
    per = NB // T
    tok = pl.BlockSpec((1, T, D), lambda b, l: (b, l, 0))
    r_spec = pl.BlockSpec((1, T * SUBLANES, LANES), lambda b, l: ((b * nl + l) // per, (b * nl + l) % per, 0))
    return pl.pallas_call(
        functools.partial(_combine_kernel, alpha=alpha),
        grid=(B, nl),
        in_specs=[tok] + _mod_specs(mod3, (3, 4, 5), T) + [r_spec]
        + [_const_spec(sg.shape), _const_spec(su.shape), _const_spec(sd.shape), _const_spec((1, D)),
           _const_spec((1, D))],
        out_specs=tok,
        out_shape=jax.ShapeDtypeStruct((B, L, D), F32),
        compiler_params=_cparams(("arbitrary", "arbitrary")),
        name="combine",
    )(x1, mod3, mod3, mod3, routed, sg, su, sd, ln_g.reshape(1, D), ln_b.reshape(1, D))


def _split_w_in(w_in_l, gla_shape, hgrn_shape, D):
    Hg, Kg, Vg = gla_shape
    Hh, Kh, Vh = hgrn_shape
    rank = w_in_l.shape[1] - (2 * Hg * Kg + 2 * Hg * Vg + 2 * Hh * Kh + 2 * Hh * Vh + 2 * D)
    gla_w = (Hg * Kg, Hg * Kg, Hg * Vg, Hg * Vg)
    hgrn_w = (Hh * Kh, Hh * Kh, Hh * Vh, Hh * Vh, 2 * D)
    n_lo = sum(gla_w)
    lo = w_in_l[:, :n_lo].astype(BF16)
    mid = w_in_l[:, n_lo:n_lo + rank].astype(BF16)
    hi = w_in_l[:, n_lo + rank:].astype(BF16)
    out = []
    for arr, widths in ((lo, gla_w), (hi, hgrn_w)):
        start = 0
        for w in widths:
            out.append(_Cols(arr, start, w))
            start += w
    return out[:4] + [mid] + out[4:]


def kernel(x_prompt, x_sample, state_gla, state_hgrn, c_prompt, c_sample, w_ada, b_ada, w_in, w_gk2, b_gk,
           hgrn_lb, gla_norm_w, hgrn_norm_w, w_proj_a, w_proj_b, w_out, ln1_g, ln1_b, w_router, router_bias,
           w_exp_gate, w_exp_up, w_exp_down, w_sh_gate, w_sh_up, w_sh_down, ln2_g, ln2_b):
    depth = w_in.shape[0]
    BP, L, D = x_prompt.shape
    NS = x_sample.shape[0]
    assert x_sample.shape[1] == 1
    gla_shape = state_gla.shape[2:]
    hgrn_shape = state_hgrn.shape[2:]
    alpha = (2.0 * depth) ** 0.25

    xp = x_prompt
    xs = x_sample.reshape(NS, D)
    c_all = jnp.concatenate([c_prompt, c_sample], axis=0)
    new_gla_p, new_hgrn_p, new_gla_s, new_hgrn_s = [], [], [], []
    for l in range(depth):
        mod = _ada_mod(c_all, w_ada[l], b_ada[l])
        mod_p = mod[:BP].reshape(BP, 1, 6 * D)
        mod_s = mod[BP:]
        (wqa, wka, wva, wga, wgk1, wqb, wfb, wib, wgb, wuab) = _split_w_in(w_in[l], gla_shape, hgrn_shape, D)
        gla_w = [wqa, wka, wva, wga, wgk1, w_gk2[l].astype(BF16), b_gk[l].reshape(1, -1)]
        hgrn_w = [wqb, wfb, wib, wgb, hgrn_lb]
        wpa = w_proj_a[l].astype(BF16)
        wpb = w_proj_b[l].astype(BF16)
        wo = w_out[l].astype(BF16)
        wrT = w_router[l].T.astype(BF16)
        eg, eu, ed = w_exp_gate[l].astype(BF16), w_exp_up[l].astype(BF16), w_exp_down[l].astype(BF16)
        sg, su, sd = w_sh_gate[l].astype(BF16), w_sh_up[l].astype(BF16), w_sh_down[l].astype(BF16)

        def tail(x3, ya, yb, mod3):
            x1, eidx, egate, xg = _merge_route(x3, ya, yb, mod3, wuab, wo, ln1_g[l], ln1_b[l], alpha,
                                               wrT, router_bias[l])
            routed = _moe_routed(xg, eidx, egate, eg, eu, ed)
            return _combine(x1, mod3, routed, sg, su, sd, ln2_g[l], ln2_b[l], alpha)

        ya, sg_p = _branch_prompt("gla", xp, mod_p, gla_w, gla_norm_w[l], wpa, *gla_shape, layer=l)
        yb, sh_p = _branch_prompt("hgrn", xp, mod_p, hgrn_w, hgrn_norm_w[l], wpb, *hgrn_shape, layer=l)
        xp = tail(xp, ya, yb, mod_p)
        new_gla_p.append(sg_p)
        new_hgrn_p.append(sh_p)

        ya, sg_s = _branch_sample("gla", xs, mod_s, gla_w, gla_norm_w[l], wpa, state_gla[l], layer=l)
        yb, sh_s = _branch_sample("hgrn", xs, mod_s, hgrn_w, hgrn_norm_w[l], wpb, state_hgrn[l], layer=l)
        xs = tail(xs[None], ya[None], yb[None], mod_s[None])[0]
        new_gla_s.append(sg_s)
        new_hgrn_s.append(sh_s)

    return (xp, xs.reshape(NS, 1, D), jnp.stack(new_gla_p), jnp.stack(new_hgrn_p),
            jnp.stack(new_gla_s), jnp.stack(new_hgrn_s))
```
